```python
import math
import jax
import jax.numpy as jnp
from jax import lax
import numpy as np

D_MODEL = 1024
BATCH = 16
SEQ = 2048
DEPTH = 2

GRID_W = 64
CTX_LEN = 256
N_EVEN = (DEPTH + 1) // 2
N_ODD = DEPTH // 2
NORM_EPS = 1e-6

A_HEADS = 12
A_KV_HEADS = 4
A_HEAD_DIM = 64
A_GROUP = A_HEADS // A_KV_HEADS
A_Q_W = A_HEADS * A_HEAD_DIM
A_KV_W = A_KV_HEADS * A_HEAD_DIM
ROPE_THETA = 10000.0
Q_BLOCK = 128

B_DIM = 512
B_CONV_WIDTH = 31

EVEN_IN = A_Q_W + 2 * A_KV_W + 2 * B_DIM
EVEN_MIX = A_Q_W + B_DIM

C_DIM = 512
C_ORDER = 2
C_SHORT = 3
C_EMB = 33
C_FILTER_HIDDEN = 64
C_FAST_DECAY = 0.3
C_SLOW_DECAY = 1.5
C_DECAY_TARGET = 1e-2
C_IN = (C_ORDER + 1) * C_DIM

D_HEADS = 4
D_KEY_DIM = 128
D_VAL_DIM = 128
D_FORGET = D_HEADS * D_KEY_DIM
D_DIM = D_HEADS * D_VAL_DIM
D_CHUNK = 32
D_IN = 3 * D_FORGET + 2 * D_DIM

ODD_IN = C_IN + D_IN
ODD_MIX = C_DIM + D_DIM

D_FF = 2816
FFN_CONV_WIDTH = 3

kernel_name = 'hybrid_dit_gqa_conformer_hyena_hgrn2'


def rms_norm(x, g):
    x32 = x.astype(jnp.float32)
    y = x32 * lax.rsqrt(jnp.mean(x32 * x32, axis=-1, keepdims=True) + NORM_EPS)
    return y.astype(x.dtype) * g


def layer_norm(x, g, b):
    x32 = x.astype(jnp.float32)
    mu = jnp.mean(x32, axis=-1, keepdims=True)
    var = jnp.mean(jnp.square(x32 - mu), axis=-1, keepdims=True)
    return ((x32 - mu) * lax.rsqrt(var + NORM_EPS)).astype(x.dtype) * g + b


def depthwise_conv(x, w, b):
    y = lax.conv_general_dilated(x, w[:, None, :].astype(x.dtype), (1,), 'SAME',
                                 dimension_numbers=('NWC', 'WIO', 'NWC'),
                                 feature_group_count=x.shape[-1])
    return y + b


def axial_rope(L):
    rows = L // GRID_W
    row = jnp.repeat(jnp.arange(rows, dtype=jnp.float32), GRID_W)
    col = jnp.tile(jnp.arange(GRID_W, dtype=jnp.float32), rows)
    axis_dim = A_HEAD_DIM // 2
    inv_freq = ROPE_THETA ** (-jnp.arange(0, axis_dim, 2, dtype=jnp.float32) / axis_dim)
    ang = jnp.concatenate([row[:, None] * inv_freq, col[:, None] * inv_freq], axis=-1)
    return jnp.cos(ang), jnp.sin(ang)


def apply_rope(x, cos, sin):
    xp = x.astype(jnp.float32).reshape(x.shape[:-1] + (A_HEAD_DIM // 2, 2))
    x0, x1 = xp[..., 0], xp[..., 1]
    c = cos[None, :, None, :]
    s = sin[None, :, None, :]
    out = jnp.stack([x0 * c - x1 * s, x0 * s + x1 * c], axis=-1)
    return out.reshape(x.shape).astype(x.dtype)


def block_attention(q, k, v):
    B, Lq = q.shape[:2]
    nb = Lq // Q_BLOCK
    qb = (q * (A_HEAD_DIM ** -0.5)).reshape(B, nb, Q_BLOCK, A_KV_HEADS, A_GROUP, A_HEAD_DIM)
    qb = qb.transpose(1, 0, 2, 3, 4, 5)

    def one_block(q_blk):
        s = jnp.einsum('bqhgd,bkhd->bhgqk', q_blk, k).astype(jnp.float32)
        p = jax.nn.softmax(s, axis=-1).astype(v.dtype)
        return jnp.einsum('bhgqk,bkhd->bqhgd', p, v)

    o = lax.map(one_block, qb)
    return o.transpose(1, 0, 2, 3, 4, 5).reshape(B, Lq, A_Q_W)


def conformer_conv(p, w, b, g, beta):
    a, gate = jnp.split(p, 2, axis=-1)
    u = depthwise_conv(a * jax.nn.sigmoid(gate), w, b)
    return jax.nn.silu(layer_norm(u, g, beta))


def even_mixer(h_lat, h_ctx, w_in, q_gain, k_gain, conv_w, conv_b, ln_g, ln_b, w_out, cos, sin, need_ctx):
    def split(p):
        B, L = p.shape[:2]
        q, k, v, glu = jnp.split(p, [A_Q_W, A_Q_W + A_KV_W, A_Q_W + 2 * A_KV_W], axis=-1)
        return (rms_norm(q.reshape(B, L, A_HEADS, A_HEAD_DIM), q_gain),
                rms_norm(k.reshape(B, L, A_KV_HEADS, A_HEAD_DIM), k_gain),
                v.reshape(B, L, A_KV_HEADS, A_HEAD_DIM), glu)

    q_l, k_l, v_l, glu_l = split(h_lat @ w_in)
    q_c, k_c, v_c, glu_c = split(h_ctx @ w_in)
    q_l = apply_rope(q_l, cos, sin)
    k_l = apply_rope(k_l, cos, sin)
    attn_l = block_attention(q_l, jnp.concatenate([k_c, k_l], axis=1), jnp.concatenate([v_c, v_l], axis=1))
    conv_l = conformer_conv(glu_l, conv_w, conv_b, ln_g, ln_b)
    y_lat = jnp.concatenate([attn_l, conv_l], axis=-1) @ w_out
    y_ctx = None
    if need_ctx:
        attn_c = block_attention(q_c, k_c, v_c)
        conv_c = conformer_conv(glu_c, conv_w, conv_b, ln_g, ln_b)
        y_ctx = jnp.concatenate([attn_c, conv_c], axis=-1) @ w_out
    return y_lat, y_ctx


def hyena_filter_spectra(L, w1, b1, w2, b2, w3, freq):
    f32 = jnp.float32
    t = jnp.linspace(0.0, 1.0, L, dtype=f32)[:, None]
    bands = (C_EMB - 1) // 2
    fb = jnp.linspace(1e-4, bands - 1, bands, dtype=f32)
    wpos = (2.0 * math.pi / L) * jnp.arange(L, dtype=f32)[:, None]
    feats = jnp.concatenate([t, jnp.cos(fb * wpos), -jnp.sin(fb * wpos)], axis=-1)
    fr = freq.astype(f32)
    hdn = jnp.sin(fr * (feats @ w1.astype(f32) + b1.astype(f32)))
    hdn = jnp.sin(fr * (hdn @ w2.astype(f32) + b2.astype(f32)))
    h = (hdn @ w3.astype(f32)).reshape(L, C_ORDER, 2, C_DIM)
    deltas = jnp.abs(jnp.linspace(math.log(C_DECAY_TARGET) / C_SLOW_DECAY,
                                  math.log(C_DECAY_TARGET) / C_FAST_DECAY, C_DIM, dtype=f32))
    h = h * jnp.exp(-t * deltas)[:, None, None, :]
    filt = jnp.concatenate([h[:, :, 0], jnp.zeros((1, C_ORDER, C_DIM), f32), h[:0:-1, :, 1]], axis=0)
    filt = filt / jnp.sum(jnp.abs(filt), axis=0, keepdims=True)
    return jnp.fft.rfft(filt, axis=0)


def fft_long_conv(u, spec):
    L = u.shape[1]
    U = jnp.fft.rfft(u.astype(jnp.float32), n=2 * L, axis=1)
    y = jnp.fft.irfft(U * spec[None], n=2 * L, axis=1)[:, :L]
    return y.astype(u.dtype)


def hyena(p, short_w, short_b, w1, b1, w2, b2, w3, freq, skip):
    L = p.shape[1]
    u = depthwise_conv(p, short_w, short_b)
    v, x1, x2 = jnp.split(u, 3, axis=-1)
    spec = hyena_filter_spectra(L, w1, b1, w2, b2, w3, freq)
    z = v
    for o, gate in enumerate((x1, x2)):
        z = gate * (fft_long_conv(z, spec[:, o]) + skip[o] * z)
    return z


def hgrn2_project(p, lb):
    B, L = p.shape[:2]
    q, f_f, f_b, i, g = jnp.split(p, [D_FORGET, 2 * D_FORGET, 3 * D_FORGET, 3 * D_FORGET + D_DIM], axis=-1)

    def heads(t, d):
        return t.astype(jnp.float32).reshape(B, L, D_HEADS, d)

    dirs = []
    for f_raw, lb_d in ((f_f, lb[0]), (f_b, lb[1])):
        f = lb_d + (1.0 - lb_d) * jax.nn.sigmoid(f_raw.astype(jnp.float32))
        dirs.append((heads(1.0 - f, D_KEY_DIM), heads(jnp.log(f), D_KEY_DIM)))
    return heads(q, D_KEY_DIM), dirs, heads(i, D_VAL_DIM), g


def hgrn2_chunk_scan(q, k, v, logf, s0):
    B, L = q.shape[:2]
    n = L // D_CHUNK

    def chunks(t):
        return t.reshape(B, n, D_CHUNK, D_HEADS, t.shape[-1]).transpose(1, 0, 3, 2, 4)

    q, k, v, logf = chunks(q), chunks(k), chunks(v), chunks(logf)
    b = jnp.cumsum(logf, axis=-2)
    b_last = b[..., -1:, :]
    q_dec = q * jnp.exp(b)
    scores = jnp.einsum('nbhtd,nbhsd->nbhts', q_dec, k * jnp.exp(-b))
    lower_tri = jnp.tril(jnp.ones((D_CHUNK, D_CHUNK), dtype=bool))
    scores = jnp.where(lower_tri, scores, 0.0)
    o_intra = jnp.einsum('nbhts,nbhse->nbhte', scores, v)
    upd = jnp.einsum('nbhsd,nbhse->nbhde', k * jnp.exp(b_last - b), v)
    chunk_decay = jnp.exp(b_last[..., 0, :])

    def step(s, xs):
        q_i, u_i, d_i = xs
        o_i = jnp.einsum('bhtd,bhde->bhte', q_i, s)
        return d_i[..., None] * s + u_i, o_i

    s_fin, o_inter = lax.scan(step, s0, (q_dec, upd, chunk_decay))
    o = (o_intra + o_inter).transpose(1, 0, 3, 2, 4).reshape(B, L, D_HEADS, v.shape[-1])
    return o, s_fin


def hgrn2_readout(o, g, gn_g):
    B, L = o.shape[:2]
    o = rms_norm(o, gn_g.astype(jnp.float32)).reshape(B, L, D_DIM)
    return (o * jax.nn.silu(g.astype(jnp.float32))).astype(g.dtype)


def hgrn2_mixer(p_lat, p_ctx, lb, gn_g, need_ctx):
    q_l, dirs_l, i_l, g_l = hgrn2_project(p_lat, lb)
    q_c, dirs_c, i_c, g_c = hgrn2_project(p_ctx, lb)
    (kf_l, lgf_l), (kb_l, lgb_l) = dirs_l
    (kf_c, lgf_c), (kb_c, lgb_c) = dirs_c

    def flip(t):
        return jnp.flip(t, axis=1)

    s0 = jnp.zeros((p_lat.shape[0], D_HEADS, D_KEY_DIM, D_VAL_DIM), jnp.float32)
    o_cf, s_cf = hgrn2_chunk_scan(q_c, kf_c, i_c, lgf_c, s0)
    o_cb, s_cb = hgrn2_chunk_scan(flip(q_c), flip(kb_c), flip(i_c), flip(lgb_c), s0)
    o_lf, _ = hgrn2_chunk_scan(q_l, kf_l, i_l, lgf_l, s_cf)
    o_lb, _ = hgrn2_chunk_scan(flip(q_l), flip(kb_l), flip(i_l), flip(lgb_l), s_cb)
    y_lat = hgrn2_readout(o_lf + flip(o_lb), g_l, gn_g)
    y_ctx = hgrn2_readout(o_cf + flip(o_cb), g_c, gn_g) if need_ctx else None
    return y_lat, y_ctx


def odd_mixer(h_lat, h_ctx, w_in, short_w, short_b, w1, b1, w2, b2, w3, freq, skip, lb, gn_g, w_out, need_ctx):
    p_lat = h_lat @ w_in
    p_ctx = h_ctx @ w_in
    d_lat, d_ctx = hgrn2_mixer(p_lat[..., C_IN:], p_ctx[..., C_IN:], lb, gn_g, need_ctx)
    c_lat = hyena(p_lat[..., :C_IN], short_w, short_b, w1, b1, w2, b2, w3, freq, skip)
    y_lat = jnp.concatenate([c_lat, d_lat], axis=-1) @ w_out
    y_ctx = None
    if need_ctx:
        c_ctx_out = hyena(p_ctx[..., :C_IN], short_w, short_b, w1, b1, w2, b2, w3, freq, skip)
        y_ctx = jnp.concatenate([c_ctx_out, d_ctx], axis=-1) @ w_out
    return y_lat, y_ctx


def conv_ffn(h, w_up, cw, cb, w_down):
    u = depthwise_conv(h @ w_up, cw, cb)
    val, gate = jnp.split(u, 2, axis=-1)
    return (jax.nn.silu(gate) * val) @ w_down


def setup_inputs(seed: int = 0) -> dict:
    key = jax.random.key(seed)
    ks = iter(jax.random.split(key, 40))
    D = D_MODEL

    def nrm(shape, scale):
        return jax.random.normal(next(ks), shape, jnp.float32) * scale

    def gain(shape):
        return 1.0 + nrm(shape, 0.02)

    return {
        'x': nrm((BATCH, SEQ, D), 1.0),
        'c': nrm((BATCH, D), 1.0),
        'ctx': nrm((BATCH, CTX_LEN, D), 1.0),
        'c_ctx': nrm((D,), 1.0),
        'ada_w': nrm((DEPTH, D, 6 * D), 0.5 * D ** -0.5),
        'ada_b': nrm((DEPTH, 6 * D), 0.01),
        'norm1_g': gain((DEPTH, D)),
        'norm2_g': gain((DEPTH, D)),
        'final_g': gain((D,)),
        'ev_w_in': nrm((N_EVEN, D, EVEN_IN), D ** -0.5),
        'ev_q_gain': gain((N_EVEN, A_HEAD_DIM)),
        'ev_k_gain': gain((N_EVEN, A_HEAD_DIM)),
        'ev_conv_w': nrm((N_EVEN, B_CONV_WIDTH, B_DIM), B_CONV_WIDTH ** -0.5),
        'ev_conv_b': nrm((N_EVEN, B_DIM), 0.01),
        'ev_ln_g': gain((N_EVEN, B_DIM)),
        'ev_ln_b': nrm((N_EVEN, B_DIM), 0.01),
        'ev_w_out': nrm((N_EVEN, EVEN_MIX, D), EVEN_MIX ** -0.5),
        'od_w_in': nrm((N_ODD, D, ODD_IN), D ** -0.5),
        'od_short_w': nrm((N_ODD, C_SHORT, C_IN), C_SHORT ** -0.5),
        'od_short_b': nrm((N_ODD, C_IN), 0.01),
        'od_filt_w1': nrm((N_ODD, C_EMB, C_FILTER_HIDDEN), C_EMB ** -0.5),
        'od_filt_b1': nrm((N_ODD, C_FILTER_HIDDEN), 0.1),
        'od_filt_w2': nrm((N_ODD, C_FILTER_HIDDEN, C_FILTER_HIDDEN), C_FILTER_HIDDEN ** -0.5),
        'od_filt_b2': nrm((N_ODD, C_FILTER_HIDDEN), 0.1),
        'od_filt_w3': nrm((N_ODD, C_FILTER_HIDDEN, 2 * C_ORDER * C_DIM), C_FILTER_HIDDEN ** -0.5),
        'od_filt_freq': gain((N_ODD, C_FILTER_HIDDEN)),
        'od_hyena_skip': nrm((N_ODD, C_ORDER, C_DIM), 0.5),
        'od_lower_bound': nrm((DEPTH, 2, D_FORGET), 0.1),
        'od_gnorm_g': gain((N_ODD, D_VAL_DIM)),
        'od_w_out': nrm((N_ODD, ODD_MIX, D), ODD_MIX ** -0.5),
        'ffn_w_up': nrm((DEPTH, D, 2 * D_FF), D ** -0.5),
        'ffn_conv_w': nrm((DEPTH, FFN_CONV_WIDTH, 2 * D_FF), FFN_CONV_WIDTH ** -0.5),
        'ffn_conv_b': nrm((DEPTH, 2 * D_FF), 0.01),
        'ffn_w_down': nrm((DEPTH, D_FF, D), D_FF ** -0.5),
    }


def reference(x, c, ctx, c_ctx, ada_w, ada_b, norm1_g, norm2_g, final_g, ev_w_in, ev_q_gain, ev_k_gain,
              ev_conv_w, ev_conv_b, ev_ln_g, ev_ln_b, ev_w_out, od_w_in, od_short_w, od_short_b,
              od_filt_w1, od_filt_b1, od_filt_w2, od_filt_b2, od_filt_w3, od_filt_freq, od_hyena_skip,
              od_lower_bound, od_gnorm_g, od_w_out, ffn_w_up, ffn_conv_w, ffn_conv_b, ffn_w_down):
    L = x.shape[1]
    cos, sin = axial_rope(L)
    lb_soft = jax.nn.softmax(od_lower_bound.astype(jnp.float32), axis=0)
    lower_bounds = jnp.cumsum(lb_soft, axis=0) - lb_soft[0:1]
    sc = jax.nn.silu(c)
    scc = jax.nn.silu(c_ctx)
    for l in range(DEPTH):
        last = l == DEPTH - 1
        j = l // 2
        mod_l = jnp.split((sc @ ada_w[l] + ada_b[l])[:, None, :], 6, axis=-1)
        mod_c = jnp.split(scc @ ada_w[l] + ada_b[l], 6, axis=-1)
        h_lat = rms_norm(x, norm1_g[l]) * (1 + mod_l[1]) + mod_l[0]
        h_ctx = rms_norm(ctx, norm1_g[l]) * (1 + mod_c[1]) + mod_c[0]
        if l % 2 == 0:
            y_lat, y_ctx = even_mixer(h_lat, h_ctx, ev_w_in[j], ev_q_gain[j], ev_k_gain[j], ev_conv_w[j],
                                      ev_conv_b[j], ev_ln_g[j], ev_ln_b[j], ev_w_out[j], cos, sin, not last)
        else:
            y_lat, y_ctx = odd_mixer(h_lat, h_ctx, od_w_in[j], od_short_w[j], od_short_b[j], od_filt_w1[j],
                                     od_filt_b1[j], od_filt_w2[j], od_filt_b2[j], od_filt_w3[j], od_filt_freq[j],
                                     od_hyena_skip[j], lower_bounds[l], od_gnorm_g[j], od_w_out[j], not last)
        x = x + mod_l[2] * y_lat
        h = rms_norm(x, norm2_g[l]) * (1 + mod_l[4]) + mod_l[3]
        x = x + mod_l[5] * conv_ffn(h, ffn_w_up[l], ffn_conv_w[l], ffn_conv_b[l], ffn_w_down[l])
        if not last:
            ctx = ctx + mod_c[2] * y_ctx
            hc = rms_norm(ctx, norm2_g[l]) * (1 + mod_c[4]) + mod_c[3]
            ctx = ctx + mod_c[5] * conv_ffn(hc, ffn_w_up[l], ffn_conv_w[l], ffn_conv_b[l], ffn_w_down[l])
    return rms_norm(x, final_g)
```

```python
import functools
import math

import jax
import jax.numpy as jnp
from jax import lax
from jax.experimental import pallas as pl
from jax.experimental.pallas import tpu as pltpu

F32 = jnp.float32
BF16 = jnp.bfloat16

NORM_EPS = 1e-6
GRID_W = 64
ROPE_THETA = 10000.0
HEAD_DIM = 64
N_Q_HEADS = 12
N_KV_HEADS = 4
Q_W = N_Q_HEADS * HEAD_DIM
KV_W = N_KV_HEADS * HEAD_DIM
CONF_DIM = 512
HY_DIM = 512
HY_IN = 3 * HY_DIM
HG_HEADS = 4
HG_DIM = 128
HG_W = HG_HEADS * HG_DIM
HG_CHUNK = 32
C_FAST_DECAY = 0.3
C_SLOW_DECAY = 1.5
C_DECAY_TARGET = 1e-2
V7X_VMEM_LIMIT = 56 * 1024 * 1024
LANES = 128


def _cparams(sem):
    return pltpu.CompilerParams(dimension_semantics=sem, vmem_limit_bytes=V7X_VMEM_LIMIT)


def _tile(n, prefs):
    for p in prefs:
        if n % p == 0:
            return p
    return n


def _norm_mod(x, gs, sh):
    r = lax.rsqrt(jnp.mean(x * x, axis=-1, keepdims=True) + NORM_EPS)
    return x * r * gs + sh


def _ada_kernel(c_ref, w_ref, b_ref, o_ref):
    c = c_ref[...]
    s = (c * jax.nn.sigmoid(c)).astype(BF16)
    o_ref[...] = jnp.dot(s, w_ref[...].astype(BF16), preferred_element_type=F32) + b_ref[...]


def ada_modulation(cs, w, b):
    R, D = cs.shape
    N = w.shape[1]
    tn = _tile(N, (1024, 512, 256, 128))
    return pl.pallas_call(
        _ada_kernel,
        grid=(N // tn,),
        in_specs=[pl.BlockSpec((R, D), lambda j: (0, 0)),
                  pl.BlockSpec((D, tn), lambda j: (0, j)),
                  pl.BlockSpec((1, tn), lambda j: (0, j))],
        out_specs=pl.BlockSpec((R, tn), lambda j: (0, j)),
        out_shape=jax.ShapeDtypeStruct((R, N), F32),
        compiler_params=_cparams(("arbitrary",)),
        name="ada_modulation",
    )(cs, w, b.reshape(1, N))


def _nm_mm_kernel(x_ref, gs_ref, sh_ref, w_ref, o_ref, h_ref):
    @pl.when(pl.program_id(2) == 0)
    def _():
        h_ref[...] = _norm_mod(x_ref[0], gs_ref[0], sh_ref[0]).astype(BF16)

    o_ref[0] = jnp.dot(h_ref[...], w_ref[...], preferred_element_type=F32).astype(o_ref.dtype)


def norm_mod_matmul(x, gs, sh, w, out_dtype=F32):
    B, L, D = x.shape
    N = w.shape[1]
    tm = _tile(L, (512, 256, 128))
    tn = _tile(N, (1024, 512, 256, 128))
    return pl.pallas_call(
        _nm_mm_kernel,
        grid=(B, L // tm, N // tn),
        in_specs=[pl.BlockSpec((1, tm, D), lambda b, i, j: (b, i, 0)),
                  pl.BlockSpec((1, 1, D), lambda b, i, j: (b, 0, 0)),
                  pl.BlockSpec((1, 1, D), lambda b, i, j: (b, 0, 0)),
                  pl.BlockSpec((D, tn), lambda b, i, j: (0, j))],
        out_specs=pl.BlockSpec((1, tm, tn), lambda b, i, j: (b, i, j)),
        out_shape=jax.ShapeDtypeStruct((B, L, N), out_dtype),
        scratch_shapes=[pltpu.VMEM((tm, D), BF16)],
        compiler_params=_cparams(("parallel", "parallel", "arbitrary")),
        name="norm_mod_matmul",
    )(x, gs, sh, w)


def _even_in_kernel(x_ref, gs_ref, sh_ref, w_ref, seg_ref, cq_ref, sq_ref, ck_ref, sk_ref,
                    q_ref, k_ref, v_ref, glu_ref):
    tm = x_ref.shape[1]
    h = _norm_mod(x_ref[0], gs_ref[0], sh_ref[0]).astype(BF16)
    seg = seg_ref[...]
    lane = lax.broadcasted_iota(jnp.int32, (tm, LANES), 1)
    even = (lane & 1) == 0

    def head_norm_rope(col0, c_ref, s_ref, out_ref, ocol0):
        y = jnp.dot(h, w_ref[:, col0:col0 + 256], preferred_element_type=F32)
        sq = y * y
        hi = sq.astype(BF16)
        lo = (sq - hi.astype(F32)).astype(BF16)
        ss = (jnp.dot(hi, seg, preferred_element_type=F32)
              + jnp.dot(lo, seg, preferred_element_type=F32))
        yn = y * lax.rsqrt(ss * (1.0 / HEAD_DIM) + NORM_EPS)
        for j in range(2):
            c = yn[:, LANES * j:LANES * (j + 1)]
            sw = jnp.where(even, pltpu.roll(c, LANES - 1, 1), pltpu.roll(c, 1, 1))
            out_ref[0, :, ocol0 + LANES * j:ocol0 + LANES * (j + 1)] = (
                c * c_ref[...] + sw * s_ref[...]).astype(out_ref.dtype)

    for t in range(Q_W // 256):
        head_norm_rope(256 * t, cq_ref, sq_ref, q_ref, 256 * t)
    for t in range(KV_W // 256):
        head_norm_rope(Q_W + 256 * t, ck_ref, sk_ref, k_ref, 256 * t)
    v_ref[0] = jnp.dot(h, w_ref[:, Q_W + KV_W:Q_W + 2 * KV_W],
                       preferred_element_type=F32).astype(v_ref.dtype)
    g0 = Q_W + 2 * KV_W
    a = jnp.dot(h, w_ref[:, g0:g0 + CONF_DIM], preferred_element_type=F32)
    gate = jnp.dot(h, w_ref[:, g0 + CONF_DIM:g0 + 2 * CONF_DIM], preferred_element_type=F32)
    glu_ref[0] = a * jax.nn.sigmoid(gate)


def even_in_proj(x, gs, sh, w, seg, cq, sq, ck, sk):
    B, L, D = x.shape
    N = w.shape[1]
    tm = _tile(L, (256, 128))
    row = lambda b, i: (b, i, 0)
    tab = pl.BlockSpec((tm, LANES), lambda b, i: (i, 0))
    return pl.pallas_call(
        _even_in_kernel,
        grid=(B, L // tm),
        in_specs=[pl.BlockSpec((1, tm, D), row),
                  pl.BlockSpec((1, 1, D), lambda b, i: (b, 0, 0)),
                  pl.BlockSpec((1, 1, D), lambda b, i: (b, 0, 0)),
                  pl.BlockSpec((D, N), lambda b, i: (0, 0)),
                  pl.BlockSpec((256, 256), lambda b, i: (0, 0)),
                  tab, tab, tab, tab],
        out_specs=[pl.BlockSpec((1, tm, Q_W), row),
                   pl.BlockSpec((1, tm, KV_W), row),
                   pl.BlockSpec((1, tm, KV_W), row),
                   pl.BlockSpec((1, tm, CONF_DIM), row)],
        out_shape=[jax.ShapeDtypeStruct((B, L, Q_W), BF16),
                   jax.ShapeDtypeStruct((B, L, KV_W), BF16),
                   jax.ShapeDtypeStruct((B, L, KV_W), BF16),
                   jax.ShapeDtypeStruct((B, L, CONF_DIM), F32)],
        compiler_params=_cparams(("parallel", "parallel")),
        name="even_in_proj",
    )(x, gs, sh, w, seg, cq, sq, ck, sk)


def _attn_kernel(q_ref, klo_ref, khi_ref, vlo_ref, vhi_ref, o_ref):
    group = N_Q_HEADS // N_KV_HEADS
    for pr in range(N_Q_HEADS // 2):
        q2 = q_ref[0, :, LANES * pr:LANES * (pr + 1)]
        acc = None
        for pos, (k_ref, v_ref) in enumerate(((klo_ref, vlo_ref), (khi_ref, vhi_ref))):
            g = (2 * pr + pos) // group
            kg = k_ref[0, :, LANES * g:LANES * (g + 1)]
            vg = v_ref[0, :, LANES * g:LANES * (g + 1)]
            s = lax.dot_general(q2, kg, (((1,), (1,)), ((), ())), preferred_element_type=F32)
            m = jnp.max(s, axis=-1, keepdims=True)
            p = jnp.exp(s - m)
            l = jnp.sum(p, axis=-1, keepdims=True)
            o = jnp.dot(p.astype(BF16), vg, preferred_element_type=F32) / l
            acc = o if acc is None else acc + o
        o_ref[0, :, LANES * pr:LANES * (pr + 1)] = acc.astype(o_ref.dtype)


def _pad_heads(t):
    B, Lk, _ = t.shape
    t4 = t.reshape(B, Lk, N_KV_HEADS, HEAD_DIM)
    z = jnp.zeros_like(t4)
    lo = jnp.concatenate([t4, z], axis=-1).reshape(B, Lk, N_KV_HEADS * LANES)
    hi = jnp.concatenate([z, t4], axis=-1).reshape(B, Lk, N_KV_HEADS * LANES)
    return lo, hi


def attention(q, k, v):
    B, Lq, _ = q.shape
    Lk = k.shape[1]
    tq = _tile(Lq, (256, 128))
    klo, khi = _pad_heads(k)
    vlo, vhi = _pad_heads(v)
    kv = pl.BlockSpec((1, Lk, N_KV_HEADS * LANES), lambda b, i: (b, 0, 0))
    return pl.pallas_call(
        _attn_kernel,
        grid=(B, Lq // tq),
        in_specs=[pl.BlockSpec((1, tq, Q_W), lambda b, i: (b, i, 0)), kv, kv, kv, kv],
        out_specs=pl.BlockSpec((1, tq, Q_W), lambda b, i: (b, i, 0)),
        out_shape=jax.ShapeDtypeStruct((B, Lq, Q_W), BF16),
        compiler_params=_cparams(("parallel", "parallel")),
        name="attention",
    )(q, klo, khi, vlo, vhi)


_CONV_PAD = 16


def _dwconv_kernel(x_ref, w_ref, b_ref, g_ref, beta_ref, o_ref, xp_ref, *, K, TR, ln_swish):
    L = x_ref.shape[1]
    C = x_ref.shape[2]
    half = (K - 1) // 2
    xp_ref[0:_CONV_PAD, :] = jnp.zeros((_CONV_PAD, C), F32)
    xp_ref[_CONV_PAD + L:, :] = jnp.zeros((_CONV_PAD, C), F32)
    xp_ref[_CONV_PAD:_CONV_PAD + L, :] = x_ref[0]

    W = TR + 2 * _CONV_PAD

    def tile(t, carry):
        r0 = pl.multiple_of(t * TR, TR)
        win = xp_ref[pl.ds(r0, W), :]
        acc = jnp.zeros((TR, C), F32) + b_ref[...]
        rolled = {0: win}
        for k in range(K):
            q8, r8 = divmod(_CONV_PAD - half + k, 8)
            if r8 not in rolled:
                rolled[r8] = pltpu.roll(win, W - r8, 0)
            acc = acc + rolled[r8][8 * q8:8 * q8 + TR, :] * w_ref[k:k + 1, :]
        if ln_swish:
            mu = jnp.mean(acc, axis=-1, keepdims=True)
            d = acc - mu
            var = jnp.mean(d * d, axis=-1, keepdims=True)
            y = d * lax.rsqrt(var + NORM_EPS) * g_ref[...] + beta_ref[...]
            acc = y * jax.nn.sigmoid(y)
        o_ref[0, pl.ds(r0, TR), :] = acc.astype(o_ref.dtype)
        return carry

    lax.fori_loop(0, L // TR, tile, 0)


def dwconv(x, w, b, g=None, beta=None, *, out_dtype=F32):
    B, L, _ = x.shape
    K, C = w.shape
    ln_swish = g is not None
    tc = C if ln_swish else _tile(C, (512, 256, 128))
    TR = _tile(L, (32, 16, 8))
    if g is None:
        g = jnp.ones((C,), F32)
        beta = jnp.zeros((C,), F32)
    vec = pl.BlockSpec((1, tc), lambda bb, j: (0, j))
    return pl.pallas_call(
        functools.partial(_dwconv_kernel, K=K, TR=TR, ln_swish=ln_swish),
        grid=(B, C // tc),
        in_specs=[pl.BlockSpec((1, L, tc), lambda bb, j: (bb, 0, j)),
                  pl.BlockSpec((K, tc), lambda bb, j: (0, j)),
                  vec, vec, vec],
        out_specs=pl.BlockSpec((1, L, tc), lambda bb, j: (bb, 0, j)),
        out_shape=jax.ShapeDtypeStruct((B, L, C), out_dtype),
        scratch_shapes=[pltpu.VMEM((L + 2 * _CONV_PAD, tc), F32)],
        compiler_params=_cparams(("parallel", "parallel")),
        name="dwconv%d" % K,
    )(x, w, b.reshape(1, C), g.reshape(1, C), beta.reshape(1, C))


def _out_proj_kernel(x_ref, a1_ref, a2_ref, w1_ref, w2_ref, g_ref, o_ref):
    y = jnp.dot(a1_ref[0], w1_ref[...], preferred_element_type=F32)
    y = y + jnp.dot(a2_ref[0], w2_ref[...], preferred_element_type=F32)
    o_ref[0] = x_ref[0] + g_ref[0] * y


def out_proj_residual(x, a1, a2, w1, w2, gate):
    B, L, D = x.shape
    K1, K2 = a1.shape[2], a2.shape[2]
    tm = _tile(L, (512, 256, 128))
    row = lambda b, i: (b, i, 0)
    return pl.pallas_call(
        _out_proj_kernel,
        grid=(B, L // tm),
        in_specs=[pl.BlockSpec((1, tm, D), row),
                  pl.BlockSpec((1, tm, K1), row),
                  pl.BlockSpec((1, tm, K2), row),
                  pl.BlockSpec((K1, D), lambda b, i: (0, 0)),
                  pl.BlockSpec((K2, D), lambda b, i: (0, 0)),
                  pl.BlockSpec((1, 1, D), lambda b, i: (b, 0, 0))],
        out_specs=pl.BlockSpec((1, tm, D), row),
        out_shape=jax.ShapeDtypeStruct((B, L, D), F32),
        compiler_params=_cparams(("parallel", "parallel")),
        name="out_proj_residual",
    )(x, a1, a2, w1, w2, gate)


_FFN_CHUNK = 256


def _ffn_kernel(x_ref, xp_ref, xn_ref, gs_ref, sh_ref, wup_ref, cw_ref, cb_ref, wdn_ref, g5_ref,
                fg_ref, o_ref, acc_ref, *, final_norm):
    tm = x_ref.shape[1]
    FF = wdn_ref.shape[0]
    i = pl.program_id(1)
    has_prev = (i > 0).astype(F32)
    has_next = (i < pl.num_programs(1) - 1).astype(F32)
    gs = gs_ref[0]
    sh = sh_ref[0]
    x = x_ref[0]
    h = _norm_mod(x, gs, sh).astype(BF16)
    halo = jnp.concatenate([_norm_mod(xp_ref[0], gs, sh), _norm_mod(xn_ref[0], gs, sh)],
                           axis=0).astype(BF16)
    row = lax.broadcasted_iota(jnp.int32, (tm, _FFN_CHUNK), 0)
    first = row == 0
    last = row == tm - 1

    def conv3(c0):
        w = wup_ref[:, c0:c0 + _FFN_CHUNK]
        p = jnp.dot(h, w, preferred_element_type=F32)
        ph = jnp.dot(halo, w, preferred_element_type=F32)
        prev_row = ph[7:8, :] * has_prev
        next_row = ph[8:9, :] * has_next
        pm1 = jnp.where(first, prev_row, pltpu.roll(p, 1, 0))
        pp1 = jnp.where(last, next_row, pltpu.roll(p, tm - 1, 0))
        return (cw_ref[0:1, c0:c0 + _FFN_CHUNK] * pm1 + cw_ref[1:2, c0:c0 + _FFN_CHUNK] * p
                + cw_ref[2:3, c0:c0 + _FFN_CHUNK] * pp1 + cb_ref[:, c0:c0 + _FFN_CHUNK])

    for c in range(FF // _FFN_CHUNK):
        val = conv3(c * _FFN_CHUNK)
        gate = conv3(FF + c * _FFN_CHUNK)
        act = (gate * jax.nn.sigmoid(gate) * val).astype(BF16)
        part = jnp.dot(act, wdn_ref[c * _FFN_CHUNK:(c + 1) * _FFN_CHUNK, :],
                       preferred_element_type=F32)
        if c == 0:
            acc_ref[...] = part
        else:
            acc_ref[...] += part
    y = x + g5_ref[0] * acc_ref[...]
    if final_norm:
        y = y * lax.rsqrt(jnp.mean(y * y, axis=-1, keepdims=True) + NORM_EPS) * fg_ref[...]
    o_ref[0] = y


def conv_ffn_residual(x, gs, sh, w_up, cw, cb, w_dn, gate, final_g=None):
    B, L, D = x.shape
    FF = w_dn.shape[0]
    assert FF % _FFN_CHUNK == 0
    tm = _tile(L, (512, 256, 128))
    nb8 = L // 8
    t8 = tm // 8
    final_norm = final_g is not None
    if final_g is None:
        final_g = jnp.ones((D,), F32)
    row = lambda b, i: (b, i, 0)
    const = lambda b, i: (0, 0)
    return pl.pallas_call(
        functools.partial(_ffn_kernel, final_norm=final_norm),
        grid=(B, L // tm),
        in_specs=[pl.BlockSpec((1, tm, D), row),
                  pl.BlockSpec((1, 8, D), lambda b, i: (b, jnp.maximum(i * t8 - 1, 0), 0)),
                  pl.BlockSpec((1, 8, D), lambda b, i: (b, jnp.minimum((i + 1) * t8, nb8 - 1), 0)),
                  pl.BlockSpec((1, 1, D), lambda b, i: (b, 0, 0)),
                  pl.BlockSpec((1, 1, D), lambda b, i: (b, 0, 0)),
                  pl.BlockSpec((D, 2 * FF), const, pipeline_mode=pl.Buffered(1)),
                  pl.BlockSpec((3, 2 * FF), const),
                  pl.BlockSpec((1, 2 * FF), const),
                  pl.BlockSpec((FF, D), const, pipeline_mode=pl.Buffered(1)),
                  pl.BlockSpec((1, 1, D), lambda b, i: (b, 0, 0)),
                  pl.BlockSpec((1, D), const)],
        out_specs=pl.BlockSpec((1, tm, D), row),
        out_shape=jax.ShapeDtypeStruct((B, L, D), F32),
        scratch_shapes=[pltpu.VMEM((tm, D), F32)],
        compiler_params=_cparams(("parallel", "arbitrary")),
        name="conv_ffn",
    )(x, x, x, gs, sh, w_up, cw, cb.reshape(1, 2 * FF), w_dn, gate, final_g.reshape(1, D))


def _hy_filter_kernel(feat_ref, w1_ref, b1_ref, w2_ref, b2_ref, w3_ref, fr_ref, dec_ref,
                      hs_ref, hd_ref, nyq_ref):
    L = feat_ref.shape[0]
    fr = fr_ref[...]
    z = jnp.dot(feat_ref[...].astype(BF16), w1_ref[...].astype(BF16), preferred_element_type=F32)
    hdn = jnp.sin(fr * (z + b1_ref[...]))
    z = jnp.dot(hdn.astype(BF16), w2_ref[...].astype(BF16), preferred_element_type=F32)
    hdn = jnp.sin(fr * (z + b2_ref[...])).astype(BF16)
    row = lax.broadcasted_iota(jnp.int32, (L, HY_DIM), 0)
    not_first = (row > 0).astype(F32)
    sign = jnp.where((row & 1) == 0, 1.0, -1.0).astype(F32)
    dec = dec_ref[...]
    n_order = w3_ref.shape[1] // (2 * HY_DIM)
    for o in range(n_order):
        c0 = o * 2 * HY_DIM
        hf = jnp.dot(hdn, w3_ref[:, c0:c0 + HY_DIM].astype(BF16), preferred_element_type=F32) * dec
        hb = (jnp.dot(hdn, w3_ref[:, c0 + HY_DIM:c0 + 2 * HY_DIM].astype(BF16),
                      preferred_element_type=F32) * dec * not_first)
        inv = 1.0 / (jnp.sum(jnp.abs(hf), axis=0, keepdims=True)
                     + jnp.sum(jnp.abs(hb), axis=0, keepdims=True))
        hs = (hf + hb) * inv
        hd = (hf - hb) * inv
        hs_ref[:, o * HY_DIM:(o + 1) * HY_DIM] = hs
        hd_ref[:, o * HY_DIM:(o + 1) * HY_DIM] = hd
        nyq_ref[:, o * HY_DIM:(o + 1) * HY_DIM] = jnp.sum(hs * sign, axis=0, keepdims=True)


def hyena_filter_taps(L, w1, b1, w2, b2, w3, freq):
    emb = w1.shape[0]
    bands = (emb - 1) // 2
    t = jnp.linspace(0.0, 1.0, L, dtype=F32)[:, None]
    fb = jnp.linspace(1e-4, bands - 1, bands, dtype=F32)
    wpos = (2.0 * math.pi / L) * jnp.arange(L, dtype=F32)[:, None]
    feats = jnp.concatenate([t, jnp.cos(fb * wpos), -jnp.sin(fb * wpos)], axis=-1)
    feats = jnp.pad(feats, ((0, 0), (0, LANES - emb)))
    w1p = jnp.pad(w1, ((0, LANES - emb), (0, 0)))
    deltas = jnp.abs(jnp.linspace(math.log(C_DECAY_TARGET) / C_SLOW_DECAY,
                                  math.log(C_DECAY_TARGET) / C_FAST_DECAY, HY_DIM, dtype=F32))
    dec = jnp.exp(-t * deltas)
    hid = w1.shape[1]
    NO = w3.shape[1] // 2
    full = lambda shape: pl.BlockSpec(shape, lambda: tuple(0 for _ in shape))
    return pl.pallas_call(
        _hy_filter_kernel,
        in_specs=[full((L, LANES)), full((LANES, hid)), full((1, hid)), full((hid, hid)),
                  full((1, hid)), full((hid, 2 * NO)), full((1, hid)), full((L, HY_DIM))],
        out_specs=[full((L, NO)), full((L, NO)), full((1, NO))],
        out_shape=[jax.ShapeDtypeStruct((L, NO), F32), jax.ShapeDtypeStruct((L, NO), F32),
                   jax.ShapeDtypeStruct((1, NO), F32)],
        compiler_params=pltpu.CompilerParams(vmem_limit_bytes=V7X_VMEM_LIMIT),
        name="hyena_filter",
    )(feats, w1p, b1.reshape(1, hid), w2, b2.reshape(1, hid), w3, freq.reshape(1, hid), dec)


def _dft_tables(L):
    N = 2 * L
    k = lax.broadcasted_iota(jnp.int32, (L, L), 0)
    n = lax.broadcasted_iota(jnp.int32, (L, L), 1)
    ang = ((k * n) & (N - 1)).astype(F32) * (2.0 * math.pi / N)
    re = jnp.cos(ang)
    im = -jnp.sin(ang)
    nyq = jnp.where((n[0:1] & 1) == 0, 1.0, -1.0).astype(F32)
    im = jnp.concatenate([nyq, im[1:]], axis=0)
    f = jnp.concatenate([re, im], axis=0).astype(BF16)
    return f, f.T


def _hy_spec_kernel(fre_ref, fim_ref, hs_ref, hd_ref, nyq_ref, a_ref, b_ref):
    kb = fre_ref.shape[0]
    L = fre_ref.shape[1]
    NO = hs_ref.shape[1]
    krow = lax.broadcasted_iota(jnp.int32, (kb, NO), 0) + pl.program_id(0) * kb
    dc = krow == 0
    scale = jnp.where(dc, 1.0, 2.0).astype(F32) * (1.0 / (2 * L))
    hr = jnp.dot(fre_ref[...], hs_ref[...].astype(BF16), preferred_element_type=F32)
    hi = jnp.dot(fim_ref[...], hd_ref[...].astype(BF16), preferred_element_type=F32)
    a_ref[...] = hr * scale
    b_ref[...] = jnp.where(dc, nyq_ref[...], hi) * scale


def hyena_spectrum(F, hs, hd, nyq):
    L, NO = hs.shape
    kb = _tile(L, (256, 128))
    nk = L // kb
    return pl.pallas_call(
        _hy_spec_kernel,
        grid=(nk,),
        in_specs=[pl.BlockSpec((kb, L), lambda i: (i, 0)),
                  pl.BlockSpec((kb, L), lambda i: (i + nk, 0)),
                  pl.BlockSpec((L, NO), lambda i: (0, 0)),
                  pl.BlockSpec((L, NO), lambda i: (0, 0)),
                  pl.BlockSpec((1, NO), lambda i: (0, 0))],
        out_specs=[pl.BlockSpec((kb, NO), lambda i: (i, 0)),
                   pl.BlockSpec((kb, NO), lambda i: (i, 0))],
        out_shape=[jax.ShapeDtypeStruct((L, NO), F32), jax.ShapeDtypeStruct((L, NO), F32)],
        compiler_params=_cparams(("arbitrary",)),
        name="hyena_spectrum",
    )(F, F, hs, hd, nyq)


def _hy_conv_kernel(z_ref, gate_ref, fre_ref, fim_ref, gre_ref, gim_ref, a_ref, b_ref, skip_ref,
                    o_ref, zb_ref, y_ref):
    kb = fre_ref.shape[0]
    C = z_ref.shape[2]
    kstep = pl.program_id(1)

    @pl.when(kstep == 0)
    def _():
        zb_ref[...] = z_ref[0].astype(BF16)

    zb = zb_ref[...]
    ur = jnp.dot(fre_ref[...], zb, preferred_element_type=F32)
    ui = jnp.dot(fim_ref[...], zb, preferred_element_type=F32)
    a = a_ref[...]
    b = b_ref[...]
    krow = lax.broadcasted_iota(jnp.int32, (kb, C), 0) + kstep * kb
    dc = krow == 0
    vr = jnp.where(dc, ur * a, ur * a - ui * b)
    vi = jnp.where(dc, ui * b, ur * b + ui * a)
    part = (jnp.dot(gre_ref[...], vr.astype(BF16), preferred_element_type=F32)
            + jnp.dot(gim_ref[...], vi.astype(BF16), preferred_element_type=F32))

    @pl.when(kstep == 0)
    def _():
        y_ref[...] = part

    @pl.when(kstep > 0)
    def _():
        y_ref[...] += part

    @pl.when(kstep == pl.num_programs(1) - 1)
    def _():
        o_ref[0] = (gate_ref[0] * (y_ref[...] + skip_ref[...] * z_ref[0])).astype(o_ref.dtype)


def hyena_conv(z_src, z_col, gate_src, gate_col, F, G, A, Bt, spec_col, skip, out_dtype):
    B, L, _ = z_src.shape
    C = HY_DIM
    kb = _tile(L, (256, 128))
    nk = L // kb
    return pl.pallas_call(
        _hy_conv_kernel,
        grid=(B, nk),
        in_specs=[pl.BlockSpec((1, L, C), lambda b, k: (b, 0, z_col)),
                  pl.BlockSpec((1, L, C), lambda b, k: (b, 0, gate_col)),
                  pl.BlockSpec((kb, L), lambda b, k: (k, 0)),
                  pl.BlockSpec((kb, L), lambda b, k: (k + nk, 0)),
                  pl.BlockSpec((L, kb), lambda b, k: (0, k)),
                  pl.BlockSpec((L, kb), lambda b, k: (0, k + nk)),
                  pl.BlockSpec((kb, C), lambda b, k: (k, spec_col)),
                  pl.BlockSpec((kb, C), lambda b, k: (k, spec_col)),
                  pl.BlockSpec((1, C), lambda b, k: (0, 0))],
        out_specs=pl.BlockSpec((1, L, C), lambda b, k: (b, 0, 0)),
        out_shape=jax.ShapeDtypeStruct((B, L, C), out_dtype),
        scratch_shapes=[pltpu.VMEM((L, C), BF16), pltpu.VMEM((L, C), F32)],
        compiler_params=_cparams(("parallel", "arbitrary")),
        name="hyena_conv",
    )(z_src, gate_src, F, F, G, G, A, Bt, skip.reshape(1, C))


def _hgrn2_kernel(ql_ref, ffl_ref, fbl_ref, il_ref, gl_ref, qc_ref, ffc_ref, fbc_ref, ic_ref,
                  lb_ref, gn_ref, o_ref, of_ref, ob_ref):
    C = HG_CHUNK
    L = ql_ref.shape[1]
    Lc = qc_ref.shape[1]
    r = lax.broadcasted_iota(jnp.int32, (C, C), 0)
    c = lax.broadcasted_iota(jnp.int32, (C, C), 1)
    tril = (c <= r)
    triu = (c >= r)
    tril_f = tril.astype(F32)
    triu_f = triu.astype(F32)
    lb_f = lb_ref[0:1, :]
    lb_b = lb_ref[1:2, :]

    def chunk(q, fraw, v, st, lbd, mask, mask_f, last_row):
        f = lbd + (1.0 - lbd) * jax.nn.sigmoid(fraw)
        k = 1.0 - f
        lg = jnp.log(f)
        b = jnp.dot(mask_f, lg, preferred_element_type=F32, precision=lax.Precision.HIGHEST)
        b_last = b[last_row:last_row + 1, :]
        qd = (q * jnp.exp(b)).astype(BF16)
        kd = (k * jnp.exp(-b)).astype(BF16)
        vb = v.astype(BF16)
        sc = lax.dot_general(qd, kd, (((1,), (1,)), ((), ())), preferred_element_type=F32)
        sc = jnp.where(mask, sc, 0.0).astype(BF16)
        o = jnp.dot(sc, vb, preferred_element_type=F32)
        o = o + lax.dot_general(qd, st.astype(BF16), (((1,), (1,)), ((), ())),
                                preferred_element_type=F32)
        ku = (k * jnp.exp(b_last - b)).astype(BF16)
        upd = lax.dot_general(vb, ku, (((0,), (0,)), ((), ())), preferred_element_type=F32)
        return o, st * jnp.exp(b_last) + upd

    def ctx_step(j, carry):
        sf, sb = carry
        rf = pl.multiple_of(j * C, C)
        rb = pl.multiple_of(Lc - C - j * C, C)
        _, sf = chunk(qc_ref[0, pl.ds(rf, C), :], ffc_ref[0, pl.ds(rf, C), :],
                      ic_ref[0, pl.ds(rf, C), :], sf, lb_f, tril, tril_f, C - 1)
        _, sb = chunk(qc_ref[0, pl.ds(rb, C), :], fbc_ref[0, pl.ds(rb, C), :],
                      ic_ref[0, pl.ds(rb, C), :], sb, lb_b, triu, triu_f, 0)
        return sf, sb

    def lat_step(j, carry):
        sf, sb = carry
        rf = pl.multiple_of(j * C, C)
        rb = pl.multiple_of(L - C - j * C, C)
        o, sf = chunk(ql_ref[0, pl.ds(rf, C), :], ffl_ref[0, pl.ds(rf, C), :],
                      il_ref[0, pl.ds(rf, C), :], sf, lb_f, tril, tril_f, C - 1)
        of_ref[pl.ds(rf, C), :] = o
        o, sb = chunk(ql_ref[0, pl.ds(rb, C), :], fbl_ref[0, pl.ds(rb, C), :],
                      il_ref[0, pl.ds(rb, C), :], sb, lb_b, triu, triu_f, 0)
        ob_ref[pl.ds(rb, C), :] = o
        return sf, sb

    zero = jnp.zeros((HG_DIM, HG_DIM), F32)
    carry = lax.fori_loop(0, Lc // C, ctx_step, (zero, zero))
    lax.fori_loop(0, L // C, lat_step, carry)
    o = of_ref[...] + ob_ref[...]
    o = o * lax.rsqrt(jnp.mean(o * o, axis=-1, keepdims=True) + NORM_EPS) * gn_ref[...]
    g = gl_ref[0]
    o_ref[0] = (o * (g * jax.nn.sigmoid(g))).astype(o_ref.dtype)


def hgrn2_mixer(p_lat, lat_col0, p_ctx, ctx_col0, lb, gn_g):
    B, L, _ = p_lat.shape
    Lc = p_ctx.shape[1]
    lc0 = lat_col0 // HG_DIM
    cc0 = ctx_col0 // HG_DIM

    def lat(seg):
        return pl.BlockSpec((1, L, HG_DIM), lambda b, h: (b, 0, lc0 + HG_HEADS * seg + h))

    def ctx(seg):
        return pl.BlockSpec((1, Lc, HG_DIM), lambda b, h: (b, 0, cc0 + HG_HEADS * seg + h))

    return pl.pallas_call(
        _hgrn2_kernel,
        grid=(B, HG_HEADS),
        in_specs=[lat(0), lat(1), lat(2), lat(3), lat(4), ctx(0), ctx(1), ctx(2), ctx(3),
                  pl.BlockSpec((2, HG_DIM), lambda b, h: (0, h)),
                  pl.BlockSpec((1, HG_DIM), lambda b, h: (0, 0))],
        out_specs=pl.BlockSpec((1, L, HG_DIM), lambda b, h: (b, 0, h)),
        out_shape=jax.ShapeDtypeStruct((B, L, HG_W), BF16),
        scratch_shapes=[pltpu.VMEM((L, HG_DIM), F32), pltpu.VMEM((L, HG_DIM), F32)],
        compiler_params=_cparams(("parallel", "parallel")),
        name="hgrn2",
    )(p_lat, p_lat, p_lat, p_lat, p_lat, p_ctx, p_ctx, p_ctx, p_ctx, lb, gn_g.reshape(1, HG_DIM))


def _rope_tables(L, gain, scale, rotate):
    g2 = jnp.tile(gain.astype(F32), LANES // HEAD_DIM)[None, :] * scale
    gsw = g2.reshape(1, LANES // 2, 2)[:, :, ::-1].reshape(1, LANES)
    if not rotate:
        return jnp.broadcast_to(g2, (L, LANES)), jnp.zeros((L, LANES), F32)
    rows = L // GRID_W
    row = jnp.repeat(jnp.arange(rows, dtype=F32), GRID_W)
    col = jnp.tile(jnp.arange(GRID_W, dtype=F32), rows)
    axis_dim = HEAD_DIM // 2
    inv_freq = ROPE_THETA ** (-jnp.arange(0, axis_dim, 2, dtype=F32) / axis_dim)
    ang = jnp.concatenate([row[:, None] * inv_freq, col[:, None] * inv_freq], axis=-1)
    cos = jnp.tile(jnp.repeat(jnp.cos(ang), 2, axis=-1), (1, LANES // HEAD_DIM))
    sin = jnp.tile(jnp.repeat(jnp.sin(ang), 2, axis=-1), (1, LANES // HEAD_DIM))
    sign = jnp.tile(jnp.array([-1.0, 1.0], F32), LANES // 2)[None, :]
    return cos * g2, sin * sign * gsw


def _split6(m):
    return jnp.split(m, 6, axis=-1)


def kernel(x, c, ctx, c_ctx, ada_w, ada_b, norm1_g, norm2_g, final_g, ev_w_in, ev_q_gain, ev_k_gain,
           ev_conv_w, ev_conv_b, ev_ln_g, ev_ln_b, ev_w_out, od_w_in, od_short_w, od_short_b,
           od_filt_w1, od_filt_b1, od_filt_w2, od_filt_b2, od_filt_w3, od_filt_freq, od_hyena_skip,
           od_lower_bound, od_gnorm_g, od_w_out, ffn_w_up, ffn_conv_w, ffn_conv_b, ffn_w_down):
    B, L, D = x.shape
    Lc = ctx.shape[1]
    depth = ada_w.shape[0]
    assert depth == 2, "layer 0 (attention + conformer) followed by layer 1 (hyena + hgrn2)"

    lb_soft = jax.nn.softmax(od_lower_bound.astype(F32), axis=0)
    lower_bounds = jnp.cumsum(lb_soft, axis=0) - lb_soft[0:1]

    cond = jnp.concatenate([c, c_ctx[None, :], jnp.zeros((7, D), F32)], axis=0)

    def mods(l):
        m = ada_modulation(cond, ada_w[l], ada_b[l])
        ml = [t[:, None, :] for t in _split6(m[:B])]
        mc = [jnp.broadcast_to(t[None], (B, 1, D)) for t in _split6(m[B:B + 1])]
        return ml, mc

    def ffn(xx, mm, l, final=None):
        return conv_ffn_residual(xx, norm2_g[l] * (1.0 + mm[4]), mm[3], ffn_w_up[l].astype(BF16),
                                 ffn_conv_w[l], ffn_conv_b[l], ffn_w_down[l].astype(BF16), mm[5], final)

    ml, mc = mods(0)
    w_in = ev_w_in[0].astype(BF16)
    seg = jnp.kron(jnp.eye(256 // HEAD_DIM, dtype=F32), jnp.ones((HEAD_DIM, HEAD_DIM), F32)).astype(BF16)
    scale = HEAD_DIM ** -0.5
    cq_l, sq_l = _rope_tables(L, ev_q_gain[0], scale, True)
    ck_l, sk_l = _rope_tables(L, ev_k_gain[0], 1.0, True)
    cq_c, sq_c = _rope_tables(Lc, ev_q_gain[0], scale, False)
    ck_c, sk_c = _rope_tables(Lc, ev_k_gain[0], 1.0, False)
    q_l, k_l, v_l, glu_l = even_in_proj(x, norm1_g[0] * (1.0 + ml[1]), ml[0], w_in, seg,
                                        cq_l, sq_l, ck_l, sk_l)
    q_c, k_c, v_c, glu_c = even_in_proj(ctx, norm1_g[0] * (1.0 + mc[1]), mc[0], w_in, seg,
                                        cq_c, sq_c, ck_c, sk_c)
    attn_l = attention(q_l, jnp.concatenate([k_c, k_l], axis=1), jnp.concatenate([v_c, v_l], axis=1))
    attn_c = attention(q_c, k_c, v_c)
    conv_l = dwconv(glu_l, ev_conv_w[0], ev_conv_b[0], ev_ln_g[0], ev_ln_b[0], out_dtype=BF16)
    conv_c = dwconv(glu_c, ev_conv_w[0], ev_conv_b[0], ev_ln_g[0], ev_ln_b[0], out_dtype=BF16)
    wo1 = ev_w_out[0][:Q_W].astype(BF16)
    wo2 = ev_w_out[0][Q_W:].astype(BF16)
    x = out_proj_residual(x, attn_l, conv_l, wo1, wo2, ml[2])
    ctx = out_proj_residual(ctx, attn_c, conv_c, wo1, wo2, mc[2])
    x = ffn(x, ml, 0)
    ctx = ffn(ctx, mc, 0)

    ml, mc = mods(1)
    w_in = od_w_in[0].astype(BF16)
    p_lat = norm_mod_matmul(x, norm1_g[1] * (1.0 + ml[1]), ml[0], w_in)
    p_ctx = norm_mod_matmul(ctx, norm1_g[1] * (1.0 + mc[1]), mc[0], w_in[:, HY_IN:])
    d_lat = hgrn2_mixer(p_lat, HY_IN, p_ctx, 0, lower_bounds[1], od_gnorm_g[0])

    u = dwconv(p_lat, od_short_w[0], od_short_b[0])
    hs, hd, nyq = hyena_filter_taps(L, od_filt_w1[0], od_filt_b1[0], od_filt_w2[0], od_filt_b2[0],
                                    od_filt_w3[0], od_filt_freq[0])
    F, G = _dft_tables(L)
    A, Bt = hyena_spectrum(F, hs, hd, nyq)
    z1 = hyena_conv(u, 0, u, 1, F, G, A, Bt, 0, od_hyena_skip[0][0], F32)
    c_lat = hyena_conv(z1, 0, u, 2, F, G, A, Bt, 1, od_hyena_skip[0][1], BF16)

    wo = od_w_out[0].astype(BF16)
    x = out_proj_residual(x, c_lat, d_lat, wo[:HY_DIM], wo[HY_DIM:], ml[2])
    return ffn(x, ml, 1, final_g)
```

```python
import functools
import math

import jax
import jax.numpy as jnp
from jax import lax
from jax.experimental import pallas as pl
from jax.experimental.pallas import tpu as pltpu

F32 = jnp.float32
BF16 = jnp.bfloat16

NORM_EPS = 1e-6
GRID_W = 64
ROPE_THETA = 10000.0
HEAD_DIM = 64
N_Q_HEADS = 12
N_KV_HEADS = 4
Q_W = N_Q_HEADS * HEAD_DIM
KV_W = N_KV_HEADS * HEAD_DIM
CONF_DIM = 512
HY_DIM = 512
HY_IN = 3 * HY_DIM
HG_HEADS = 4
HG_DIM = 128
HG_W = HG_HEADS * HG_DIM
HG_CHUNK = 32
C_FAST_DECAY = 0.3
C_SLOW_DECAY = 1.5
C_DECAY_TARGET = 1e-2
V7X_VMEM_LIMIT = 56 * 1024 * 1024
LANES = 128


def _cparams(sem):
    return pltpu.CompilerParams(dimension_semantics=sem, vmem_limit_bytes=V7X_VMEM_LIMIT)


def _tile(n, prefs):
    for p in prefs:
        if n % p == 0:
            return p
    return n


def _norm_mod(x, gs, sh):
    r = lax.rsqrt(jnp.mean(x * x, axis=-1, keepdims=True) + NORM_EPS)
    return x * r * gs + sh


def _ada_kernel(c_ref, w_ref, b_ref, o_ref):
    c = c_ref[...]
    s = (c * jax.nn.sigmoid(c)).astype(BF16)
    o_ref[...] = jnp.dot(s, w_ref[...].astype(BF16), preferred_element_type=F32) + b_ref[...]


def ada_modulation(cs, w, b):
    R, D = cs.shape
    N = w.shape[1]
    tn = _tile(N, (1024, 512, 256, 128))
    return pl.pallas_call(
        _ada_kernel,
        grid=(N // tn,),
        in_specs=[pl.BlockSpec((R, D), lambda j: (0, 0)),
                  pl.BlockSpec((D, tn), lambda j: (0, j)),
                  pl.BlockSpec((1, tn), lambda j: (0, j))],
        out_specs=pl.BlockSpec((R, tn), lambda j: (0, j)),
        out_shape=jax.ShapeDtypeStruct((R, N), F32),
        compiler_params=_cparams(("arbitrary",)),
        name="ada_modulation",
    )(cs, w, b.reshape(1, N))


def _nm_mm_kernel(x_ref, gs_ref, sh_ref, w_ref, o_ref, *, cn):
    h = _norm_mod(x_ref[0], gs_ref[0], sh_ref[0]).astype(BF16)
    for c0 in range(0, w_ref.shape[1], cn):
        o_ref[0, :, c0:c0 + cn] = jnp.dot(h, w_ref[:, c0:c0 + cn],
                                          preferred_element_type=F32).astype(o_ref.dtype)


def norm_mod_matmul(x, gs, sh, w, out_dtype=F32):
    B, L, D = x.shape
    N = w.shape[1]
    tm = _tile(L, (512, 256, 128))
    cn = _tile(N, (512, 256, 128))
    return pl.pallas_call(
        functools.partial(_nm_mm_kernel, cn=cn),
        grid=(B, L // tm),
        in_specs=[pl.BlockSpec((1, tm, D), lambda b, i: (b, i, 0)),
                  pl.BlockSpec((1, 1, D), lambda b, i: (b, 0, 0)),
                  pl.BlockSpec((1, 1, D), lambda b, i: (b, 0, 0)),
                  pl.BlockSpec((D, N), lambda b, i: (0, 0), pipeline_mode=pl.Buffered(1))],
        out_specs=pl.BlockSpec((1, tm, N), lambda b, i: (b, i, 0)),
        out_shape=jax.ShapeDtypeStruct((B, L, N), out_dtype),
        compiler_params=_cparams(("parallel", "parallel")),
        name="norm_mod_matmul",
    )(x, gs, sh, w)


def _even_in_kernel(x_ref, gs_ref, sh_ref, w_ref, seg_ref, cq_ref, sq_ref, ck_ref, sk_ref,
                    q_ref, klo_ref, khi_ref, vlo_ref, vhi_ref, glu_ref):
    tm = x_ref.shape[1]
    h = _norm_mod(x_ref[0], gs_ref[0], sh_ref[0]).astype(BF16)
    seg = seg_ref[...]
    lane = lax.broadcasted_iota(jnp.int32, (tm, LANES), 1)
    even = (lane & 1) == 0
    low = lane < HEAD_DIM

    def proj(col0, width):
        return jnp.dot(h, w_ref[:, col0:col0 + width], preferred_element_type=F32)

    def head_norm_rope(y, c_ref, s_ref):
        sq = y * y
        hi = sq.astype(BF16)
        lo = (sq - hi.astype(F32)).astype(BF16)
        ss = (jnp.dot(hi, seg, preferred_element_type=F32)
              + jnp.dot(lo, seg, preferred_element_type=F32))
        yn = y * lax.rsqrt(ss * (1.0 / HEAD_DIM) + NORM_EPS)
        out = []
        for j in range(2):
            c = yn[:, LANES * j:LANES * (j + 1)]
            sw = jnp.where(even, pltpu.roll(c, LANES - 1, 1), pltpu.roll(c, 1, 1))
            out.append(c * c_ref[...] + sw * s_ref[...])
        return out

    def put_padded(c, j, lo_ref, hi_ref):
        r = pltpu.roll(c, HEAD_DIM, 1)
        dt = lo_ref.dtype
        lo_ref[0, :, LANES * 2 * j:LANES * (2 * j + 1)] = jnp.where(low, c, 0.0).astype(dt)
        hi_ref[0, :, LANES * 2 * j:LANES * (2 * j + 1)] = jnp.where(low, 0.0, r).astype(dt)
        lo_ref[0, :, LANES * (2 * j + 1):LANES * (2 * j + 2)] = jnp.where(low, r, 0.0).astype(dt)
        hi_ref[0, :, LANES * (2 * j + 1):LANES * (2 * j + 2)] = jnp.where(low, 0.0, c).astype(dt)

    n_q = Q_W // 256
    n_k = KV_W // 256
    g0 = Q_W + 2 * KV_W
    y = proj(0, 256)
    for t in range(n_q + n_k):
        y_next = proj(256 * (t + 1), 256) if t + 1 < n_q + n_k else None
        if t < n_q:
            for j, c in enumerate(head_norm_rope(y, cq_ref, sq_ref)):
                q_ref[0, :, 256 * t + LANES * j:256 * t + LANES * (j + 1)] = c.astype(q_ref.dtype)
        else:
            for j, c in enumerate(head_norm_rope(y, ck_ref, sk_ref)):
                put_padded(c, 2 * (t - n_q) + j, klo_ref, khi_ref)
        y = y_next
    v = proj(Q_W + KV_W, KV_W)
    for j in range(KV_W // LANES):
        put_padded(v[:, LANES * j:LANES * (j + 1)], j, vlo_ref, vhi_ref)
    a = proj(g0, CONF_DIM)
    gate = proj(g0 + CONF_DIM, CONF_DIM)
    glu_ref[0] = a * jax.nn.sigmoid(gate)


def even_in_proj(x, gs, sh, w, seg, cq, sq, ck, sk):
    B, L, D = x.shape
    N = w.shape[1]
    tm = _tile(L, (256, 128))
    row = lambda b, i: (b, i, 0)
    tab = pl.BlockSpec((tm, LANES), lambda b, i: (i, 0))
    KP = N_KV_HEADS * LANES
    kv_spec = pl.BlockSpec((1, tm, KP), row)
    kv_shape = jax.ShapeDtypeStruct((B, L, KP), BF16)
    return pl.pallas_call(
        _even_in_kernel,
        grid=(B, L // tm),
        in_specs=[pl.BlockSpec((1, tm, D), row),
                  pl.BlockSpec((1, 1, D), lambda b, i: (b, 0, 0)),
                  pl.BlockSpec((1, 1, D), lambda b, i: (b, 0, 0)),
                  pl.BlockSpec((D, N), lambda b, i: (0, 0), pipeline_mode=pl.Buffered(1)),
                  pl.BlockSpec((256, 256), lambda b, i: (0, 0)),
                  tab, tab, tab, tab],
        out_specs=[pl.BlockSpec((1, tm, Q_W), row), kv_spec, kv_spec, kv_spec, kv_spec,
                   pl.BlockSpec((1, tm, CONF_DIM), row)],
        out_shape=[jax.ShapeDtypeStruct((B, L, Q_W), BF16), kv_shape, kv_shape, kv_shape, kv_shape,
                   jax.ShapeDtypeStruct((B, L, CONF_DIM), F32)],
        compiler_params=_cparams(("parallel", "parallel")),
        name="even_in_proj",
    )(x, gs, sh, w, seg, cq, sq, ck, sk)


def _attn_kernel(q_ref, *refs, n_seg):
    o_ref = refs[4 * n_seg]
    group = N_Q_HEADS // N_KV_HEADS
    nt = (((1,), (1,)), ((), ()))

    def scores(hq):
        pr, pos = divmod(hq, 2)
        q2 = q_ref[0, :, LANES * pr:LANES * (pr + 1)]
        g = hq // group
        cols = slice(LANES * g, LANES * (g + 1))
        return [lax.dot_general(q2, refs[4 * t + pos][0, :, cols], nt, preferred_element_type=F32)
                for t in range(n_seg)]

    ss = scores(0)
    acc = None
    for hq in range(N_Q_HEADS):
        nxt = scores(hq + 1) if hq + 1 < N_Q_HEADS else None
        pr, pos = divmod(hq, 2)
        g = hq // group
        cols = slice(LANES * g, LANES * (g + 1))
        m = functools.reduce(jnp.maximum, [jnp.max(s, axis=-1, keepdims=True) for s in ss])
        ps = [jnp.exp(s - m) for s in ss]
        l = functools.reduce(jnp.add, [jnp.sum(p, axis=-1, keepdims=True) for p in ps])
        o = functools.reduce(jnp.add, [
            jnp.dot(p.astype(BF16), refs[4 * t + 2 + pos][0, :, cols], preferred_element_type=F32)
            for t, p in enumerate(ps)])
        o = o / l
        if pos == 0:
            acc = o
        else:
            o_ref[0, :, LANES * pr:LANES * (pr + 1)] = (acc + o).astype(o_ref.dtype)
        ss = nxt


def attention(q, segments):
    B, Lq, _ = q.shape
    tq = _tile(Lq, (256, 128))
    KP = N_KV_HEADS * LANES
    specs, args = [], []
    for seg in segments:
        for t in seg:
            specs.append(pl.BlockSpec((1, t.shape[1], KP), lambda b, i: (b, 0, 0)))
            args.append(t)
    return pl.pallas_call(
        functools.partial(_attn_kernel, n_seg=len(segments)),
        grid=(B, Lq // tq),
        in_specs=[pl.BlockSpec((1, tq, Q_W), lambda b, i: (b, i, 0))] + specs,
        out_specs=pl.BlockSpec((1, tq, Q_W), lambda b, i: (b, i, 0)),
        out_shape=jax.ShapeDtypeStruct((B, Lq, Q_W), BF16),
        compiler_params=_cparams(("parallel", "parallel")),
        name="attention",
    )(q, *args)


_CONV_PAD = 16


def _dwconv_kernel(x_ref, w_ref, b_ref, g_ref, beta_ref, o_ref, xp_ref, *, K, TR, ln_swish):
    L = x_ref.shape[1]
    C = x_ref.shape[2]
    half = (K - 1) // 2
    xp_ref[0:_CONV_PAD, :] = jnp.zeros((_CONV_PAD, C), F32)
    xp_ref[_CONV_PAD + L:, :] = jnp.zeros((_CONV_PAD, C), F32)
    xp_ref[_CONV_PAD:_CONV_PAD + L, :] = x_ref[0]

    W = TR + 2 * _CONV_PAD

    def tile(t, carry):
        r0 = pl.multiple_of(t * TR, TR)
        win = xp_ref[pl.ds(r0, W), :]
        acc = jnp.zeros((TR, C), F32) + b_ref[...]
        rolled = {0: win}
        for k in range(K):
            q8, r8 = divmod(_CONV_PAD - half + k, 8)
            if r8 not in rolled:
                rolled[r8] = pltpu.roll(win, W - r8, 0)
            acc = acc + rolled[r8][8 * q8:8 * q8 + TR, :] * w_ref[k:k + 1, :]
        if ln_swish:
            mu = jnp.mean(acc, axis=-1, keepdims=True)
            d = acc - mu
            var = jnp.mean(d * d, axis=-1, keepdims=True)
            y = d * lax.rsqrt(var + NORM_EPS) * g_ref[...] + beta_ref[...]
            acc = y * jax.nn.sigmoid(y)
        o_ref[0, pl.ds(r0, TR), :] = acc.astype(o_ref.dtype)
        return carry

    lax.fori_loop(0, L // TR, tile, 0)


def dwconv(x, w, b, g=None, beta=None, *, out_dtype=F32):
    B, L, _ = x.shape
    K, C = w.shape
    ln_swish = g is not None
    tc = C if ln_swish else _tile(C, (512, 256, 128))
    TR = _tile(L, (32, 16, 8))
    if g is None:
        g = jnp.ones((C,), F32)
        beta = jnp.zeros((C,), F32)
    vec = pl.BlockSpec((1, tc), lambda bb, j: (0, j))
    return pl.pallas_call(
        functools.partial(_dwconv_kernel, K=K, TR=TR, ln_swish=ln_swish),
        grid=(B, C // tc),
        in_specs=[pl.BlockSpec((1, L, tc), lambda bb, j: (bb, 0, j)),
                  pl.BlockSpec((K, tc), lambda bb, j: (0, j)),
                  vec, vec, vec],
        out_specs=pl.BlockSpec((1, L, tc), lambda bb, j: (bb, 0, j)),
        out_shape=jax.ShapeDtypeStruct((B, L, C), out_dtype),
        scratch_shapes=[pltpu.VMEM((L + 2 * _CONV_PAD, tc), F32)],
        compiler_params=_cparams(("parallel", "parallel")),
        name="dwconv%d" % K,
    )(x, w, b.reshape(1, C), g.reshape(1, C), beta.reshape(1, C))


def _out_proj_kernel(x_ref, a1_ref, a2_ref, w1_ref, w2_ref, g_ref, o_ref):
    y = jnp.dot(a1_ref[0], w1_ref[...], preferred_element_type=F32)
    y = y + jnp.dot(a2_ref[0], w2_ref[...], preferred_element_type=F32)
    o_ref[0] = x_ref[0] + g_ref[0] * y


def out_proj_residual(x, a1, a2, w1, w2, gate):
    B, L, D = x.shape
    K1, K2 = a1.shape[2], a2.shape[2]
    tm = _tile(L, (512, 256, 128))
    row = lambda b, i: (b, i, 0)
    return pl.pallas_call(
        _out_proj_kernel,
        grid=(B, L // tm),
        in_specs=[pl.BlockSpec((1, tm, D), row),
                  pl.BlockSpec((1, tm, K1), row),
                  pl.BlockSpec((1, tm, K2), row),
                  pl.BlockSpec((K1, D), lambda b, i: (0, 0)),
                  pl.BlockSpec((K2, D), lambda b, i: (0, 0)),
                  pl.BlockSpec((1, 1, D), lambda b, i: (b, 0, 0))],
        out_specs=pl.BlockSpec((1, tm, D), row),
        out_shape=jax.ShapeDtypeStruct((B, L, D), F32),
        compiler_params=_cparams(("parallel", "parallel")),
        name="out_proj_residual",
    )(x, a1, a2, w1, w2, gate)


_FFN_CHUNK = 256


def _ffn_kernel(x_ref, xp_ref, xn_ref, gs_ref, sh_ref, wup_ref, cw_ref, cb_ref, wdn_ref, g5_ref,
                fg_ref, o_ref, acc_ref, *, final_norm):
    tm = x_ref.shape[1]
    FF = wdn_ref.shape[0]
    i = pl.program_id(1)
    has_prev = (i > 0).astype(F32)
    has_next = (i < pl.num_programs(1) - 1).astype(F32)
    gs = gs_ref[0]
    sh = sh_ref[0]
    x = x_ref[0]
    h = _norm_mod(x, gs, sh).astype(BF16)
    halo = jnp.concatenate([_norm_mod(xp_ref[0], gs, sh), _norm_mod(xn_ref[0], gs, sh)],
                           axis=0).astype(BF16)
    row = lax.broadcasted_iota(jnp.int32, (tm, _FFN_CHUNK), 0)
    first = row == 0
    last = row == tm - 1

    def up(c0):
        w = wup_ref[:, c0:c0 + _FFN_CHUNK]
        return (jnp.dot(h, w, preferred_element_type=F32),
                jnp.dot(halo, w, preferred_element_type=F32))

    def conv3(c0, p, ph):
        prev_row = ph[7:8, :] * has_prev
        next_row = ph[8:9, :] * has_next
        pm1 = jnp.where(first, prev_row, pltpu.roll(p, 1, 0))
        pp1 = jnp.where(last, next_row, pltpu.roll(p, tm - 1, 0))
        return (cw_ref[0:1, c0:c0 + _FFN_CHUNK] * pm1 + cw_ref[1:2, c0:c0 + _FFN_CHUNK] * p
                + cw_ref[2:3, c0:c0 + _FFN_CHUNK] * pp1 + cb_ref[:, c0:c0 + _FFN_CHUNK])

    n_chunks = FF // _FFN_CHUNK
    ups = (up(0), up(FF))
    for c in range(n_chunks):
        nxt = (up((c + 1) * _FFN_CHUNK), up(FF + (c + 1) * _FFN_CHUNK)) if c + 1 < n_chunks else None
        val = conv3(c * _FFN_CHUNK, *ups[0])
        gate = conv3(FF + c * _FFN_CHUNK, *ups[1])
        act = (gate * jax.nn.sigmoid(gate) * val).astype(BF16)
        part = jnp.dot(act, wdn_ref[c * _FFN_CHUNK:(c + 1) * _FFN_CHUNK, :],
                       preferred_element_type=F32)
        if c == 0:
            acc_ref[...] = part
        else:
            acc_ref[...] += part
        ups = nxt
    y = x + g5_ref[0] * acc_ref[...]
    if final_norm:
        y = y * lax.rsqrt(jnp.mean(y * y, axis=-1, keepdims=True) + NORM_EPS) * fg_ref[...]
    o_ref[0] = y


def conv_ffn_residual(x, gs, sh, w_up, cw, cb, w_dn, gate, final_g=None):
    B, L, D = x.shape
    FF = w_dn.shape[0]
    assert FF % _FFN_CHUNK == 0
    tm = _tile(L, (512, 256, 128))
    nb8 = L // 8
    t8 = tm // 8
    final_norm = final_g is not None
    if final_g is None:
        final_g = jnp.ones((D,), F32)
    row = lambda b, i: (b, i, 0)
    const = lambda b, i: (0, 0)
    return pl.pallas_call(
        functools.partial(_ffn_kernel, final_norm=final_norm),
        grid=(B, L // tm),
        in_specs=[pl.BlockSpec((1, tm, D), row),
                  pl.BlockSpec((1, 8, D), lambda b, i: (b, jnp.maximum(i * t8 - 1, 0), 0)),
                  pl.BlockSpec((1, 8, D), lambda b, i: (b, jnp.minimum((i + 1) * t8, nb8 - 1), 0)),
                  pl.BlockSpec((1, 1, D), lambda b, i: (b, 0, 0)),
                  pl.BlockSpec((1, 1, D), lambda b, i: (b, 0, 0)),
                  pl.BlockSpec((D, 2 * FF), const, pipeline_mode=pl.Buffered(1)),
                  pl.BlockSpec((3, 2 * FF), const),
                  pl.BlockSpec((1, 2 * FF), const),
                  pl.BlockSpec((FF, D), const, pipeline_mode=pl.Buffered(1)),
                  pl.BlockSpec((1, 1, D), lambda b, i: (b, 0, 0)),
                  pl.BlockSpec((1, D), const)],
        out_specs=pl.BlockSpec((1, tm, D), row),
        out_shape=jax.ShapeDtypeStruct((B, L, D), F32),
        scratch_shapes=[pltpu.VMEM((tm, D), F32)],
        compiler_params=_cparams(("parallel", "arbitrary")),
        name="conv_ffn",
    )(x, x, x, gs, sh, w_up, cw, cb.reshape(1, 2 * FF), w_dn, gate, final_g.reshape(1, D))


def _hy_filter_kernel(feat_ref, w1_ref, b1_ref, w2_ref, b2_ref, w3_ref, fr_ref, dec_ref,
                      hs_ref, hd_ref, nyq_ref):
    L = feat_ref.shape[0]
    fr = fr_ref[...]
    z = jnp.dot(feat_ref[...].astype(BF16), w1_ref[...].astype(BF16), preferred_element_type=F32)
    hdn = jnp.sin(fr * (z + b1_ref[...]))
    z = jnp.dot(hdn.astype(BF16), w2_ref[...].astype(BF16), preferred_element_type=F32)
    hdn = jnp.sin(fr * (z + b2_ref[...])).astype(BF16)
    row = lax.broadcasted_iota(jnp.int32, (L, HY_DIM), 0)
    not_first = (row > 0).astype(F32)
    sign = jnp.where((row & 1) == 0, 1.0, -1.0).astype(F32)
    dec = dec_ref[...]
    n_order = w3_ref.shape[1] // (2 * HY_DIM)
    for o in range(n_order):
        c0 = o * 2 * HY_DIM
        hf = jnp.dot(hdn, w3_ref[:, c0:c0 + HY_DIM].astype(BF16), preferred_element_type=F32) * dec
        hb = (jnp.dot(hdn, w3_ref[:, c0 + HY_DIM:c0 + 2 * HY_DIM].astype(BF16),
                      preferred_element_type=F32) * dec * not_first)
        inv = 1.0 / (jnp.sum(jnp.abs(hf), axis=0, keepdims=True)
                     + jnp.sum(jnp.abs(hb), axis=0, keepdims=True))
        hs = (hf + hb) * inv
        hd = (hf - hb) * inv
        hs_ref[:, o * HY_DIM:(o + 1) * HY_DIM] = hs
        hd_ref[:, o * HY_DIM:(o + 1) * HY_DIM] = hd
        nyq_ref[:, o * HY_DIM:(o + 1) * HY_DIM] = jnp.sum(hs * sign, axis=0, keepdims=True)


def hyena_filter_taps(L, w1, b1, w2, b2, w3, freq):
    emb = w1.shape[0]
    bands = (emb - 1) // 2
    t = jnp.linspace(0.0, 1.0, L, dtype=F32)[:, None]
    fb = jnp.linspace(1e-4, bands - 1, bands, dtype=F32)
    wpos = (2.0 * math.pi / L) * jnp.arange(L, dtype=F32)[:, None]
    feats = jnp.concatenate([t, jnp.cos(fb * wpos), -jnp.sin(fb * wpos)], axis=-1)
    feats = jnp.pad(feats, ((0, 0), (0, LANES - emb)))
    w1p = jnp.pad(w1, ((0, LANES - emb), (0, 0)))
    deltas = jnp.abs(jnp.linspace(math.log(C_DECAY_TARGET) / C_SLOW_DECAY,
                                  math.log(C_DECAY_TARGET) / C_FAST_DECAY, HY_DIM, dtype=F32))
    dec = jnp.exp(-t * deltas)
    hid = w1.shape[1]
    NO = w3.shape[1] // 2
    full = lambda shape: pl.BlockSpec(shape, lambda: tuple(0 for _ in shape))
    return pl.pallas_call(
        _hy_filter_kernel,
        in_specs=[full((L, LANES)), full((LANES, hid)), full((1, hid)), full((hid, hid)),
                  full((1, hid)), full((hid, 2 * NO)), full((1, hid)), full((L, HY_DIM))],
        out_specs=[full((L, NO)), full((L, NO)), full((1, NO))],
        out_shape=[jax.ShapeDtypeStruct((L, NO), F32), jax.ShapeDtypeStruct((L, NO), F32),
                   jax.ShapeDtypeStruct((1, NO), F32)],
        compiler_params=pltpu.CompilerParams(vmem_limit_bytes=V7X_VMEM_LIMIT),
        name="hyena_filter",
    )(feats, w1p, b1.reshape(1, hid), w2, b2.reshape(1, hid), w3, freq.reshape(1, hid), dec)


_DFT_SPLIT = 64


def _dft_tables(L):
    N = 2 * L
    n = jnp.arange(L, dtype=jnp.int32)

    def cos_sin(mult):
        ang = ((mult[:, None] * n[None, :]) & (N - 1)).astype(F32) * (2.0 * math.pi / N)
        return jnp.cos(ang), jnp.sin(ang)

    ca, sa = cos_sin(jnp.arange(L // _DFT_SPLIT, dtype=jnp.int32) * _DFT_SPLIT)
    cb, sb = cos_sin(jnp.arange(_DFT_SPLIT, dtype=jnp.int32))
    nyq = jnp.where((n & 1) == 0, 1.0, -1.0).astype(F32)
    re = (ca[:, None, :] * cb[None] - sa[:, None, :] * sb[None]).reshape(L, L)
    im = -(sa[:, None, :] * cb[None] + ca[:, None, :] * sb[None]).reshape(L, L)
    im = jnp.where(jnp.arange(L)[:, None] == 0, nyq[None, :], im)
    f = jnp.concatenate([re, im], axis=0).astype(BF16)
    cat, sat, cbt, sbt = ca.T, sa.T, cb.T, sb.T
    re_t = (cat[:, :, None] * cbt[:, None, :] - sat[:, :, None] * sbt[:, None, :]).reshape(L, L)
    im_t = -(sat[:, :, None] * cbt[:, None, :] + cat[:, :, None] * sbt[:, None, :]).reshape(L, L)
    im_t = jnp.where(jnp.arange(L)[None, :] == 0, nyq[:, None], im_t)
    g = jnp.concatenate([re_t, im_t], axis=1).astype(BF16)
    return f, g


def _hy_spec_kernel(fre_ref, fim_ref, hs_ref, hd_ref, nyq_ref, a_ref, b_ref):
    kb = fre_ref.shape[0]
    L = fre_ref.shape[1]
    NO = hs_ref.shape[1]
    krow = lax.broadcasted_iota(jnp.int32, (kb, NO), 0) + pl.program_id(0) * kb
    dc = krow == 0
    scale = jnp.where(dc, 1.0, 2.0).astype(F32) * (1.0 / (2 * L))
    hr = jnp.dot(fre_ref[...], hs_ref[...].astype(BF16), preferred_element_type=F32)
    hi = jnp.dot(fim_ref[...], hd_ref[...].astype(BF16), preferred_element_type=F32)
    a_ref[...] = hr * scale
    b_ref[...] = jnp.where(dc, nyq_ref[...], hi) * scale


def hyena_spectrum(F, hs, hd, nyq):
    L, NO = hs.shape
    kb = _tile(L, (256, 128))
    nk = L // kb
    return pl.pallas_call(
        _hy_spec_kernel,
        grid=(nk,),
        in_specs=[pl.BlockSpec((kb, L), lambda i: (i, 0)),
                  pl.BlockSpec((kb, L), lambda i: (i + nk, 0)),
                  pl.BlockSpec((L, NO), lambda i: (0, 0)),
                  pl.BlockSpec((L, NO), lambda i: (0, 0)),
                  pl.BlockSpec((1, NO), lambda i: (0, 0))],
        out_specs=[pl.BlockSpec((kb, NO), lambda i: (i, 0)),
                   pl.BlockSpec((kb, NO), lambda i: (i, 0))],
        out_shape=[jax.ShapeDtypeStruct((L, NO), F32), jax.ShapeDtypeStruct((L, NO), F32)],
        compiler_params=_cparams(("arbitrary",)),
        name="hyena_spectrum",
    )(F, F, hs, hd, nyq)


def _hy_conv_kernel(z_ref, gate_ref, fre_ref, fim_ref, gre_ref, gim_ref, a_ref, b_ref, skip_ref,
                    o_ref, zb_ref, y_ref):
    kb = fre_ref.shape[0]
    C = z_ref.shape[2]
    kstep = pl.program_id(1)

    @pl.when(kstep == 0)
    def _():
        zb_ref[...] = z_ref[0].astype(BF16)
        y_ref[...] = jnp.zeros(y_ref.shape, F32)

    zb = zb_ref[...]
    NH = 2
    hb = kb // NH
    fwd = []
    for hf in range(NH):
        rows = slice(hf * hb, (hf + 1) * hb)
        fwd.append((jnp.dot(fre_ref[rows, :], zb, preferred_element_type=F32),
                    jnp.dot(fim_ref[rows, :], zb, preferred_element_type=F32)))
    for hf in range(NH):
        rows = slice(hf * hb, (hf + 1) * hb)
        ur, ui = fwd[hf]
        a = a_ref[rows, :]
        b = b_ref[rows, :]
        krow = lax.broadcasted_iota(jnp.int32, (hb, C), 0) + (kstep * kb + hf * hb)
        dc = krow == 0
        vr = jnp.where(dc, ur * a, ur * a - ui * b)
        vi = jnp.where(dc, ui * b, ur * b + ui * a)
        y_ref[...] += (jnp.dot(gre_ref[:, rows], vr.astype(BF16), preferred_element_type=F32)
                       + jnp.dot(gim_ref[:, rows], vi.astype(BF16), preferred_element_type=F32))

    @pl.when(kstep == pl.num_programs(1) - 1)
    def _():
        o_ref[0] = (gate_ref[0] * (y_ref[...] + skip_ref[...] * z_ref[0])).astype(o_ref.dtype)


def hyena_conv(z_src, z_col, gate_src, gate_col, F, G, A, Bt, spec_col, skip, out_dtype):
    B, L, _ = z_src.shape
    C = HY_DIM
    kb = _tile(L, (512, 256))
    nk = L // kb
    return pl.pallas_call(
        _hy_conv_kernel,
        grid=(B, nk),
        in_specs=[pl.BlockSpec((1, L, C), lambda b, k: (b, 0, z_col)),
                  pl.BlockSpec((1, L, C), lambda b, k: (b, 0, gate_col)),
                  pl.BlockSpec((kb, L), lambda b, k: (k, 0)),
                  pl.BlockSpec((kb, L), lambda b, k: (k + nk, 0)),
                  pl.BlockSpec((L, kb), lambda b, k: (0, k)),
                  pl.BlockSpec((L, kb), lambda b, k: (0, k + nk)),
                  pl.BlockSpec((kb, C), lambda b, k: (k, spec_col)),
                  pl.BlockSpec((kb, C), lambda b, k: (k, spec_col)),
                  pl.BlockSpec((1, C), lambda b, k: (0, 0))],
        out_specs=pl.BlockSpec((1, L, C), lambda b, k: (b, 0, 0)),
        out_shape=jax.ShapeDtypeStruct((B, L, C), out_dtype),
        scratch_shapes=[pltpu.VMEM((L, C), BF16), pltpu.VMEM((L, C), F32)],
        compiler_params=_cparams(("parallel", "arbitrary")),
        name="hyena_conv",
    )(z_src, gate_src, F, F, G, G, A, Bt, skip.reshape(1, C))


HG_GROUP = 128


def _hgrn2_kernel(ql_ref, ffl_ref, fbl_ref, il_ref, gl_ref, qc_ref, ffc_ref, fbc_ref, ic_ref,
                  lb_ref, gn_ref, o_ref, vb_ref, qdf_ref, qdb_ref, kuf_ref, kub_ref,
                  kdf_ref, kdb_ref, decf_ref, decb_ref, oif_ref, oib_ref, stf_ref, stb_ref):
    C = HG_CHUNK
    GR = HG_GROUP
    L = ql_ref.shape[1]
    Lc = qc_ref.shape[1]
    r = lax.broadcasted_iota(jnp.int32, (GR, GR), 0)
    c = lax.broadcasted_iota(jnp.int32, (GR, GR), 1)
    same = (r // C) == (c // C)
    dirs = []
    for d, (incl, excl) in enumerate(((c <= r, c > r), (c >= r, c < r))):
        mask = same & incl
        dirs.append((lb_ref[d:d + 1, :], mask.astype(BF16), (same & excl).astype(BF16), mask))
    f_refs_lat = (ffl_ref, fbl_ref)
    f_refs_ctx = (ffc_ref, fbc_ref)
    qd_refs = (qdf_ref, qdb_ref)
    ku_refs = (kuf_ref, kub_ref)
    kd_refs = (kdf_ref, kdb_ref)
    dec_refs = (decf_ref, decb_ref)
    oi_refs = (oif_ref, oib_ref)
    st_refs = (stf_ref, stb_ref)
    nt = (((1,), (1,)), ((), ()))

    def gates(q_ref, f_refs, i_ref, n_rows, row_off, with_out):
        def body(gi, carry):
            r0 = pl.multiple_of(gi * GR, GR)
            ro = pl.multiple_of(row_off + gi * GR, GR)
            q = q_ref[0, pl.ds(r0, GR), :]
            vb_ref[pl.ds(ro, GR), :] = i_ref[0, pl.ds(r0, GR), :].astype(BF16)
            for d, (lbd, t_in, t_ex, _) in enumerate(dirs):
                f = lbd + (1.0 - lbd) * jax.nn.sigmoid(f_refs[d][0, pl.ds(r0, GR), :])
                k = 1.0 - f
                lg = jnp.log(f)
                hi = lg.astype(BF16)
                lo = (lg - hi.astype(F32)).astype(BF16)
                b = (jnp.dot(t_in, hi, preferred_element_type=F32)
                     + jnp.dot(t_in, lo, preferred_element_type=F32))
                rest = (jnp.dot(t_ex, hi, preferred_element_type=F32)
                        + jnp.dot(t_ex, lo, preferred_element_type=F32))
                qd_refs[d][pl.ds(ro, GR), :] = (q * jnp.exp(b)).astype(BF16)
                ku_refs[d][pl.ds(ro, GR), :] = (k * jnp.exp(rest)).astype(BF16)
                dec_refs[d][pl.ds(ro, GR), :] = b + rest
                if with_out:
                    kd_refs[d][pl.ds(r0, GR), :] = (k * jnp.exp(-b)).astype(BF16)
            return carry

        lax.fori_loop(0, n_rows // GR, body, 0, unroll=2)

    def intra():
        U = 4 if (L // GR) % 4 == 0 else 1

        def body(gi, carry):
            jobs = []
            for u in range(U):
                r0 = pl.multiple_of((gi * U + u) * GR, GR)
                ro = pl.multiple_of(Lc + (gi * U + u) * GR, GR)
                for d in range(2):
                    sc = lax.dot_general(qd_refs[d][pl.ds(ro, GR), :], kd_refs[d][pl.ds(r0, GR), :], nt,
                                         preferred_element_type=F32)
                    jobs.append((d, r0, ro, sc))
            for d, r0, ro, sc in jobs:
                sc = jnp.where(dirs[d][3], sc, 0.0).astype(BF16)
                oi_refs[d][pl.ds(r0, GR), :] = jnp.dot(sc, vb_ref[pl.ds(ro, GR), :],
                                                       preferred_element_type=F32)
            return carry

        lax.fori_loop(0, L // (GR * U), body, 0)

    def recurrence(n_rows, row_off, keep, carry):
        n = n_rows // C

        def body(j, states):
            new = []
            for d, st in enumerate(states):
                jj = j if d == 0 else n - 1 - j
                ro = pl.multiple_of(row_off + jj * C, C)
                if keep:
                    st_refs[d][pl.ds(pl.multiple_of(jj * HG_DIM, HG_DIM), HG_DIM), :] = st.astype(BF16)
                upd = lax.dot_general(vb_ref[pl.ds(ro, C), :], ku_refs[d][pl.ds(ro, C), :],
                                      (((0,), (0,)), ((), ())), preferred_element_type=F32)
                new.append(st * jnp.exp(dec_refs[d][pl.ds(ro, 1), :]) + upd)
            return tuple(new)

        return lax.fori_loop(0, n, body, carry, unroll=8)

    def inter():
        def body(j, carry):
            r0 = pl.multiple_of(j * C, C)
            ro = pl.multiple_of(Lc + j * C, C)
            so = pl.multiple_of(j * HG_DIM, HG_DIM)
            for d in range(2):
                o = lax.dot_general(qd_refs[d][pl.ds(ro, C), :], st_refs[d][pl.ds(so, HG_DIM), :], nt,
                                    preferred_element_type=F32)
                oi_refs[d][pl.ds(r0, C), :] += o
            return carry

        lax.fori_loop(0, L // C, body, 0, unroll=8)

    gates(qc_ref, f_refs_ctx, ic_ref, Lc, 0, False)
    gates(ql_ref, f_refs_lat, il_ref, L, Lc, True)
    intra()
    zero = jnp.zeros((HG_DIM, HG_DIM), F32)
    carry = recurrence(Lc, 0, False, (zero, zero))
    recurrence(L, Lc, True, carry)
    inter()
    o = oif_ref[...] + oib_ref[...]
    o = o * lax.rsqrt(jnp.mean(o * o, axis=-1, keepdims=True) + NORM_EPS) * gn_ref[...]
    g = gl_ref[0]
    o_ref[0] = (o * (g * jax.nn.sigmoid(g))).astype(o_ref.dtype)


def hgrn2_mixer(p_lat, lat_col0, p_ctx, ctx_col0, lb, gn_g):
    B, L, _ = p_lat.shape
    Lc = p_ctx.shape[1]
    lc0 = lat_col0 // HG_DIM
    cc0 = ctx_col0 // HG_DIM

    def lat(seg):
        return pl.BlockSpec((1, L, HG_DIM), lambda b, h: (b, 0, lc0 + HG_HEADS * seg + h))

    def ctx(seg):
        return pl.BlockSpec((1, Lc, HG_DIM), lambda b, h: (b, 0, cc0 + HG_HEADS * seg + h))

    return pl.pallas_call(
        _hgrn2_kernel,
        grid=(B, HG_HEADS),
        in_specs=[lat(0), lat(1), lat(2), lat(3), lat(4), ctx(0), ctx(1), ctx(2), ctx(3),
                  pl.BlockSpec((2, HG_DIM), lambda b, h: (0, h)),
                  pl.BlockSpec((1, HG_DIM), lambda b, h: (0, 0))],
        out_specs=pl.BlockSpec((1, L, HG_DIM), lambda b, h: (b, 0, h)),
        out_shape=jax.ShapeDtypeStruct((B, L, HG_W), BF16),
        scratch_shapes=([pltpu.VMEM((Lc + L, HG_DIM), BF16)] * 5
                        + [pltpu.VMEM((L, HG_DIM), BF16)] * 2
                        + [pltpu.VMEM((Lc + L, HG_DIM), F32)] * 2
                        + [pltpu.VMEM((L, HG_DIM), F32)] * 2
                        + [pltpu.VMEM((L // HG_CHUNK * HG_DIM, HG_DIM), BF16)] * 2),
        compiler_params=_cparams(("parallel", "parallel")),
        name="hgrn2",
    )(p_lat, p_lat, p_lat, p_lat, p_lat, p_ctx, p_ctx, p_ctx, p_ctx, lb, gn_g.reshape(1, HG_DIM))


def _rope_tables(L, gain, scale, rotate):
    g2 = jnp.tile(gain.astype(F32), LANES // HEAD_DIM)[None, :] * scale
    gsw = g2.reshape(1, LANES // 2, 2)[:, :, ::-1].reshape(1, LANES)
    if not rotate:
        return jnp.broadcast_to(g2, (L, LANES)), jnp.zeros((L, LANES), F32)
    rows = L // GRID_W
    row = jnp.repeat(jnp.arange(rows, dtype=F32), GRID_W)
    col = jnp.tile(jnp.arange(GRID_W, dtype=F32), rows)
    axis_dim = HEAD_DIM // 2
    inv_freq = ROPE_THETA ** (-jnp.arange(0, axis_dim, 2, dtype=F32) / axis_dim)
    ang = jnp.concatenate([row[:, None] * inv_freq, col[:, None] * inv_freq], axis=-1)
    cos = jnp.tile(jnp.repeat(jnp.cos(ang), 2, axis=-1), (1, LANES // HEAD_DIM))
    sin = jnp.tile(jnp.repeat(jnp.sin(ang), 2, axis=-1), (1, LANES // HEAD_DIM))
    sign = jnp.tile(jnp.array([-1.0, 1.0], F32), LANES // 2)[None, :]
    return cos * g2, sin * sign * gsw


def _split6(m):
    return jnp.split(m, 6, axis=-1)


def kernel(x, c, ctx, c_ctx, ada_w, ada_b, norm1_g, norm2_g, final_g, ev_w_in, ev_q_gain, ev_k_gain,
           ev_conv_w, ev_conv_b, ev_ln_g, ev_ln_b, ev_w_out, od_w_in, od_short_w, od_short_b,
           od_filt_w1, od_filt_b1, od_filt_w2, od_filt_b2, od_filt_w3, od_filt_freq, od_hyena_skip,
           od_lower_bound, od_gnorm_g, od_w_out, ffn_w_up, ffn_conv_w, ffn_conv_b, ffn_w_down):
    B, L, D = x.shape
    Lc = ctx.shape[1]
    depth = ada_w.shape[0]
    assert depth == 2, "layer 0 (attention + conformer) followed by layer 1 (hyena + hgrn2)"

    lb_soft = jax.nn.softmax(od_lower_bound.astype(F32), axis=0)
    lower_bounds = jnp.cumsum(lb_soft, axis=0) - lb_soft[0:1]

    cond = jnp.concatenate([c, c_ctx[None, :], jnp.zeros((7, D), F32)], axis=0)

    def mods(l):
        m = ada_modulation(cond, ada_w[l], ada_b[l])
        ml = [t[:, None, :] for t in _split6(m[:B])]
        mc = [jnp.broadcast_to(t[None], (B, 1, D)) for t in _split6(m[B:B + 1])]
        return ml, mc

    def ffn(xx, mm, l, final=None):
        return conv_ffn_residual(xx, norm2_g[l] * (1.0 + mm[4]), mm[3], ffn_w_up[l].astype(BF16),
                                 ffn_conv_w[l], ffn_conv_b[l], ffn_w_down[l].astype(BF16), mm[5], final)

    ml, mc = mods(0)
    w_in = ev_w_in[0].astype(BF16)
    seg = jnp.kron(jnp.eye(256 // HEAD_DIM, dtype=F32), jnp.ones((HEAD_DIM, HEAD_DIM), F32)).astype(BF16)
    scale = HEAD_DIM ** -0.5
    cq_l, sq_l = _rope_tables(L, ev_q_gain[0], scale, True)
    ck_l, sk_l = _rope_tables(L, ev_k_gain[0], 1.0, True)
    cq_c, sq_c = _rope_tables(Lc, ev_q_gain[0], scale, False)
    ck_c, sk_c = _rope_tables(Lc, ev_k_gain[0], 1.0, False)
    q_l, *kv_l, glu_l = even_in_proj(x, norm1_g[0] * (1.0 + ml[1]), ml[0], w_in, seg,
                                     cq_l, sq_l, ck_l, sk_l)
    q_c, *kv_c, glu_c = even_in_proj(ctx, norm1_g[0] * (1.0 + mc[1]), mc[0], w_in, seg,
                                     cq_c, sq_c, ck_c, sk_c)
    attn_l = attention(q_l, [kv_c, kv_l])
    attn_c = attention(q_c, [kv_c])
    conv_l = dwconv(glu_l, ev_conv_w[0], ev_conv_b[0], ev_ln_g[0], ev_ln_b[0], out_dtype=BF16)
    conv_c = dwconv(glu_c, ev_conv_w[0], ev_conv_b[0], ev_ln_g[0], ev_ln_b[0], out_dtype=BF16)
    wo1 = ev_w_out[0][:Q_W].astype(BF16)
    wo2 = ev_w_out[0][Q_W:].astype(BF16)
    x = out_proj_residual(x, attn_l, conv_l, wo1, wo2, ml[2])
    ctx = out_proj_residual(ctx, attn_c, conv_c, wo1, wo2, mc[2])
    x = ffn(x, ml, 0)
    ctx = ffn(ctx, mc, 0)

    ml, mc = mods(1)
    w_in = od_w_in[0].astype(BF16)
    p_lat = norm_mod_matmul(x, norm1_g[1] * (1.0 + ml[1]), ml[0], w_in)
    p_ctx = norm_mod_matmul(ctx, norm1_g[1] * (1.0 + mc[1]), mc[0], w_in[:, HY_IN:])
    d_lat = hgrn2_mixer(p_lat, HY_IN, p_ctx, 0, lower_bounds[1], od_gnorm_g[0])

    u = dwconv(p_lat, od_short_w[0], od_short_b[0])
    hs, hd, nyq = hyena_filter_taps(L, od_filt_w1[0], od_filt_b1[0], od_filt_w2[0], od_filt_b2[0],
                                    od_filt_w3[0], od_filt_freq[0])
    F, G = _dft_tables(L)
    A, Bt = hyena_spectrum(F, hs, hd, nyq)
    z1 = hyena_conv(u, 0, u, 1, F, G, A, Bt, 0, od_hyena_skip[0][0], F32)
    c_lat = hyena_conv(z1, 0, u, 2, F, G, A, Bt, 1, od_hyena_skip[0][1], BF16)

    wo = od_w_out[0].astype(BF16)
    x = out_proj_residual(x, c_lat, d_lat, wo[:HY_DIM], wo[HY_DIM:], ml[2])
    return ffn(x, ml, 1, final_g)
```

```python
import functools
import math

import jax
import jax.numpy as jnp
from jax import lax
from jax.experimental import pallas as pl
from jax.experimental.pallas import tpu as pltpu

F32 = jnp.float32
BF16 = jnp.bfloat16

NORM_EPS = 1e-6
GRID_W = 64
ROPE_THETA = 10000.0
HEAD_DIM = 64
N_Q_HEADS = 12
N_KV_HEADS = 4
Q_W = N_Q_HEADS * HEAD_DIM
KV_W = N_KV_HEADS * HEAD_DIM
CONF_DIM = 512
HY_DIM = 512
HY_IN = 3 * HY_DIM
HG_HEADS = 4
HG_DIM = 128
HG_W = HG_HEADS * HG_DIM
HG_CHUNK = 32
C_FAST_DECAY = 0.3
C_SLOW_DECAY = 1.5
C_DECAY_TARGET = 1e-2
V7X_VMEM_LIMIT = 56 * 1024 * 1024
LANES = 128


def _cparams(sem):
    return pltpu.CompilerParams(dimension_semantics=sem, vmem_limit_bytes=V7X_VMEM_LIMIT)


def _tile(n, prefs):
    for p in prefs:
        if n % p == 0:
            return p
    return n


def _norm_mod(x, gs, sh):
    r = lax.rsqrt(jnp.mean(x * x, axis=-1, keepdims=True) + NORM_EPS)
    return x * r * gs + sh


def _ada_kernel(c_ref, w_ref, b_ref, o_ref):
    c = c_ref[...]
    s = (c * jax.nn.sigmoid(c)).astype(BF16)
    o_ref[...] = jnp.dot(s, w_ref[...].astype(BF16), preferred_element_type=F32) + b_ref[...]


def ada_modulation(cs, w, b):
    R, D = cs.shape
    N = w.shape[1]
    tn = _tile(N, (1024, 512, 256, 128))
    return pl.pallas_call(
        _ada_kernel,
        grid=(N // tn,),
        in_specs=[pl.BlockSpec((R, D), lambda j: (0, 0)),
                  pl.BlockSpec((D, tn), lambda j: (0, j)),
                  pl.BlockSpec((1, tn), lambda j: (0, j))],
        out_specs=pl.BlockSpec((R, tn), lambda j: (0, j)),
        out_shape=jax.ShapeDtypeStruct((R, N), F32),
        compiler_params=_cparams(("arbitrary",)),
        name="ada_modulation",
    )(cs, w, b.reshape(1, N))


def _nm_mm_kernel(x_ref, gs_ref, sh_ref, w_ref, o_ref, *, cn):
    h = _norm_mod(x_ref[0], gs_ref[0], sh_ref[0]).astype(BF16)
    for c0 in range(0, w_ref.shape[1], cn):
        o_ref[0, :, c0:c0 + cn] = jnp.dot(h, w_ref[:, c0:c0 + cn],
                                          preferred_element_type=F32).astype(o_ref.dtype)


def norm_mod_matmul(x, gs, sh, w, out_dtype=F32):
    B, L, D = x.shape
    N = w.shape[1]
    tm = _tile(L, (512, 256, 128))
    cn = _tile(N, (512, 256, 128))
    return pl.pallas_call(
        functools.partial(_nm_mm_kernel, cn=cn),
        grid=(B, L // tm),
        in_specs=[pl.BlockSpec((1, tm, D), lambda b, i: (b, i, 0)),
                  pl.BlockSpec((1, 1, D), lambda b, i: (b, 0, 0)),
                  pl.BlockSpec((1, 1, D), lambda b, i: (b, 0, 0)),
                  pl.BlockSpec((D, N), lambda b, i: (0, 0), pipeline_mode=pl.Buffered(1))],
        out_specs=pl.BlockSpec((1, tm, N), lambda b, i: (b, i, 0)),
        out_shape=jax.ShapeDtypeStruct((B, L, N), out_dtype),
        compiler_params=_cparams(("parallel", "parallel")),
        name="norm_mod_matmul",
    )(x, gs, sh, w)


def _odd_in_kernel(x_ref, xp_ref, xn_ref, gs_ref, sh_ref, w_ref, cw_ref, cb_ref, o_ref, *, cn, conv_cols):
    tm = x_ref.shape[1]
    i = pl.program_id(1)
    has_prev = (i > 0).astype(F32)
    has_next = (i < pl.num_programs(1) - 1).astype(F32)
    gs = gs_ref[0]
    sh = sh_ref[0]
    h = _norm_mod(x_ref[0], gs, sh).astype(BF16)
    halo = jnp.concatenate([_norm_mod(xp_ref[0], gs, sh), _norm_mod(xn_ref[0], gs, sh)],
                           axis=0).astype(BF16)
    row = lax.broadcasted_iota(jnp.int32, (tm, cn), 0)
    first = row == 0
    last = row == tm - 1
    for c0 in range(0, w_ref.shape[1], cn):
        w = w_ref[:, c0:c0 + cn]
        p = jnp.dot(h, w, preferred_element_type=F32)
        if c0 < conv_cols:
            ph = jnp.dot(halo, w, preferred_element_type=F32)
            pm1 = jnp.where(first, ph[7:8, :] * has_prev, pltpu.roll(p, 1, 0))
            pp1 = jnp.where(last, ph[8:9, :] * has_next, pltpu.roll(p, tm - 1, 0))
            p = (cw_ref[0:1, c0:c0 + cn] * pm1 + cw_ref[1:2, c0:c0 + cn] * p
                 + cw_ref[2:3, c0:c0 + cn] * pp1 + cb_ref[:, c0:c0 + cn])
        o_ref[0, :, c0:c0 + cn] = p


def odd_in_proj(x, gs, sh, w, cw, cb):
    B, L, D = x.shape
    N = w.shape[1]
    conv_cols = cw.shape[1]
    tm = _tile(L, (512, 256, 128))
    cn = _tile(math.gcd(N, conv_cols), (512, 256, 128))
    nb8 = L // 8
    t8 = tm // 8
    const = lambda b, i: (0, 0)
    return pl.pallas_call(
        functools.partial(_odd_in_kernel, cn=cn, conv_cols=conv_cols),
        grid=(B, L // tm),
        in_specs=[pl.BlockSpec((1, tm, D), lambda b, i: (b, i, 0)),
                  pl.BlockSpec((1, 8, D), lambda b, i: (b, jnp.maximum(i * t8 - 1, 0), 0)),
                  pl.BlockSpec((1, 8, D), lambda b, i: (b, jnp.minimum((i + 1) * t8, nb8 - 1), 0)),
                  pl.BlockSpec((1, 1, D), lambda b, i: (b, 0, 0)),
                  pl.BlockSpec((1, 1, D), lambda b, i: (b, 0, 0)),
                  pl.BlockSpec((D, N), const, pipeline_mode=pl.Buffered(1)),
                  pl.BlockSpec((3, conv_cols), const),
                  pl.BlockSpec((1, conv_cols), const)],
        out_specs=pl.BlockSpec((1, tm, N), lambda b, i: (b, i, 0)),
        out_shape=jax.ShapeDtypeStruct((B, L, N), F32),
        compiler_params=_cparams(("parallel", "arbitrary")),
        name="odd_in_proj",
    )(x, x, x, gs, sh, w, cw, cb.reshape(1, conv_cols))


def _even_in_kernel(x_ref, gs_ref, sh_ref, w_ref, seg_ref, cq_ref, sq_ref, ck_ref, sk_ref,
                    q_ref, klo_ref, khi_ref, vlo_ref, vhi_ref, glu_ref):
    tm = x_ref.shape[1]
    h = _norm_mod(x_ref[0], gs_ref[0], sh_ref[0]).astype(BF16)
    seg = seg_ref[...]
    lane = lax.broadcasted_iota(jnp.int32, (tm, LANES), 1)
    even = (lane & 1) == 0
    low = lane < HEAD_DIM

    def proj(col0, width):
        return jnp.dot(h, w_ref[:, col0:col0 + width], preferred_element_type=F32)

    def head_norm_rope(y, c_ref, s_ref):
        sq = y * y
        hi = sq.astype(BF16)
        lo = (sq - hi.astype(F32)).astype(BF16)
        ss = (jnp.dot(hi, seg, preferred_element_type=F32)
              + jnp.dot(lo, seg, preferred_element_type=F32))
        yn = y * lax.rsqrt(ss * (1.0 / HEAD_DIM) + NORM_EPS)
        out = []
        for j in range(2):
            c = yn[:, LANES * j:LANES * (j + 1)]
            sw = jnp.where(even, pltpu.roll(c, LANES - 1, 1), pltpu.roll(c, 1, 1))
            out.append(c * c_ref[...] + sw * s_ref[...])
        return out

    def put_padded(c, j, lo_ref, hi_ref):
        r = pltpu.roll(c, HEAD_DIM, 1)
        dt = lo_ref.dtype
        lo_ref[0, :, LANES * 2 * j:LANES * (2 * j + 1)] = jnp.where(low, c, 0.0).astype(dt)
        hi_ref[0, :, LANES * 2 * j:LANES * (2 * j + 1)] = jnp.where(low, 0.0, r).astype(dt)
        lo_ref[0, :, LANES * (2 * j + 1):LANES * (2 * j + 2)] = jnp.where(low, r, 0.0).astype(dt)
        hi_ref[0, :, LANES * (2 * j + 1):LANES * (2 * j + 2)] = jnp.where(low, 0.0, c).astype(dt)

    n_q = Q_W // 256
    n_k = KV_W // 256
    g0 = Q_W + 2 * KV_W
    y = proj(0, 256)
    for t in range(n_q + n_k):
        y_next = proj(256 * (t + 1), 256) if t + 1 < n_q + n_k else None
        if t < n_q:
            for j, c in enumerate(head_norm_rope(y, cq_ref, sq_ref)):
                q_ref[0, :, 256 * t + LANES * j:256 * t + LANES * (j + 1)] = c.astype(q_ref.dtype)
        else:
            for j, c in enumerate(head_norm_rope(y, ck_ref, sk_ref)):
                put_padded(c, 2 * (t - n_q) + j, klo_ref, khi_ref)
        y = y_next
    v = proj(Q_W + KV_W, KV_W)
    for j in range(KV_W // LANES):
        put_padded(v[:, LANES * j:LANES * (j + 1)], j, vlo_ref, vhi_ref)
    a = proj(g0, CONF_DIM)
    gate = proj(g0 + CONF_DIM, CONF_DIM)
    glu_ref[0] = a * jax.nn.sigmoid(gate)


def even_in_proj(x, gs, sh, w, seg, cq, sq, ck, sk):
    B, L, D = x.shape
    N = w.shape[1]
    tm = _tile(L, (512, 256, 128))
    row = lambda b, i: (b, i, 0)
    tab = pl.BlockSpec((tm, LANES), lambda b, i: (i, 0))
    KP = N_KV_HEADS * LANES
    kv_spec = pl.BlockSpec((1, tm, KP), row)
    kv_shape = jax.ShapeDtypeStruct((B, L, KP), BF16)
    return pl.pallas_call(
        _even_in_kernel,
        grid=(B, L // tm),
        in_specs=[pl.BlockSpec((1, tm, D), row),
                  pl.BlockSpec((1, 1, D), lambda b, i: (b, 0, 0)),
                  pl.BlockSpec((1, 1, D), lambda b, i: (b, 0, 0)),
                  pl.BlockSpec((D, N), lambda b, i: (0, 0), pipeline_mode=pl.Buffered(1)),
                  pl.BlockSpec((256, 256), lambda b, i: (0, 0)),
                  tab, tab, tab, tab],
        out_specs=[pl.BlockSpec((1, tm, Q_W), row), kv_spec, kv_spec, kv_spec, kv_spec,
                   pl.BlockSpec((1, tm, CONF_DIM), row)],
        out_shape=[jax.ShapeDtypeStruct((B, L, Q_W), BF16), kv_shape, kv_shape, kv_shape, kv_shape,
                   jax.ShapeDtypeStruct((B, L, CONF_DIM), F32)],
        compiler_params=_cparams(("parallel", "parallel")),
        name="even_in_proj",
    )(x, gs, sh, w, seg, cq, sq, ck, sk)


def _attn_kernel(q_ref, *refs, n_seg):
    o_ref = refs[4 * n_seg]
    group = N_Q_HEADS // N_KV_HEADS
    nt = (((1,), (1,)), ((), ()))

    def scores(hq):
        pr, pos = divmod(hq, 2)
        q2 = q_ref[0, :, LANES * pr:LANES * (pr + 1)]
        g = hq // group
        cols = slice(LANES * g, LANES * (g + 1))
        return [lax.dot_general(q2, refs[4 * t + pos][0, :, cols], nt, preferred_element_type=F32)
                for t in range(n_seg)]

    ss = scores(0)
    acc = None
    for hq in range(N_Q_HEADS):
        nxt = scores(hq + 1) if hq + 1 < N_Q_HEADS else None
        pr, pos = divmod(hq, 2)
        g = hq // group
        cols = slice(LANES * g, LANES * (g + 1))
        m = functools.reduce(jnp.maximum, [jnp.max(s, axis=-1, keepdims=True) for s in ss])
        ps = [jnp.exp(s - m) for s in ss]
        l = functools.reduce(jnp.add, [jnp.sum(p, axis=-1, keepdims=True) for p in ps])
        o = functools.reduce(jnp.add, [
            jnp.dot(p.astype(BF16), refs[4 * t + 2 + pos][0, :, cols], preferred_element_type=F32)
            for t, p in enumerate(ps)])
        o = o / l
        if pos == 0:
            acc = o
        else:
            o_ref[0, :, LANES * pr:LANES * (pr + 1)] = (acc + o).astype(o_ref.dtype)
        ss = nxt


def attention(q, segments):
    B, Lq, _ = q.shape
    tq = _tile(Lq, (256, 128))
    KP = N_KV_HEADS * LANES
    specs, args = [], []
    for seg in segments:
        for t in seg:
            specs.append(pl.BlockSpec((1, t.shape[1], KP), lambda b, i: (b, 0, 0)))
            args.append(t)
    return pl.pallas_call(
        functools.partial(_attn_kernel, n_seg=len(segments)),
        grid=(B, Lq // tq),
        in_specs=[pl.BlockSpec((1, tq, Q_W), lambda b, i: (b, i, 0))] + specs,
        out_specs=pl.BlockSpec((1, tq, Q_W), lambda b, i: (b, i, 0)),
        out_shape=jax.ShapeDtypeStruct((B, Lq, Q_W), BF16),
        compiler_params=_cparams(("parallel", "parallel")),
        name="attention",
    )(q, *args)


_CONV_PAD = 16


def _dwconv_kernel(x_ref, w_ref, b_ref, g_ref, beta_ref, o_ref, xp_ref, *, K, TR, ln_swish):
    L = x_ref.shape[1]
    C = x_ref.shape[2]
    half = (K - 1) // 2
    xp_ref[0:_CONV_PAD, :] = jnp.zeros((_CONV_PAD, C), F32)
    xp_ref[_CONV_PAD + L:, :] = jnp.zeros((_CONV_PAD, C), F32)
    xp_ref[_CONV_PAD:_CONV_PAD + L, :] = x_ref[0]

    groups = {}
    for k in range(K):
        q8, r8 = divmod(_CONV_PAD - half + k, 8)
        groups.setdefault(r8, []).append((k, q8))
    R = TR + 8
    CW = 256 if C % 256 == 0 else C

    def tile(t, carry):
        r0 = pl.multiple_of(t * TR, TR)
        parts = []
        for c0 in range(0, C, CW):
            acc = None
            for r8, taps in sorted(groups.items()):
                part = None
                for k, q8 in taps:
                    term = (xp_ref[pl.ds(pl.multiple_of(r0 + 8 * q8, 8), R), c0:c0 + CW]
                            * w_ref[k:k + 1, c0:c0 + CW])
                    part = term if part is None else part + term
                if r8:
                    part = pltpu.roll(part, R - r8, 0)
                part = part[:TR]
                acc = part if acc is None else acc + part
            parts.append(acc + b_ref[:, c0:c0 + CW])
        acc = parts[0] if len(parts) == 1 else jnp.concatenate(parts, axis=-1)
        if ln_swish:
            mu = jnp.mean(acc, axis=-1, keepdims=True)
            d = acc - mu
            var = jnp.mean(d * d, axis=-1, keepdims=True)
            y = d * lax.rsqrt(var + NORM_EPS) * g_ref[...] + beta_ref[...]
            acc = y * jax.nn.sigmoid(y)
        o_ref[0, pl.ds(r0, TR), :] = acc.astype(o_ref.dtype)
        return carry

    lax.fori_loop(0, L // TR, tile, 0)


def dwconv(x, w, b, g=None, beta=None, *, out_dtype=F32):
    B, L, _ = x.shape
    K, C = w.shape
    ln_swish = g is not None
    tc = C if ln_swish else _tile(C, (512, 256, 128))
    TR = _tile(L, (64, 32, 16, 8))
    if g is None:
        g = jnp.ones((C,), F32)
        beta = jnp.zeros((C,), F32)
    vec = pl.BlockSpec((1, tc), lambda bb, j: (0, j))
    return pl.pallas_call(
        functools.partial(_dwconv_kernel, K=K, TR=TR, ln_swish=ln_swish),
        grid=(B, C // tc),
        in_specs=[pl.BlockSpec((1, L, tc), lambda bb, j: (bb, 0, j)),
                  pl.BlockSpec((K, tc), lambda bb, j: (0, j)),
                  vec, vec, vec],
        out_specs=pl.BlockSpec((1, L, tc), lambda bb, j: (bb, 0, j)),
        out_shape=jax.ShapeDtypeStruct((B, L, C), out_dtype),
        scratch_shapes=[pltpu.VMEM((L + 2 * _CONV_PAD, tc), F32)],
        compiler_params=_cparams(("parallel", "parallel")),
        name="dwconv%d" % K,
    )(x, w, b.reshape(1, C), g.reshape(1, C), beta.reshape(1, C))


def _out_proj_kernel(x_ref, a1_ref, a2_ref, w1_ref, w2_ref, g_ref, o_ref):
    y = jnp.dot(a1_ref[0], w1_ref[...], preferred_element_type=F32)
    y = y + jnp.dot(a2_ref[0], w2_ref[...], preferred_element_type=F32)
    o_ref[0] = x_ref[0] + g_ref[0] * y


def out_proj_residual(x, a1, a2, w1, w2, gate):
    B, L, D = x.shape
    K1, K2 = a1.shape[2], a2.shape[2]
    tm = _tile(L, (512, 256, 128))
    row = lambda b, i: (b, i, 0)
    return pl.pallas_call(
        _out_proj_kernel,
        grid=(B, L // tm),
        in_specs=[pl.BlockSpec((1, tm, D), row),
                  pl.BlockSpec((1, tm, K1), row),
                  pl.BlockSpec((1, tm, K2), row),
                  pl.BlockSpec((K1, D), lambda b, i: (0, 0)),
                  pl.BlockSpec((K2, D), lambda b, i: (0, 0)),
                  pl.BlockSpec((1, 1, D), lambda b, i: (b, 0, 0))],
        out_specs=pl.BlockSpec((1, tm, D), row),
        out_shape=jax.ShapeDtypeStruct((B, L, D), F32),
        compiler_params=_cparams(("parallel", "parallel")),
        name="out_proj_residual",
    )(x, a1, a2, w1, w2, gate)


_FFN_CHUNK = 256


def _ffn_kernel(x_ref, xp_ref, xn_ref, gs_ref, sh_ref, wup_ref, cw_ref, cb_ref, wdn_ref, g5_ref,
                fg_ref, o_ref, acc_ref, *, final_norm):
    tm = x_ref.shape[1]
    FF = wdn_ref.shape[0]
    i = pl.program_id(1)
    has_prev = (i > 0).astype(F32)
    has_next = (i < pl.num_programs(1) - 1).astype(F32)
    gs = gs_ref[0]
    sh = sh_ref[0]
    x = x_ref[0]
    h = _norm_mod(x, gs, sh).astype(BF16)
    halo = jnp.concatenate([_norm_mod(xp_ref[0], gs, sh), _norm_mod(xn_ref[0], gs, sh)],
                           axis=0).astype(BF16)
    row = lax.broadcasted_iota(jnp.int32, (tm, _FFN_CHUNK), 0)
    first = row == 0
    last = row == tm - 1

    hcat = jnp.concatenate([h, halo], axis=0)

    def up(c0):
        r = jnp.dot(hcat, wup_ref[:, c0:c0 + _FFN_CHUNK], preferred_element_type=F32)
        return r[:tm], r[tm:]

    def conv3(c0, p, ph):
        prev_row = ph[7:8, :] * has_prev
        next_row = ph[8:9, :] * has_next
        pm1 = jnp.where(first, prev_row, pltpu.roll(p, 1, 0))
        pp1 = jnp.where(last, next_row, pltpu.roll(p, tm - 1, 0))
        return (cw_ref[0:1, c0:c0 + _FFN_CHUNK] * pm1 + cw_ref[1:2, c0:c0 + _FFN_CHUNK] * p
                + cw_ref[2:3, c0:c0 + _FFN_CHUNK] * pp1 + cb_ref[:, c0:c0 + _FFN_CHUNK])

    n_chunks = FF // _FFN_CHUNK
    ahead = 2
    ups = [(up(c * _FFN_CHUNK), up(FF + c * _FFN_CHUNK)) for c in range(min(ahead, n_chunks))]
    for c in range(n_chunks):
        if c + ahead < n_chunks:
            ups.append((up((c + ahead) * _FFN_CHUNK), up(FF + (c + ahead) * _FFN_CHUNK)))
        val = conv3(c * _FFN_CHUNK, *ups[c][0])
        gate = conv3(FF + c * _FFN_CHUNK, *ups[c][1])
        ups[c] = None
        act = (gate * jax.nn.sigmoid(gate) * val).astype(BF16)
        part = jnp.dot(act, wdn_ref[c * _FFN_CHUNK:(c + 1) * _FFN_CHUNK, :],
                       preferred_element_type=F32)
        if c == 0:
            acc_ref[...] = part
        else:
            acc_ref[...] += part
    y = x + g5_ref[0] * acc_ref[...]
    if final_norm:
        y = y * lax.rsqrt(jnp.mean(y * y, axis=-1, keepdims=True) + NORM_EPS) * fg_ref[...]
    o_ref[0] = y


def conv_ffn_residual(x, gs, sh, w_up, cw, cb, w_dn, gate, final_g=None):
    B, L, D = x.shape
    FF = w_dn.shape[0]
    assert FF % _FFN_CHUNK == 0
    tm = _tile(L, (256, 128))
    nb8 = L // 8
    t8 = tm // 8
    final_norm = final_g is not None
    if final_g is None:
        final_g = jnp.ones((D,), F32)
    row = lambda b, i: (b, i, 0)
    const = lambda b, i: (0, 0)
    return pl.pallas_call(
        functools.partial(_ffn_kernel, final_norm=final_norm),
        grid=(B, L // tm),
        in_specs=[pl.BlockSpec((1, tm, D), row),
                  pl.BlockSpec((1, 8, D), lambda b, i: (b, jnp.maximum(i * t8 - 1, 0), 0)),
                  pl.BlockSpec((1, 8, D), lambda b, i: (b, jnp.minimum((i + 1) * t8, nb8 - 1), 0)),
                  pl.BlockSpec((1, 1, D), lambda b, i: (b, 0, 0)),
                  pl.BlockSpec((1, 1, D), lambda b, i: (b, 0, 0)),
                  pl.BlockSpec((D, 2 * FF), const, pipeline_mode=pl.Buffered(1)),
                  pl.BlockSpec((3, 2 * FF), const),
                  pl.BlockSpec((1, 2 * FF), const),
                  pl.BlockSpec((FF, D), const, pipeline_mode=pl.Buffered(1)),
                  pl.BlockSpec((1, 1, D), lambda b, i: (b, 0, 0)),
                  pl.BlockSpec((1, D), const)],
        out_specs=pl.BlockSpec((1, tm, D), row),
        out_shape=jax.ShapeDtypeStruct((B, L, D), F32),
        scratch_shapes=[pltpu.VMEM((tm, D), F32)],
        compiler_params=_cparams(("parallel", "arbitrary")),
        name="conv_ffn",
    )(x, x, x, gs, sh, w_up, cw, cb.reshape(1, 2 * FF), w_dn, gate, final_g.reshape(1, D))


def _hy_filter_kernel(feat_ref, w1_ref, b1_ref, w2_ref, b2_ref, w3_ref, fr_ref, dec_ref,
                      hs_ref, hd_ref, nyq_ref):
    L = feat_ref.shape[0]
    fr = fr_ref[...]
    z = jnp.dot(feat_ref[...].astype(BF16), w1_ref[...].astype(BF16), preferred_element_type=F32)
    hdn = jnp.sin(fr * (z + b1_ref[...]))
    z = jnp.dot(hdn.astype(BF16), w2_ref[...].astype(BF16), preferred_element_type=F32)
    hdn = jnp.sin(fr * (z + b2_ref[...])).astype(BF16)
    row = lax.broadcasted_iota(jnp.int32, (L, HY_DIM), 0)
    not_first = (row > 0).astype(F32)
    sign = jnp.where((row & 1) == 0, 1.0, -1.0).astype(F32)
    dec = dec_ref[...]
    n_order = w3_ref.shape[1] // (2 * HY_DIM)
    for o in range(n_order):
        c0 = o * 2 * HY_DIM
        hf = jnp.dot(hdn, w3_ref[:, c0:c0 + HY_DIM].astype(BF16), preferred_element_type=F32) * dec
        hb = (jnp.dot(hdn, w3_ref[:, c0 + HY_DIM:c0 + 2 * HY_DIM].astype(BF16),
                      preferred_element_type=F32) * dec * not_first)
        inv = 1.0 / (jnp.sum(jnp.abs(hf), axis=0, keepdims=True)
                     + jnp.sum(jnp.abs(hb), axis=0, keepdims=True))
        hs = (hf + hb) * inv
        hd = (hf - hb) * inv
        hs_ref[:, o * HY_DIM:(o + 1) * HY_DIM] = hs
        hd_ref[:, o * HY_DIM:(o + 1) * HY_DIM] = hd
        nyq_ref[:, o * HY_DIM:(o + 1) * HY_DIM] = jnp.sum(hs * sign, axis=0, keepdims=True)


def hyena_filter_taps(L, w1, b1, w2, b2, w3, freq):
    emb = w1.shape[0]
    bands = (emb - 1) // 2
    t = jnp.linspace(0.0, 1.0, L, dtype=F32)[:, None]
    fb = jnp.linspace(1e-4, bands - 1, bands, dtype=F32)
    wpos = (2.0 * math.pi / L) * jnp.arange(L, dtype=F32)[:, None]
    feats = jnp.concatenate([t, jnp.cos(fb * wpos), -jnp.sin(fb * wpos)], axis=-1)
    feats = jnp.pad(feats, ((0, 0), (0, LANES - emb)))
    w1p = jnp.pad(w1, ((0, LANES - emb), (0, 0)))
    deltas = jnp.abs(jnp.linspace(math.log(C_DECAY_TARGET) / C_SLOW_DECAY,
                                  math.log(C_DECAY_TARGET) / C_FAST_DECAY, HY_DIM, dtype=F32))
    dec = jnp.exp(-t * deltas)
    hid = w1.shape[1]
    NO = w3.shape[1] // 2
    full = lambda shape: pl.BlockSpec(shape, lambda: tuple(0 for _ in shape))
    return pl.pallas_call(
        _hy_filter_kernel,
        in_specs=[full((L, LANES)), full((LANES, hid)), full((1, hid)), full((hid, hid)),
                  full((1, hid)), full((hid, 2 * NO)), full((1, hid)), full((L, HY_DIM))],
        out_specs=[full((L, NO)), full((L, NO)), full((1, NO))],
        out_shape=[jax.ShapeDtypeStruct((L, NO), F32), jax.ShapeDtypeStruct((L, NO), F32),
                   jax.ShapeDtypeStruct((1, NO), F32)],
        compiler_params=pltpu.CompilerParams(vmem_limit_bytes=V7X_VMEM_LIMIT),
        name="hyena_filter",
    )(feats, w1p, b1.reshape(1, hid), w2, b2.reshape(1, hid), w3, freq.reshape(1, hid), dec)


_DFT_SPLIT = 64


def _dft_tables(L):
    N = 2 * L
    n = jnp.arange(L, dtype=jnp.int32)

    def cos_sin(mult):
        ang = ((mult[:, None] * n[None, :]) & (N - 1)).astype(F32) * (2.0 * math.pi / N)
        return jnp.cos(ang), jnp.sin(ang)

    ca, sa = cos_sin(jnp.arange(L // _DFT_SPLIT, dtype=jnp.int32) * _DFT_SPLIT)
    cb, sb = cos_sin(jnp.arange(_DFT_SPLIT, dtype=jnp.int32))
    nyq = jnp.where((n & 1) == 0, 1.0, -1.0).astype(F32)
    re = (ca[:, None, :] * cb[None] - sa[:, None, :] * sb[None]).reshape(L, L)
    im = -(sa[:, None, :] * cb[None] + ca[:, None, :] * sb[None]).reshape(L, L)
    im = jnp.where(jnp.arange(L)[:, None] == 0, nyq[None, :], im)
    f = jnp.concatenate([re, im], axis=0).astype(BF16)
    cat, sat, cbt, sbt = ca.T, sa.T, cb.T, sb.T
    re_t = (cat[:, :, None] * cbt[:, None, :] - sat[:, :, None] * sbt[:, None, :]).reshape(L, L)
    im_t = -(sat[:, :, None] * cbt[:, None, :] + cat[:, :, None] * sbt[:, None, :]).reshape(L, L)
    im_t = jnp.where(jnp.arange(L)[None, :] == 0, nyq[:, None], im_t)
    g = jnp.concatenate([re_t, im_t], axis=1).astype(BF16)
    return f, g


def _hy_spec_kernel(fre_ref, fim_ref, hs_ref, hd_ref, nyq_ref, a_ref, b_ref):
    kb = fre_ref.shape[0]
    L = fre_ref.shape[1]
    NO = hs_ref.shape[1]
    krow = lax.broadcasted_iota(jnp.int32, (kb, NO), 0) + pl.program_id(0) * kb
    dc = krow == 0
    scale = jnp.where(dc, 1.0, 2.0).astype(F32) * (1.0 / (2 * L))
    hr = jnp.dot(fre_ref[...], hs_ref[...].astype(BF16), preferred_element_type=F32)
    hi = jnp.dot(fim_ref[...], hd_ref[...].astype(BF16), preferred_element_type=F32)
    a_ref[...] = hr * scale
    b_ref[...] = jnp.where(dc, nyq_ref[...], hi) * scale


def hyena_spectrum(F, hs, hd, nyq):
    L, NO = hs.shape
    kb = _tile(L, (256, 128))
    nk = L // kb
    return pl.pallas_call(
        _hy_spec_kernel,
        grid=(nk,),
        in_specs=[pl.BlockSpec((kb, L), lambda i: (i, 0)),
                  pl.BlockSpec((kb, L), lambda i: (i + nk, 0)),
                  pl.BlockSpec((L, NO), lambda i: (0, 0)),
                  pl.BlockSpec((L, NO), lambda i: (0, 0)),
                  pl.BlockSpec((1, NO), lambda i: (0, 0))],
        out_specs=[pl.BlockSpec((kb, NO), lambda i: (i, 0)),
                   pl.BlockSpec((kb, NO), lambda i: (i, 0))],
        out_shape=[jax.ShapeDtypeStruct((L, NO), F32), jax.ShapeDtypeStruct((L, NO), F32)],
        compiler_params=_cparams(("arbitrary",)),
        name="hyena_spectrum",
    )(F, F, hs, hd, nyq)


def _hy_conv_kernel(z_ref, gate_ref, fre_ref, fim_ref, gre_ref, gim_ref, a_ref, b_ref, skip_ref,
                    o_ref, zb_ref, y_ref):
    kb = fre_ref.shape[0]
    C = z_ref.shape[2]
    kstep = pl.program_id(1)

    @pl.when(kstep == 0)
    def _():
        zb_ref[...] = z_ref[0].astype(BF16)
        y_ref[...] = jnp.zeros(y_ref.shape, F32)

    zb = zb_ref[...]
    NH = 2
    hb = kb // NH
    fwd = []
    for hf in range(NH):
        rows = slice(hf * hb, (hf + 1) * hb)
        fwd.append((jnp.dot(fre_ref[rows, :], zb, preferred_element_type=F32),
                    jnp.dot(fim_ref[rows, :], zb, preferred_element_type=F32)))
    for hf in range(NH):
        rows = slice(hf * hb, (hf + 1) * hb)
        ur, ui = fwd[hf]
        a = a_ref[rows, :]
        b = b_ref[rows, :]
        krow = lax.broadcasted_iota(jnp.int32, (hb, C), 0) + (kstep * kb + hf * hb)
        dc = krow == 0
        vr = jnp.where(dc, ur * a, ur * a - ui * b)
        vi = jnp.where(dc, ui * b, ur * b + ui * a)
        y_ref[...] += (jnp.dot(gre_ref[:, rows], vr.astype(BF16), preferred_element_type=F32)
                       + jnp.dot(gim_ref[:, rows], vi.astype(BF16), preferred_element_type=F32))

    @pl.when(kstep == pl.num_programs(1) - 1)
    def _():
        o_ref[0] = (gate_ref[0] * (y_ref[...] + skip_ref[...] * z_ref[0])).astype(o_ref.dtype)


def hyena_conv(z_src, z_col, gate_src, gate_col, F, G, A, Bt, spec_col, skip, out_dtype):
    B, L, _ = z_src.shape
    C = HY_DIM
    kb = _tile(L, (512, 256))
    nk = L // kb
    return pl.pallas_call(
        _hy_conv_kernel,
        grid=(B, nk),
        in_specs=[pl.BlockSpec((1, L, C), lambda b, k: (b, 0, z_col)),
                  pl.BlockSpec((1, L, C), lambda b, k: (b, 0, gate_col)),
                  pl.BlockSpec((kb, L), lambda b, k: (k, 0)),
                  pl.BlockSpec((kb, L), lambda b, k: (k + nk, 0)),
                  pl.BlockSpec((L, kb), lambda b, k: (0, k)),
                  pl.BlockSpec((L, kb), lambda b, k: (0, k + nk)),
                  pl.BlockSpec((kb, C), lambda b, k: (k, spec_col)),
                  pl.BlockSpec((kb, C), lambda b, k: (k, spec_col)),
                  pl.BlockSpec((1, C), lambda b, k: (0, 0))],
        out_specs=pl.BlockSpec((1, L, C), lambda b, k: (b, 0, 0)),
        out_shape=jax.ShapeDtypeStruct((B, L, C), out_dtype),
        scratch_shapes=[pltpu.VMEM((L, C), BF16), pltpu.VMEM((L, C), F32)],
        compiler_params=_cparams(("parallel", "arbitrary")),
        name="hyena_conv",
    )(z_src, gate_src, F, F, G, G, A, Bt, skip.reshape(1, C))


HG_GROUP = 128


def _hgrn2_kernel(ql_ref, ffl_ref, fbl_ref, il_ref, gl_ref, qc_ref, ffc_ref, fbc_ref, ic_ref,
                  lb_ref, gn_ref, o_ref, vb_ref, qdf_ref, qdb_ref, kuf_ref, kub_ref,
                  kdf_ref, kdb_ref, decf_ref, decb_ref, oif_ref, oib_ref, stf_ref, stb_ref):
    C = HG_CHUNK
    GR = HG_GROUP
    L = ql_ref.shape[1]
    Lc = qc_ref.shape[1]
    r = lax.broadcasted_iota(jnp.int32, (GR, GR), 0)
    c = lax.broadcasted_iota(jnp.int32, (GR, GR), 1)
    same = (r // C) == (c // C)
    dirs = []
    for d, (incl, excl) in enumerate(((c <= r, c > r), (c >= r, c < r))):
        mask = same & incl
        dirs.append((lb_ref[d:d + 1, :], mask.astype(BF16), (same & excl).astype(BF16), mask))
    f_refs_lat = (ffl_ref, fbl_ref)
    f_refs_ctx = (ffc_ref, fbc_ref)
    qd_refs = (qdf_ref, qdb_ref)
    ku_refs = (kuf_ref, kub_ref)
    kd_refs = (kdf_ref, kdb_ref)
    dec_refs = (decf_ref, decb_ref)
    oi_refs = (oif_ref, oib_ref)
    st_refs = (stf_ref, stb_ref)
    nt = (((1,), (1,)), ((), ()))

    def gates(q_ref, f_refs, i_ref, n_rows, row_off, with_out):
        def body(gi, carry):
            r0 = pl.multiple_of(gi * GR, GR)
            ro = pl.multiple_of(row_off + gi * GR, GR)
            q = q_ref[0, pl.ds(r0, GR), :]
            vb_ref[pl.ds(ro, GR), :] = i_ref[0, pl.ds(r0, GR), :].astype(BF16)
            for d, (lbd, t_in, t_ex, _) in enumerate(dirs):
                f = lbd + (1.0 - lbd) * jax.nn.sigmoid(f_refs[d][0, pl.ds(r0, GR), :])
                k = 1.0 - f
                lg = jnp.log(f)
                hi = lg.astype(BF16)
                lo = (lg - hi.astype(F32)).astype(BF16)
                b = (jnp.dot(t_in, hi, preferred_element_type=F32)
                     + jnp.dot(t_in, lo, preferred_element_type=F32))
                rest = (jnp.dot(t_ex, hi, preferred_element_type=F32)
                        + jnp.dot(t_ex, lo, preferred_element_type=F32))
                qd_refs[d][pl.ds(ro, GR), :] = (q * jnp.exp(b)).astype(BF16)
                ku_refs[d][pl.ds(ro, GR), :] = (k * jnp.exp(rest)).astype(BF16)
                dec_refs[d][pl.ds(ro, GR), :] = b + rest
                if with_out:
                    kd_refs[d][pl.ds(r0, GR), :] = (k * jnp.exp(-b)).astype(BF16)
            return carry

        lax.fori_loop(0, n_rows // GR, body, 0, unroll=2)

    def intra():
        U = 4 if (L // GR) % 4 == 0 else 1

        def body(gi, carry):
            jobs = []
            for u in range(U):
                r0 = pl.multiple_of((gi * U + u) * GR, GR)
                ro = pl.multiple_of(Lc + (gi * U + u) * GR, GR)
                for d in range(2):
                    sc = lax.dot_general(qd_refs[d][pl.ds(ro, GR), :], kd_refs[d][pl.ds(r0, GR), :], nt,
                                         preferred_element_type=F32)
                    jobs.append((d, r0, ro, sc))
            for d, r0, ro, sc in jobs:
                sc = jnp.where(dirs[d][3], sc, 0.0).astype(BF16)
                oi_refs[d][pl.ds(r0, GR), :] = jnp.dot(sc, vb_ref[pl.ds(ro, GR), :],
                                                       preferred_element_type=F32)
            return carry

        lax.fori_loop(0, L // (GR * U), body, 0)

    def recurrence(n_rows, row_off, keep, carry):
        n = n_rows // C

        def body(j, states):
            new = []
            for d, st in enumerate(states):
                jj = j if d == 0 else n - 1 - j
                ro = pl.multiple_of(row_off + jj * C, C)
                if keep:
                    st_refs[d][pl.ds(pl.multiple_of(jj * HG_DIM, HG_DIM), HG_DIM), :] = st.astype(BF16)
                upd = lax.dot_general(vb_ref[pl.ds(ro, C), :], ku_refs[d][pl.ds(ro, C), :],
                                      (((0,), (0,)), ((), ())), preferred_element_type=F32)
                new.append(st * jnp.exp(dec_refs[d][pl.ds(ro, 1), :]) + upd)
            return tuple(new)

        return lax.fori_loop(0, n, body, carry, unroll=8)

    def inter():
        def body(j, carry):
            r0 = pl.multiple_of(j * C, C)
            ro = pl.multiple_of(Lc + j * C, C)
            so = pl.multiple_of(j * HG_DIM, HG_DIM)
            for d in range(2):
                o = lax.dot_general(qd_refs[d][pl.ds(ro, C), :], st_refs[d][pl.ds(so, HG_DIM), :], nt,
                                    preferred_element_type=F32)
                oi_refs[d][pl.ds(r0, C), :] += o
            return carry

        lax.fori_loop(0, L // C, body, 0, unroll=8)

    gates(qc_ref, f_refs_ctx, ic_ref, Lc, 0, False)
    gates(ql_ref, f_refs_lat, il_ref, L, Lc, True)
    intra()
    zero = jnp.zeros((HG_DIM, HG_DIM), F32)
    carry = recurrence(Lc, 0, False, (zero, zero))
    recurrence(L, Lc, True, carry)
    inter()
    o = oif_ref[...] + oib_ref[...]
    o = o * lax.rsqrt(jnp.mean(o * o, axis=-1, keepdims=True) + NORM_EPS) * gn_ref[...]
    g = gl_ref[0]
    o_ref[0] = (o * (g * jax.nn.sigmoid(g))).astype(o_ref.dtype)


def hgrn2_mixer(p_lat, lat_col0, p_ctx, ctx_col0, lb, gn_g):
    B, L, _ = p_lat.shape
    Lc = p_ctx.shape[1]
    lc0 = lat_col0 // HG_DIM
    cc0 = ctx_col0 // HG_DIM

    def lat(seg):
        return pl.BlockSpec((1, L, HG_DIM), lambda b, h: (b, 0, lc0 + HG_HEADS * seg + h))

    def ctx(seg):
        return pl.BlockSpec((1, Lc, HG_DIM), lambda b, h: (b, 0, cc0 + HG_HEADS * seg + h))

    return pl.pallas_call(
        _hgrn2_kernel,
        grid=(B, HG_HEADS),
        in_specs=[lat(0), lat(1), lat(2), lat(3), lat(4), ctx(0), ctx(1), ctx(2), ctx(3),
                  pl.BlockSpec((2, HG_DIM), lambda b, h: (0, h)),
                  pl.BlockSpec((1, HG_DIM), lambda b, h: (0, 0))],
        out_specs=pl.BlockSpec((1, L, HG_DIM), lambda b, h: (b, 0, h)),
        out_shape=jax.ShapeDtypeStruct((B, L, HG_W), BF16),
        scratch_shapes=([pltpu.VMEM((Lc + L, HG_DIM), BF16)] * 5
                        + [pltpu.VMEM((L, HG_DIM), BF16)] * 2
                        + [pltpu.VMEM((Lc + L, HG_DIM), F32)] * 2
                        + [pltpu.VMEM((L, HG_DIM), F32)] * 2
                        + [pltpu.VMEM((L // HG_CHUNK * HG_DIM, HG_DIM), BF16)] * 2),
        compiler_params=_cparams(("parallel", "parallel")),
        name="hgrn2",
    )(p_lat, p_lat, p_lat, p_lat, p_lat, p_ctx, p_ctx, p_ctx, p_ctx, lb, gn_g.reshape(1, HG_DIM))


def _rope_tables(L, gain, scale, rotate):
    g2 = jnp.tile(gain.astype(F32), LANES // HEAD_DIM)[None, :] * scale
    gsw = g2.reshape(1, LANES // 2, 2)[:, :, ::-1].reshape(1, LANES)
    if not rotate:
        return jnp.broadcast_to(g2, (L, LANES)), jnp.zeros((L, LANES), F32)
    rows = L // GRID_W
    row = jnp.repeat(jnp.arange(rows, dtype=F32), GRID_W)
    col = jnp.tile(jnp.arange(GRID_W, dtype=F32), rows)
    axis_dim = HEAD_DIM // 2
    inv_freq = ROPE_THETA ** (-jnp.arange(0, axis_dim, 2, dtype=F32) / axis_dim)
    ang = jnp.concatenate([row[:, None] * inv_freq, col[:, None] * inv_freq], axis=-1)
    cos = jnp.tile(jnp.repeat(jnp.cos(ang), 2, axis=-1), (1, LANES // HEAD_DIM))
    sin = jnp.tile(jnp.repeat(jnp.sin(ang), 2, axis=-1), (1, LANES // HEAD_DIM))
    sign = jnp.tile(jnp.array([-1.0, 1.0], F32), LANES // 2)[None, :]
    return cos * g2, sin * sign * gsw


def _split6(m):
    return jnp.split(m, 6, axis=-1)


def kernel(x, c, ctx, c_ctx, ada_w, ada_b, norm1_g, norm2_g, final_g, ev_w_in, ev_q_gain, ev_k_gain,
           ev_conv_w, ev_conv_b, ev_ln_g, ev_ln_b, ev_w_out, od_w_in, od_short_w, od_short_b,
           od_filt_w1, od_filt_b1, od_filt_w2, od_filt_b2, od_filt_w3, od_filt_freq, od_hyena_skip,
           od_lower_bound, od_gnorm_g, od_w_out, ffn_w_up, ffn_conv_w, ffn_conv_b, ffn_w_down):
    B, L, D = x.shape
    Lc = ctx.shape[1]
    depth = ada_w.shape[0]
    assert depth == 2, "layer 0 (attention + conformer) followed by layer 1 (hyena + hgrn2)"

    lb_soft = jax.nn.softmax(od_lower_bound.astype(F32), axis=0)
    lower_bounds = jnp.cumsum(lb_soft, axis=0) - lb_soft[0:1]

    cond = jnp.concatenate([c, c_ctx[None, :], jnp.zeros((7, D), F32)], axis=0)

    def mods(l):
        m = ada_modulation(cond, ada_w[l], ada_b[l])
        ml = [t[:, None, :] for t in _split6(m[:B])]
        mc = [jnp.broadcast_to(t[None], (B, 1, D)) for t in _split6(m[B:B + 1])]
        return ml, mc

    def ffn(xx, mm, l, final=None):
        return conv_ffn_residual(xx, norm2_g[l] * (1.0 + mm[4]), mm[3], ffn_w_up[l].astype(BF16),
                                 ffn_conv_w[l], ffn_conv_b[l], ffn_w_down[l].astype(BF16), mm[5], final)

    ml, mc = mods(0)
    w_in = ev_w_in[0].astype(BF16)
    seg = jnp.kron(jnp.eye(256 // HEAD_DIM, dtype=F32), jnp.ones((HEAD_DIM, HEAD_DIM), F32)).astype(BF16)
    scale = HEAD_DIM ** -0.5
    cq_l, sq_l = _rope_tables(L, ev_q_gain[0], scale, True)
    ck_l, sk_l = _rope_tables(L, ev_k_gain[0], 1.0, True)
    cq_c, sq_c = _rope_tables(Lc, ev_q_gain[0], scale, False)
    ck_c, sk_c = _rope_tables(Lc, ev_k_gain[0], 1.0, False)
    q_l, *kv_l, glu_l = even_in_proj(x, norm1_g[0] * (1.0 + ml[1]), ml[0], w_in, seg,
                                     cq_l, sq_l, ck_l, sk_l)
    q_c, *kv_c, glu_c = even_in_proj(ctx, norm1_g[0] * (1.0 + mc[1]), mc[0], w_in, seg,
                                     cq_c, sq_c, ck_c, sk_c)
    attn_l = attention(q_l, [kv_c, kv_l])
    attn_c = attention(q_c, [kv_c])
    conv_l = dwconv(glu_l, ev_conv_w[0], ev_conv_b[0], ev_ln_g[0], ev_ln_b[0], out_dtype=BF16)
    conv_c = dwconv(glu_c, ev_conv_w[0], ev_conv_b[0], ev_ln_g[0], ev_ln_b[0], out_dtype=BF16)
    wo1 = ev_w_out[0][:Q_W].astype(BF16)
    wo2 = ev_w_out[0][Q_W:].astype(BF16)
    x = out_proj_residual(x, attn_l, conv_l, wo1, wo2, ml[2])
    ctx = out_proj_residual(ctx, attn_c, conv_c, wo1, wo2, mc[2])
    x = ffn(x, ml, 0)
    ctx = ffn(ctx, mc, 0)

    ml, mc = mods(1)
    w_in = od_w_in[0].astype(BF16)
    p_lat = odd_in_proj(x, norm1_g[1] * (1.0 + ml[1]), ml[0], w_in, od_short_w[0], od_short_b[0])
    p_ctx = norm_mod_matmul(ctx, norm1_g[1] * (1.0 + mc[1]), mc[0], w_in[:, HY_IN:])
    d_lat = hgrn2_mixer(p_lat, HY_IN, p_ctx, 0, lower_bounds[1], od_gnorm_g[0])

    hs, hd, nyq = hyena_filter_taps(L, od_filt_w1[0], od_filt_b1[0], od_filt_w2[0], od_filt_b2[0],
                                    od_filt_w3[0], od_filt_freq[0])
    F, G = _dft_tables(L)
    A, Bt = hyena_spectrum(F, hs, hd, nyq)
    z1 = hyena_conv(p_lat, 0, p_lat, 1, F, G, A, Bt, 0, od_hyena_skip[0][0], F32)
    c_lat = hyena_conv(z1, 0, p_lat, 2, F, G, A, Bt, 1, od_hyena_skip[0][1], BF16)

    wo = od_w_out[0].astype(BF16)
    x = out_proj_residual(x, c_lat, d_lat, wo[:HY_DIM], wo[HY_DIM:], ml[2])
    return ffn(x, ml, 1, final_g)
```

```python
import functools
import math

import jax
import jax.numpy as jnp
from jax import lax
from jax.experimental import pallas as pl
from jax.experimental.pallas import tpu as pltpu

F32 = jnp.float32
BF16 = jnp.bfloat16

NORM_EPS = 1e-6
GRID_W = 64
ROPE_THETA = 10000.0
HEAD_DIM = 64
N_Q_HEADS = 12
N_KV_HEADS = 4
Q_W = N_Q_HEADS * HEAD_DIM
KV_W = N_KV_HEADS * HEAD_DIM
CONF_DIM = 512
HY_DIM = 512
HY_IN = 3 * HY_DIM
HG_HEADS = 4
HG_DIM = 128
HG_W = HG_HEADS * HG_DIM
HG_CHUNK = 32
C_FAST_DECAY = 0.3
C_SLOW_DECAY = 1.5
C_DECAY_TARGET = 1e-2
V7X_VMEM_LIMIT = 56 * 1024 * 1024
LANES = 128


def _cparams(sem):
    return pltpu.CompilerParams(dimension_semantics=sem, vmem_limit_bytes=V7X_VMEM_LIMIT)


def _tile(n, prefs):
    for p in prefs:
        if n % p == 0:
            return p
    return n


def _norm_mod(x, gs, sh):
    r = lax.rsqrt(jnp.mean(x * x, axis=-1, keepdims=True) + NORM_EPS)
    return x * r * gs + sh


def _cast_kernel(w_ref, o_ref):
    o_ref[...] = w_ref[0].astype(o_ref.dtype)


def layer_weight_bf16(w, layer):
    _, R, C = w.shape
    tr = _tile(R, (256, 128, 64, 32, 16))
    return pl.pallas_call(
        _cast_kernel,
        grid=(R // tr,),
        in_specs=[pl.BlockSpec((1, tr, C), lambda i: (layer, i, 0))],
        out_specs=pl.BlockSpec((tr, C), lambda i: (i, 0)),
        out_shape=jax.ShapeDtypeStruct((R, C), BF16),
        compiler_params=_cparams(("parallel",)),
        name="layer_weight_bf16",
    )(w)


def _ada_kernel(c_ref, w_ref, b_ref, o_ref):
    c = c_ref[...]
    s = (c * jax.nn.sigmoid(c)).astype(BF16)
    o_ref[...] = jnp.dot(s, w_ref[...].astype(BF16), preferred_element_type=F32) + b_ref[...]


def ada_modulation(cs, w, b):
    R, D = cs.shape
    N = w.shape[1]
    tn = _tile(N, (1024, 512, 256, 128))
    return pl.pallas_call(
        _ada_kernel,
        grid=(N // tn,),
        in_specs=[pl.BlockSpec((R, D), lambda j: (0, 0)),
                  pl.BlockSpec((D, tn), lambda j: (0, j)),
                  pl.BlockSpec((1, tn), lambda j: (0, j))],
        out_specs=pl.BlockSpec((R, tn), lambda j: (0, j)),
        out_shape=jax.ShapeDtypeStruct((R, N), F32),
        compiler_params=_cparams(("arbitrary",)),
        name="ada_modulation",
    )(cs, w, b.reshape(1, N))


def _nm_mm_kernel(x_ref, gs_ref, sh_ref, w_ref, o_ref, *, cn):
    h = _norm_mod(x_ref[0], gs_ref[0], sh_ref[0]).astype(BF16)
    for c0 in range(0, w_ref.shape[1], cn):
        o_ref[0, :, c0:c0 + cn] = jnp.dot(h, w_ref[:, c0:c0 + cn],
                                          preferred_element_type=F32).astype(o_ref.dtype)


def norm_mod_matmul(x, gs, sh, w, out_dtype=F32):
    B, L, D = x.shape
    N = w.shape[1]
    tm = _tile(L, (512, 256, 128))
    cn = _tile(N, (512, 256, 128))
    return pl.pallas_call(
        functools.partial(_nm_mm_kernel, cn=cn),
        grid=(B, L // tm),
        in_specs=[pl.BlockSpec((1, tm, D), lambda b, i: (b, i, 0)),
                  pl.BlockSpec((1, 1, D), lambda b, i: (b, 0, 0)),
                  pl.BlockSpec((1, 1, D), lambda b, i: (b, 0, 0)),
                  pl.BlockSpec((D, N), lambda b, i: (0, 0), pipeline_mode=pl.Buffered(1))],
        out_specs=pl.BlockSpec((1, tm, N), lambda b, i: (b, i, 0)),
        out_shape=jax.ShapeDtypeStruct((B, L, N), out_dtype),
        compiler_params=_cparams(("parallel", "parallel")),
        name="norm_mod_matmul",
    )(x, gs, sh, w)


def _odd_in_kernel(x_ref, xp_ref, xn_ref, gs_ref, sh_ref, w_ref, cw_ref, cb_ref, o_ref, *, cn, conv_cols):
    tm = x_ref.shape[1]
    i = pl.program_id(1)
    has_prev = (i > 0).astype(F32)
    has_next = (i < pl.num_programs(1) - 1).astype(F32)
    gs = gs_ref[0]
    sh = sh_ref[0]
    h = _norm_mod(x_ref[0], gs, sh).astype(BF16)
    halo = jnp.concatenate([_norm_mod(xp_ref[0], gs, sh), _norm_mod(xn_ref[0], gs, sh)],
                           axis=0).astype(BF16)
    row = lax.broadcasted_iota(jnp.int32, (tm, cn), 0)
    first = row == 0
    last = row == tm - 1
    hcat = jnp.concatenate([h, halo], axis=0)

    def proj(c0):
        lhs = hcat if c0 < conv_cols else h
        return jnp.dot(lhs, w_ref[:, c0:c0 + cn], preferred_element_type=F32)

    starts = list(range(0, w_ref.shape[1], cn))
    r = proj(starts[0])
    for idx, c0 in enumerate(starts):
        r_next = proj(starts[idx + 1]) if idx + 1 < len(starts) else None
        if c0 < conv_cols:
            p = r[:tm]
            pm1 = jnp.where(first, r[tm + 7:tm + 8, :] * has_prev, pltpu.roll(p, 1, 0))
            pp1 = jnp.where(last, r[tm + 8:tm + 9, :] * has_next, pltpu.roll(p, tm - 1, 0))
            r = (cw_ref[0:1, c0:c0 + cn] * pm1 + cw_ref[1:2, c0:c0 + cn] * p
                 + cw_ref[2:3, c0:c0 + cn] * pp1 + cb_ref[:, c0:c0 + cn])
        o_ref[0, :, c0:c0 + cn] = r
        r = r_next


def odd_in_proj(x, gs, sh, w, cw, cb):
    B, L, D = x.shape
    N = w.shape[1]
    conv_cols = cw.shape[1]
    tm = _tile(L, (512, 256, 128))
    cn = _tile(math.gcd(N, conv_cols), (512, 256, 128))
    nb8 = L // 8
    t8 = tm // 8
    const = lambda b, i: (0, 0)
    return pl.pallas_call(
        functools.partial(_odd_in_kernel, cn=cn, conv_cols=conv_cols),
        grid=(B, L // tm),
        in_specs=[pl.BlockSpec((1, tm, D), lambda b, i: (b, i, 0)),
                  pl.BlockSpec((1, 8, D), lambda b, i: (b, jnp.maximum(i * t8 - 1, 0), 0)),
                  pl.BlockSpec((1, 8, D), lambda b, i: (b, jnp.minimum((i + 1) * t8, nb8 - 1), 0)),
                  pl.BlockSpec((1, 1, D), lambda b, i: (b, 0, 0)),
                  pl.BlockSpec((1, 1, D), lambda b, i: (b, 0, 0)),
                  pl.BlockSpec((D, N), const, pipeline_mode=pl.Buffered(1)),
                  pl.BlockSpec((3, conv_cols), const),
                  pl.BlockSpec((1, conv_cols), const)],
        out_specs=pl.BlockSpec((1, tm, N), lambda b, i: (b, i, 0)),
        out_shape=jax.ShapeDtypeStruct((B, L, N), F32),
        compiler_params=_cparams(("parallel", "arbitrary")),
        name="odd_in_proj",
    )(x, x, x, gs, sh, w, cw, cb.reshape(1, conv_cols))


def _even_in_kernel(x_ref, gs_ref, sh_ref, w_ref, seg_ref, cq_ref, sq_ref, ck_ref, sk_ref,
                    q_ref, klo_ref, khi_ref, vlo_ref, vhi_ref, glu_ref):
    tm = x_ref.shape[1]
    h = _norm_mod(x_ref[0], gs_ref[0], sh_ref[0]).astype(BF16)
    seg = seg_ref[...]
    lane = lax.broadcasted_iota(jnp.int32, (tm, LANES), 1)
    even = (lane & 1) == 0
    low = lane < HEAD_DIM

    def proj(col0, width):
        return jnp.dot(h, w_ref[:, col0:col0 + width], preferred_element_type=F32)

    def head_norm_rope(y, c_ref, s_ref):
        sq = y * y
        hi = sq.astype(BF16)
        lo = (sq - hi.astype(F32)).astype(BF16)
        ss = (jnp.dot(hi, seg, preferred_element_type=F32)
              + jnp.dot(lo, seg, preferred_element_type=F32))
        yn = y * lax.rsqrt(ss * (1.0 / HEAD_DIM) + NORM_EPS)
        out = []
        for j in range(2):
            c = yn[:, LANES * j:LANES * (j + 1)]
            sw = jnp.where(even, pltpu.roll(c, LANES - 1, 1), pltpu.roll(c, 1, 1))
            out.append(c * c_ref[...] + sw * s_ref[...])
        return out

    def put_padded(c, j, lo_ref, hi_ref):
        r = pltpu.roll(c, HEAD_DIM, 1)
        dt = lo_ref.dtype
        lo_ref[0, :, LANES * 2 * j:LANES * (2 * j + 1)] = jnp.where(low, c, 0.0).astype(dt)
        hi_ref[0, :, LANES * 2 * j:LANES * (2 * j + 1)] = jnp.where(low, 0.0, r).astype(dt)
        lo_ref[0, :, LANES * (2 * j + 1):LANES * (2 * j + 2)] = jnp.where(low, r, 0.0).astype(dt)
        hi_ref[0, :, LANES * (2 * j + 1):LANES * (2 * j + 2)] = jnp.where(low, 0.0, c).astype(dt)

    n_q = Q_W // 256
    n_k = KV_W // 256
    g0 = Q_W + 2 * KV_W
    y = proj(0, 256)
    for t in range(n_q + n_k):
        y_next = proj(256 * (t + 1), 256) if t + 1 < n_q + n_k else None
        if t < n_q:
            for j, c in enumerate(head_norm_rope(y, cq_ref, sq_ref)):
                q_ref[0, :, 256 * t + LANES * j:256 * t + LANES * (j + 1)] = c.astype(q_ref.dtype)
        else:
            for j, c in enumerate(head_norm_rope(y, ck_ref, sk_ref)):
                put_padded(c, 2 * (t - n_q) + j, klo_ref, khi_ref)
        y = y_next
    v = proj(Q_W + KV_W, KV_W)
    for j in range(KV_W // LANES):
        put_padded(v[:, LANES * j:LANES * (j + 1)], j, vlo_ref, vhi_ref)
    a = proj(g0, CONF_DIM)
    gate = proj(g0 + CONF_DIM, CONF_DIM)
    glu_ref[0] = a * jax.nn.sigmoid(gate)


def even_in_proj(x, gs, sh, w, seg, cq, sq, ck, sk):
    B, L, D = x.shape
    N = w.shape[1]
    tm = _tile(L, (512, 256, 128))
    row = lambda b, i: (b, i, 0)
    tab = pl.BlockSpec((tm, LANES), lambda b, i: (i, 0))
    KP = N_KV_HEADS * LANES
    kv_spec = pl.BlockSpec((1, tm, KP), row)
    kv_shape = jax.ShapeDtypeStruct((B, L, KP), BF16)
    return pl.pallas_call(
        _even_in_kernel,
        grid=(B, L // tm),
        in_specs=[pl.BlockSpec((1, tm, D), row),
                  pl.BlockSpec((1, 1, D), lambda b, i: (b, 0, 0)),
                  pl.BlockSpec((1, 1, D), lambda b, i: (b, 0, 0)),
                  pl.BlockSpec((D, N), lambda b, i: (0, 0), pipeline_mode=pl.Buffered(1)),
                  pl.BlockSpec((256, 256), lambda b, i: (0, 0)),
                  tab, tab, tab, tab],
        out_specs=[pl.BlockSpec((1, tm, Q_W), row), kv_spec, kv_spec, kv_spec, kv_spec,
                   pl.BlockSpec((1, tm, CONF_DIM), row)],
        out_shape=[jax.ShapeDtypeStruct((B, L, Q_W), BF16), kv_shape, kv_shape, kv_shape, kv_shape,
                   jax.ShapeDtypeStruct((B, L, CONF_DIM), F32)],
        compiler_params=_cparams(("parallel", "parallel")),
        name="even_in_proj",
    )(x, gs, sh, w, seg, cq, sq, ck, sk)


def _attn_kernel(q_ref, *refs, n_seg):
    o_ref = refs[4 * n_seg]
    group = N_Q_HEADS // N_KV_HEADS
    nt = (((1,), (1,)), ((), ()))

    def scores(hq):
        pr, pos = divmod(hq, 2)
        q2 = q_ref[0, :, LANES * pr:LANES * (pr + 1)]
        g = hq // group
        cols = slice(LANES * g, LANES * (g + 1))
        return [lax.dot_general(q2, refs[4 * t + pos][0, :, cols], nt, preferred_element_type=F32)
                for t in range(n_seg)]

    ss = scores(0)
    acc = None
    for hq in range(N_Q_HEADS):
        nxt = scores(hq + 1) if hq + 1 < N_Q_HEADS else None
        pr, pos = divmod(hq, 2)
        g = hq // group
        cols = slice(LANES * g, LANES * (g + 1))
        m = functools.reduce(jnp.maximum, [jnp.max(s, axis=-1, keepdims=True) for s in ss])
        ps = [jnp.exp(s - m) for s in ss]
        l = functools.reduce(jnp.add, [jnp.sum(p, axis=-1, keepdims=True) for p in ps])
        o = functools.reduce(jnp.add, [
            jnp.dot(p.astype(BF16), refs[4 * t + 2 + pos][0, :, cols], preferred_element_type=F32)
            for t, p in enumerate(ps)])
        o = o / l
        if pos == 0:
            acc = o
        else:
            o_ref[0, :, LANES * pr:LANES * (pr + 1)] = (acc + o).astype(o_ref.dtype)
        ss = nxt


def attention(q, segments):
    B, Lq, _ = q.shape
    tq = _tile(Lq, (256, 128))
    KP = N_KV_HEADS * LANES
    specs, args = [], []
    for seg in segments:
        for t in seg:
            specs.append(pl.BlockSpec((1, t.shape[1], KP), lambda b, i: (b, 0, 0)))
            args.append(t)
    return pl.pallas_call(
        functools.partial(_attn_kernel, n_seg=len(segments)),
        grid=(B, Lq // tq),
        in_specs=[pl.BlockSpec((1, tq, Q_W), lambda b, i: (b, i, 0))] + specs,
        out_specs=pl.BlockSpec((1, tq, Q_W), lambda b, i: (b, i, 0)),
        out_shape=jax.ShapeDtypeStruct((B, Lq, Q_W), BF16),
        compiler_params=_cparams(("parallel", "parallel")),
        name="attention",
    )(q, *args)


_CONV_PAD = 16


def _dwconv_kernel(x_ref, w_ref, b_ref, g_ref, beta_ref, o_ref, xp_ref, *, K, TR, ln_swish):
    L = x_ref.shape[1]
    C = x_ref.shape[2]
    half = (K - 1) // 2
    xp_ref[0:_CONV_PAD, :] = jnp.zeros((_CONV_PAD, C), F32)
    xp_ref[_CONV_PAD + L:, :] = jnp.zeros((_CONV_PAD, C), F32)
    xp_ref[_CONV_PAD:_CONV_PAD + L, :] = x_ref[0]

    groups = {}
    for k in range(K):
        q8, r8 = divmod(_CONV_PAD - half + k, 8)
        groups.setdefault(r8, []).append((k, q8))
    R = TR + 8
    CW = 256 if C % 256 == 0 else C

    def tile(t, carry):
        r0 = pl.multiple_of(t * TR, TR)
        parts = []
        for c0 in range(0, C, CW):
            acc = None
            for r8, taps in sorted(groups.items()):
                part = None
                for k, q8 in taps:
                    term = (xp_ref[pl.ds(pl.multiple_of(r0 + 8 * q8, 8), R), c0:c0 + CW]
                            * w_ref[k:k + 1, c0:c0 + CW])
                    part = term if part is None else part + term
                if r8:
                    part = pltpu.roll(part, R - r8, 0)
                part = part[:TR]
                acc = part if acc is None else acc + part
            parts.append(acc + b_ref[:, c0:c0 + CW])
        acc = parts[0] if len(parts) == 1 else jnp.concatenate(parts, axis=-1)
        if ln_swish:
            mu = jnp.mean(acc, axis=-1, keepdims=True)
            d = acc - mu
            var = jnp.mean(d * d, axis=-1, keepdims=True)
            y = d * lax.rsqrt(var + NORM_EPS) * g_ref[...] + beta_ref[...]
            acc = y * jax.nn.sigmoid(y)
        o_ref[0, pl.ds(r0, TR), :] = acc.astype(o_ref.dtype)
        return carry

    lax.fori_loop(0, L // TR, tile, 0)


def dwconv(x, w, b, g=None, beta=None, *, out_dtype=F32):
    B, L, _ = x.shape
    K, C = w.shape
    ln_swish = g is not None
    tc = C if ln_swish else _tile(C, (512, 256, 128))
    TR = _tile(L, (64, 32, 16, 8))
    if g is None:
        g = jnp.ones((C,), F32)
        beta = jnp.zeros((C,), F32)
    vec = pl.BlockSpec((1, tc), lambda bb, j: (0, j))
    return pl.pallas_call(
        functools.partial(_dwconv_kernel, K=K, TR=TR, ln_swish=ln_swish),
        grid=(B, C // tc),
        in_specs=[pl.BlockSpec((1, L, tc), lambda bb, j: (bb, 0, j)),
                  pl.BlockSpec((K, tc), lambda bb, j: (0, j)),
                  vec, vec, vec],
        out_specs=pl.BlockSpec((1, L, tc), lambda bb, j: (bb, 0, j)),
        out_shape=jax.ShapeDtypeStruct((B, L, C), out_dtype),
        scratch_shapes=[pltpu.VMEM((L + 2 * _CONV_PAD, tc), F32)],
        compiler_params=_cparams(("parallel", "parallel")),
        name="dwconv%d" % K,
    )(x, w, b.reshape(1, C), g.reshape(1, C), beta.reshape(1, C))


def _out_proj_kernel(x_ref, a1_ref, a2_ref, w1_ref, w2_ref, g_ref, o_ref):
    y = jnp.dot(a1_ref[0], w1_ref[...], preferred_element_type=F32)
    y = y + jnp.dot(a2_ref[0], w2_ref[...], preferred_element_type=F32)
    o_ref[0] = x_ref[0] + g_ref[0] * y


def out_proj_residual(x, a1, a2, w1, w2, gate):
    B, L, D = x.shape
    K1, K2 = a1.shape[2], a2.shape[2]
    tm = _tile(L, (512, 256, 128))
    row = lambda b, i: (b, i, 0)
    return pl.pallas_call(
        _out_proj_kernel,
        grid=(B, L // tm),
        in_specs=[pl.BlockSpec((1, tm, D), row),
                  pl.BlockSpec((1, tm, K1), row),
                  pl.BlockSpec((1, tm, K2), row),
                  pl.BlockSpec((K1, D), lambda b, i: (0, 0)),
                  pl.BlockSpec((K2, D), lambda b, i: (0, 0)),
                  pl.BlockSpec((1, 1, D), lambda b, i: (b, 0, 0))],
        out_specs=pl.BlockSpec((1, tm, D), row),
        out_shape=jax.ShapeDtypeStruct((B, L, D), F32),
        compiler_params=_cparams(("parallel", "parallel")),
        name="out_proj_residual",
    )(x, a1, a2, w1, w2, gate)


_FFN_CHUNK = 256


def _ffn_kernel(x_ref, xp_ref, xn_ref, gs_ref, sh_ref, wup_ref, cw_ref, cb_ref, wdn_ref, g5_ref,
                fg_ref, o_ref, acc_ref, *, final_norm):
    tm = x_ref.shape[1]
    FF = wdn_ref.shape[0]
    i = pl.program_id(1)
    has_prev = (i > 0).astype(F32)
    has_next = (i < pl.num_programs(1) - 1).astype(F32)
    gs = gs_ref[0]
    sh = sh_ref[0]
    x = x_ref[0]
    h = _norm_mod(x, gs, sh).astype(BF16)
    halo = jnp.concatenate([_norm_mod(xp_ref[0], gs, sh), _norm_mod(xn_ref[0], gs, sh)],
                           axis=0).astype(BF16)
    row = lax.broadcasted_iota(jnp.int32, (tm, _FFN_CHUNK), 0)
    first = row == 0
    last = row == tm - 1

    hcat = jnp.concatenate([h, halo], axis=0)

    def up(c0):
        r = jnp.dot(hcat, wup_ref[:, c0:c0 + _FFN_CHUNK], preferred_element_type=F32)
        return r[:tm], r[tm:]

    def conv3(c0, p, ph):
        prev_row = ph[7:8, :] * has_prev
        next_row = ph[8:9, :] * has_next
        pm1 = jnp.where(first, prev_row, pltpu.roll(p, 1, 0))
        pp1 = jnp.where(last, next_row, pltpu.roll(p, tm - 1, 0))
        return (cw_ref[0:1, c0:c0 + _FFN_CHUNK] * pm1 + cw_ref[1:2, c0:c0 + _FFN_CHUNK] * p
                + cw_ref[2:3, c0:c0 + _FFN_CHUNK] * pp1 + cb_ref[:, c0:c0 + _FFN_CHUNK])

    n_chunks = FF // _FFN_CHUNK
    ahead = 2
    ups = [(up(c * _FFN_CHUNK), up(FF + c * _FFN_CHUNK)) for c in range(min(ahead, n_chunks))]
    for c in range(n_chunks):
        if c + ahead < n_chunks:
            ups.append((up((c + ahead) * _FFN_CHUNK), up(FF + (c + ahead) * _FFN_CHUNK)))
        val = conv3(c * _FFN_CHUNK, *ups[c][0])
        gate = conv3(FF + c * _FFN_CHUNK, *ups[c][1])
        ups[c] = None
        act = (gate * jax.nn.sigmoid(gate) * val).astype(BF16)
        part = jnp.dot(act, wdn_ref[c * _FFN_CHUNK:(c + 1) * _FFN_CHUNK, :],
                       preferred_element_type=F32)
        if c == 0:
            acc_ref[...] = part
        else:
            acc_ref[...] += part
    y = x + g5_ref[0] * acc_ref[...]
    if final_norm:
        y = y * lax.rsqrt(jnp.mean(y * y, axis=-1, keepdims=True) + NORM_EPS) * fg_ref[...]
    o_ref[0] = y


def conv_ffn_residual(x, gs, sh, w_up, cw, cb, w_dn, gate, final_g=None):
    B, L, D = x.shape
    FF = w_dn.shape[0]
    assert FF % _FFN_CHUNK == 0
    tm = _tile(L, (256, 128))
    nb8 = L // 8
    t8 = tm // 8
    final_norm = final_g is not None
    if final_g is None:
        final_g = jnp.ones((D,), F32)
    row = lambda b, i: (b, i, 0)
    const = lambda b, i: (0, 0)
    return pl.pallas_call(
        functools.partial(_ffn_kernel, final_norm=final_norm),
        grid=(B, L // tm),
        in_specs=[pl.BlockSpec((1, tm, D), row),
                  pl.BlockSpec((1, 8, D), lambda b, i: (b, jnp.maximum(i * t8 - 1, 0), 0)),
                  pl.BlockSpec((1, 8, D), lambda b, i: (b, jnp.minimum((i + 1) * t8, nb8 - 1), 0)),
                  pl.BlockSpec((1, 1, D), lambda b, i: (b, 0, 0)),
                  pl.BlockSpec((1, 1, D), lambda b, i: (b, 0, 0)),
                  pl.BlockSpec((D, 2 * FF), const, pipeline_mode=pl.Buffered(1)),
                  pl.BlockSpec((3, 2 * FF), const),
                  pl.BlockSpec((1, 2 * FF), const),
                  pl.BlockSpec((FF, D), const, pipeline_mode=pl.Buffered(1)),
                  pl.BlockSpec((1, 1, D), lambda b, i: (b, 0, 0)),
                  pl.BlockSpec((1, D), const)],
        out_specs=pl.BlockSpec((1, tm, D), row),
        out_shape=jax.ShapeDtypeStruct((B, L, D), F32),
        scratch_shapes=[pltpu.VMEM((tm, D), F32)],
        compiler_params=_cparams(("parallel", "arbitrary")),
        name="conv_ffn",
    )(x, x, x, gs, sh, w_up, cw, cb.reshape(1, 2 * FF), w_dn, gate, final_g.reshape(1, D))


def _hy_filter_kernel(feat_ref, w1_ref, b1_ref, w2_ref, b2_ref, w3_ref, fr_ref, dec_ref,
                      hs_ref, hd_ref, nyq_ref):
    L = feat_ref.shape[0]
    fr = fr_ref[...]
    z = jnp.dot(feat_ref[...].astype(BF16), w1_ref[...].astype(BF16), preferred_element_type=F32)
    hdn = jnp.sin(fr * (z + b1_ref[...]))
    z = jnp.dot(hdn.astype(BF16), w2_ref[...].astype(BF16), preferred_element_type=F32)
    hdn = jnp.sin(fr * (z + b2_ref[...])).astype(BF16)
    row = lax.broadcasted_iota(jnp.int32, (L, HY_DIM), 0)
    not_first = (row > 0).astype(F32)
    sign = jnp.where((row & 1) == 0, 1.0, -1.0).astype(F32)
    dec = dec_ref[...]
    n_order = w3_ref.shape[1] // (2 * HY_DIM)
    for o in range(n_order):
        c0 = o * 2 * HY_DIM
        hf = jnp.dot(hdn, w3_ref[:, c0:c0 + HY_DIM].astype(BF16), preferred_element_type=F32) * dec
        hb = (jnp.dot(hdn, w3_ref[:, c0 + HY_DIM:c0 + 2 * HY_DIM].astype(BF16),
                      preferred_element_type=F32) * dec * not_first)
        inv = 1.0 / (jnp.sum(jnp.abs(hf), axis=0, keepdims=True)
                     + jnp.sum(jnp.abs(hb), axis=0, keepdims=True))
        hs = (hf + hb) * inv
        hd = (hf - hb) * inv
        hs_ref[:, o * HY_DIM:(o + 1) * HY_DIM] = hs
        hd_ref[:, o * HY_DIM:(o + 1) * HY_DIM] = hd
        nyq_ref[:, o * HY_DIM:(o + 1) * HY_DIM] = jnp.sum(hs * sign, axis=0, keepdims=True)


def hyena_filter_taps(L, w1, b1, w2, b2, w3, freq):
    emb = w1.shape[0]
    bands = (emb - 1) // 2
    t = jnp.linspace(0.0, 1.0, L, dtype=F32)[:, None]
    fb = jnp.linspace(1e-4, bands - 1, bands, dtype=F32)
    wpos = (2.0 * math.pi / L) * jnp.arange(L, dtype=F32)[:, None]
    feats = jnp.concatenate([t, jnp.cos(fb * wpos), -jnp.sin(fb * wpos)], axis=-1)
    feats = jnp.pad(feats, ((0, 0), (0, LANES - emb)))
    w1p = jnp.pad(w1, ((0, LANES - emb), (0, 0)))
    deltas = jnp.abs(jnp.linspace(math.log(C_DECAY_TARGET) / C_SLOW_DECAY,
                                  math.log(C_DECAY_TARGET) / C_FAST_DECAY, HY_DIM, dtype=F32))
    dec = jnp.exp(-t * deltas)
    hid = w1.shape[1]
    NO = w3.shape[1] // 2
    full = lambda shape: pl.BlockSpec(shape, lambda: tuple(0 for _ in shape))
    return pl.pallas_call(
        _hy_filter_kernel,
        in_specs=[full((L, LANES)), full((LANES, hid)), full((1, hid)), full((hid, hid)),
                  full((1, hid)), full((hid, 2 * NO)), full((1, hid)), full((L, HY_DIM))],
        out_specs=[full((L, NO)), full((L, NO)), full((1, NO))],
        out_shape=[jax.ShapeDtypeStruct((L, NO), F32), jax.ShapeDtypeStruct((L, NO), F32),
                   jax.ShapeDtypeStruct((1, NO), F32)],
        compiler_params=pltpu.CompilerParams(vmem_limit_bytes=V7X_VMEM_LIMIT),
        name="hyena_filter",
    )(feats, w1p, b1.reshape(1, hid), w2, b2.reshape(1, hid), w3, freq.reshape(1, hid), dec)


_DFT_SPLIT = 64


def _dft_tables(L):
    N = 2 * L
    n = jnp.arange(L, dtype=jnp.int32)

    def cos_sin(mult):
        ang = ((mult[:, None] * n[None, :]) & (N - 1)).astype(F32) * (2.0 * math.pi / N)
        return jnp.cos(ang), jnp.sin(ang)

    ca, sa = cos_sin(jnp.arange(L // _DFT_SPLIT, dtype=jnp.int32) * _DFT_SPLIT)
    cb, sb = cos_sin(jnp.arange(_DFT_SPLIT, dtype=jnp.int32))
    nyq = jnp.where((n & 1) == 0, 1.0, -1.0).astype(F32)
    re = (ca[:, None, :] * cb[None] - sa[:, None, :] * sb[None]).reshape(L, L)
    im = -(sa[:, None, :] * cb[None] + ca[:, None, :] * sb[None]).reshape(L, L)
    im = jnp.where(jnp.arange(L)[:, None] == 0, nyq[None, :], im)
    f = jnp.concatenate([re, im], axis=0).astype(BF16)
    cat, sat, cbt, sbt = ca.T, sa.T, cb.T, sb.T
    re_t = (cat[:, :, None] * cbt[:, None, :] - sat[:, :, None] * sbt[:, None, :]).reshape(L, L)
    im_t = -(sat[:, :, None] * cbt[:, None, :] + cat[:, :, None] * sbt[:, None, :]).reshape(L, L)
    im_t = jnp.where(jnp.arange(L)[None, :] == 0, nyq[:, None], im_t)
    g = jnp.concatenate([re_t, im_t], axis=1).astype(BF16)
    return f, g


def _hy_spec_kernel(fre_ref, fim_ref, hs_ref, hd_ref, nyq_ref, a_ref, b_ref):
    kb = fre_ref.shape[0]
    L = fre_ref.shape[1]
    NO = hs_ref.shape[1]
    krow = lax.broadcasted_iota(jnp.int32, (kb, NO), 0) + pl.program_id(0) * kb
    dc = krow == 0
    scale = jnp.where(dc, 1.0, 2.0).astype(F32) * (1.0 / (2 * L))
    hr = jnp.dot(fre_ref[...], hs_ref[...].astype(BF16), preferred_element_type=F32)
    hi = jnp.dot(fim_ref[...], hd_ref[...].astype(BF16), preferred_element_type=F32)
    a_ref[...] = hr * scale
    b_ref[...] = jnp.where(dc, nyq_ref[...], hi) * scale


def hyena_spectrum(F, hs, hd, nyq):
    L, NO = hs.shape
    kb = _tile(L, (256, 128))
    nk = L // kb
    return pl.pallas_call(
        _hy_spec_kernel,
        grid=(nk,),
        in_specs=[pl.BlockSpec((kb, L), lambda i: (i, 0)),
                  pl.BlockSpec((kb, L), lambda i: (i + nk, 0)),
                  pl.BlockSpec((L, NO), lambda i: (0, 0)),
                  pl.BlockSpec((L, NO), lambda i: (0, 0)),
                  pl.BlockSpec((1, NO), lambda i: (0, 0))],
        out_specs=[pl.BlockSpec((kb, NO), lambda i: (i, 0)),
                   pl.BlockSpec((kb, NO), lambda i: (i, 0))],
        out_shape=[jax.ShapeDtypeStruct((L, NO), F32), jax.ShapeDtypeStruct((L, NO), F32)],
        compiler_params=_cparams(("arbitrary",)),
        name="hyena_spectrum",
    )(F, F, hs, hd, nyq)


def _hy_conv_kernel(z_ref, gate_ref, fre_ref, fim_ref, gre_ref, gim_ref, a_ref, b_ref, skip_ref,
                    o_ref, zb_ref, y_ref):
    kb = fre_ref.shape[0]
    C = z_ref.shape[2]
    kstep = pl.program_id(1)

    @pl.when(kstep == 0)
    def _():
        zb_ref[...] = z_ref[0].astype(BF16)
        y_ref[...] = jnp.zeros(y_ref.shape, F32)

    zb = zb_ref[...]
    NH = 2
    hb = kb // NH
    fwd = []
    for hf in range(NH):
        rows = slice(hf * hb, (hf + 1) * hb)
        fwd.append((jnp.dot(fre_ref[rows, :], zb, preferred_element_type=F32),
                    jnp.dot(fim_ref[rows, :], zb, preferred_element_type=F32)))
    for hf in range(NH):
        rows = slice(hf * hb, (hf + 1) * hb)
        ur, ui = fwd[hf]
        a = a_ref[rows, :]
        b = b_ref[rows, :]
        krow = lax.broadcasted_iota(jnp.int32, (hb, C), 0) + (kstep * kb + hf * hb)
        dc = krow == 0
        vr = jnp.where(dc, ur * a, ur * a - ui * b)
        vi = jnp.where(dc, ui * b, ur * b + ui * a)
        y_ref[...] += (jnp.dot(gre_ref[:, rows], vr.astype(BF16), preferred_element_type=F32)
                       + jnp.dot(gim_ref[:, rows], vi.astype(BF16), preferred_element_type=F32))

    @pl.when(kstep == pl.num_programs(1) - 1)
    def _():
        o_ref[0] = (gate_ref[0] * (y_ref[...] + skip_ref[...] * z_ref[0])).astype(o_ref.dtype)


def hyena_conv(z_src, z_col, gate_src, gate_col, F, G, A, Bt, spec_col, skip, out_dtype):
    B, L, _ = z_src.shape
    C = HY_DIM
    kb = _tile(L, (512, 256))
    nk = L // kb
    return pl.pallas_call(
        _hy_conv_kernel,
        grid=(B, nk),
        in_specs=[pl.BlockSpec((1, L, C), lambda b, k: (b, 0, z_col)),
                  pl.BlockSpec((1, L, C), lambda b, k: (b, 0, gate_col)),
                  pl.BlockSpec((kb, L), lambda b, k: (k, 0)),
                  pl.BlockSpec((kb, L), lambda b, k: (k + nk, 0)),
                  pl.BlockSpec((L, kb), lambda b, k: (0, k)),
                  pl.BlockSpec((L, kb), lambda b, k: (0, k + nk)),
                  pl.BlockSpec((kb, C), lambda b, k: (k, spec_col)),
                  pl.BlockSpec((kb, C), lambda b, k: (k, spec_col)),
                  pl.BlockSpec((1, C), lambda b, k: (0, 0))],
        out_specs=pl.BlockSpec((1, L, C), lambda b, k: (b, 0, 0)),
        out_shape=jax.ShapeDtypeStruct((B, L, C), out_dtype),
        scratch_shapes=[pltpu.VMEM((L, C), BF16), pltpu.VMEM((L, C), F32)],
        compiler_params=_cparams(("parallel", "arbitrary")),
        name="hyena_conv",
    )(z_src, gate_src, F, F, G, G, A, Bt, skip.reshape(1, C))


HG_GROUP = 128


def _hgrn2_kernel(ql_ref, ffl_ref, fbl_ref, il_ref, gl_ref, qc_ref, ffc_ref, fbc_ref, ic_ref,
                  lb_ref, gn_ref, o_ref, vb_ref, qdf_ref, qdb_ref, qgf_ref, qgb_ref,
                  kuf_ref, kub_ref, kgf_ref, kgb_ref, kdf_ref, kdb_ref, decf_ref, decb_ref,
                  oif_ref, oib_ref, stf_ref, stb_ref):
    C = HG_CHUNK
    GR = HG_GROUP
    G2 = 2 * C
    L = ql_ref.shape[1]
    Lc = qc_ref.shape[1]
    r = lax.broadcasted_iota(jnp.int32, (GR, GR), 0)
    c = lax.broadcasted_iota(jnp.int32, (GR, GR), 1)
    same = (r // C) == (c // C)
    same_group = (r // G2) == (c // G2)
    row_odd = ((lax.broadcasted_iota(jnp.int32, (GR, HG_DIM), 0) // C) & 1) == 1
    dirs = []
    for d, (incl, excl) in enumerate(((c <= r, c > r), (c >= r, c < r))):
        mask = same & incl
        cross = same_group & jnp.logical_not(same) & incl
        second = row_odd if d == 0 else jnp.logical_not(row_odd)
        dirs.append((lb_ref[d:d + 1, :], mask.astype(BF16), (same & excl).astype(BF16), mask, cross,
                     second))
    f_refs_lat = (ffl_ref, fbl_ref)
    f_refs_ctx = (ffc_ref, fbc_ref)
    qd_refs = (qdf_ref, qdb_ref)
    qg_refs = (qgf_ref, qgb_ref)
    ku_refs = (kuf_ref, kub_ref)
    kg_refs = (kgf_ref, kgb_ref)
    kd_refs = (kdf_ref, kdb_ref)
    dec_refs = (decf_ref, decb_ref)
    oi_refs = (oif_ref, oib_ref)
    st_refs = (stf_ref, stb_ref)
    nt = (((1,), (1,)), ((), ()))

    def gates(q_ref, f_refs, i_ref, n_rows, row_off, with_out):
        def body(gi, carry):
            r0 = pl.multiple_of(gi * GR, GR)
            ro = pl.multiple_of(row_off + gi * GR, GR)
            q = q_ref[0, pl.ds(r0, GR), :]
            vb_ref[pl.ds(ro, GR), :] = i_ref[0, pl.ds(r0, GR), :].astype(BF16)
            for d, (lbd, t_in, t_ex, _, _, second) in enumerate(dirs):
                f = lbd + (1.0 - lbd) * jax.nn.sigmoid(f_refs[d][0, pl.ds(r0, GR), :])
                k = 1.0 - f
                lg = jnp.log(f)
                hi = lg.astype(BF16)
                lo = (lg - hi.astype(F32)).astype(BF16)
                b = (jnp.dot(t_in, hi, preferred_element_type=F32)
                     + jnp.dot(t_in, lo, preferred_element_type=F32))
                rest = (jnp.dot(t_ex, hi, preferred_element_type=F32)
                        + jnp.dot(t_ex, lo, preferred_element_type=F32))
                tot = b + rest
                other = jnp.where(row_odd, pltpu.roll(tot, C, 0), pltpu.roll(tot, GR - C, 0))
                qd_refs[d][pl.ds(ro, GR), :] = (q * jnp.exp(b)).astype(BF16)
                qg_refs[d][pl.ds(ro, GR), :] = (
                    q * jnp.exp(b + jnp.where(second, other, 0.0))).astype(BF16)
                ku_refs[d][pl.ds(ro, GR), :] = (k * jnp.exp(rest)).astype(BF16)
                kg_refs[d][pl.ds(ro, GR), :] = (
                    k * jnp.exp(rest + jnp.where(second, 0.0, other))).astype(BF16)
                dec_refs[d][pl.ds(ro, GR), :] = tot + other
                if with_out:
                    kd_refs[d][pl.ds(r0, GR), :] = (k * jnp.exp(-b)).astype(BF16)
            return carry

        lax.fori_loop(0, n_rows // GR, body, 0, unroll=2)

    def intra():
        U = 4 if (L // GR) % 4 == 0 else 1

        def body(gi, carry):
            jobs = []
            for u in range(U):
                r0 = pl.multiple_of((gi * U + u) * GR, GR)
                ro = pl.multiple_of(Lc + (gi * U + u) * GR, GR)
                for d in range(2):
                    qd = qd_refs[d][pl.ds(ro, GR), :]
                    sc = lax.dot_general(qd, kd_refs[d][pl.ds(r0, GR), :], nt,
                                         preferred_element_type=F32)
                    sx = lax.dot_general(qd, ku_refs[d][pl.ds(ro, GR), :], nt,
                                         preferred_element_type=F32)
                    jobs.append((d, r0, ro, sc, sx))
            for d, r0, ro, sc, sx in jobs:
                sc = jnp.where(dirs[d][3], sc, jnp.where(dirs[d][4], sx, 0.0)).astype(BF16)
                oi_refs[d][pl.ds(r0, GR), :] = jnp.dot(sc, vb_ref[pl.ds(ro, GR), :],
                                                       preferred_element_type=F32)
            return carry

        lax.fori_loop(0, L // (GR * U), body, 0)

    def recurrence(n_rows, row_off, keep, carry):
        n = n_rows // G2

        def body(j, states):
            new = []
            for d, st in enumerate(states):
                jj = j if d == 0 else n - 1 - j
                ro = pl.multiple_of(row_off + jj * G2, G2)
                if keep:
                    st_refs[d][pl.ds(pl.multiple_of(jj * HG_DIM, HG_DIM), HG_DIM), :] = st.astype(BF16)
                upd = lax.dot_general(vb_ref[pl.ds(ro, G2), :], kg_refs[d][pl.ds(ro, G2), :],
                                      (((0,), (0,)), ((), ())), preferred_element_type=F32)
                new.append(st * jnp.exp(dec_refs[d][pl.ds(ro, 1), :]) + upd)
            return tuple(new)

        return lax.fori_loop(0, n, body, carry, unroll=8)

    def inter():
        def body(j, carry):
            r0 = pl.multiple_of(j * G2, G2)
            ro = pl.multiple_of(Lc + j * G2, G2)
            so = pl.multiple_of(j * HG_DIM, HG_DIM)
            for d in range(2):
                o = lax.dot_general(qg_refs[d][pl.ds(ro, G2), :], st_refs[d][pl.ds(so, HG_DIM), :], nt,
                                    preferred_element_type=F32)
                oi_refs[d][pl.ds(r0, G2), :] += o
            return carry

        lax.fori_loop(0, L // G2, body, 0, unroll=8)

    gates(qc_ref, f_refs_ctx, ic_ref, Lc, 0, False)
    gates(ql_ref, f_refs_lat, il_ref, L, Lc, True)
    intra()
    zero = jnp.zeros((HG_DIM, HG_DIM), F32)
    carry = recurrence(Lc, 0, False, (zero, zero))
    recurrence(L, Lc, True, carry)
    inter()
    o = oif_ref[...] + oib_ref[...]
    o = o * lax.rsqrt(jnp.mean(o * o, axis=-1, keepdims=True) + NORM_EPS) * gn_ref[...]
    g = gl_ref[0]
    o_ref[0] = (o * (g * jax.nn.sigmoid(g))).astype(o_ref.dtype)


def hgrn2_mixer(p_lat, lat_col0, p_ctx, ctx_col0, lb, gn_g):
    B, L, _ = p_lat.shape
    Lc = p_ctx.shape[1]
    lc0 = lat_col0 // HG_DIM
    cc0 = ctx_col0 // HG_DIM

    def lat(seg):
        return pl.BlockSpec((1, L, HG_DIM), lambda b, h: (b, 0, lc0 + HG_HEADS * seg + h))

    def ctx(seg):
        return pl.BlockSpec((1, Lc, HG_DIM), lambda b, h: (b, 0, cc0 + HG_HEADS * seg + h))

    return pl.pallas_call(
        _hgrn2_kernel,
        grid=(B, HG_HEADS),
        in_specs=[lat(0), lat(1), lat(2), lat(3), lat(4), ctx(0), ctx(1), ctx(2), ctx(3),
                  pl.BlockSpec((2, HG_DIM), lambda b, h: (0, h)),
                  pl.BlockSpec((1, HG_DIM), lambda b, h: (0, 0))],
        out_specs=pl.BlockSpec((1, L, HG_DIM), lambda b, h: (b, 0, h)),
        out_shape=jax.ShapeDtypeStruct((B, L, HG_W), BF16),
        scratch_shapes=([pltpu.VMEM((Lc + L, HG_DIM), BF16)] * 9
                        + [pltpu.VMEM((L, HG_DIM), BF16)] * 2
                        + [pltpu.VMEM((Lc + L, HG_DIM), F32)] * 2
                        + [pltpu.VMEM((L, HG_DIM), F32)] * 2
                        + [pltpu.VMEM((L // (2 * HG_CHUNK) * HG_DIM, HG_DIM), BF16)] * 2),
        compiler_params=_cparams(("parallel", "parallel")),
        name="hgrn2",
    )(p_lat, p_lat, p_lat, p_lat, p_lat, p_ctx, p_ctx, p_ctx, p_ctx, lb, gn_g.reshape(1, HG_DIM))


def _rope_tables(L, gain, scale, rotate):
    g2 = jnp.tile(gain.astype(F32), LANES // HEAD_DIM)[None, :] * scale
    gsw = g2.reshape(1, LANES // 2, 2)[:, :, ::-1].reshape(1, LANES)
    if not rotate:
        return jnp.broadcast_to(g2, (L, LANES)), jnp.zeros((L, LANES), F32)
    rows = L // GRID_W
    row = jnp.repeat(jnp.arange(rows, dtype=F32), GRID_W)
    col = jnp.tile(jnp.arange(GRID_W, dtype=F32), rows)
    axis_dim = HEAD_DIM // 2
    inv_freq = ROPE_THETA ** (-jnp.arange(0, axis_dim, 2, dtype=F32) / axis_dim)
    ang = jnp.concatenate([row[:, None] * inv_freq, col[:, None] * inv_freq], axis=-1)
    cos = jnp.tile(jnp.repeat(jnp.cos(ang), 2, axis=-1), (1, LANES // HEAD_DIM))
    sin = jnp.tile(jnp.repeat(jnp.sin(ang), 2, axis=-1), (1, LANES // HEAD_DIM))
    sign = jnp.tile(jnp.array([-1.0, 1.0], F32), LANES // 2)[None, :]
    return cos * g2, sin * sign * gsw


def _split6(m):
    return jnp.split(m, 6, axis=-1)


def kernel(x, c, ctx, c_ctx, ada_w, ada_b, norm1_g, norm2_g, final_g, ev_w_in, ev_q_gain, ev_k_gain,
           ev_conv_w, ev_conv_b, ev_ln_g, ev_ln_b, ev_w_out, od_w_in, od_short_w, od_short_b,
           od_filt_w1, od_filt_b1, od_filt_w2, od_filt_b2, od_filt_w3, od_filt_freq, od_hyena_skip,
           od_lower_bound, od_gnorm_g, od_w_out, ffn_w_up, ffn_conv_w, ffn_conv_b, ffn_w_down):
    B, L, D = x.shape
    Lc = ctx.shape[1]
    depth = ada_w.shape[0]
    assert depth == 2, "layer 0 (attention + conformer) followed by layer 1 (hyena + hgrn2)"

    lb_soft = jax.nn.softmax(od_lower_bound.astype(F32), axis=0)
    lower_bounds = jnp.cumsum(lb_soft, axis=0) - lb_soft[0:1]

    cond = jnp.concatenate([c, c_ctx[None, :], jnp.zeros((7, D), F32)], axis=0)

    def mods(l):
        m = ada_modulation(cond, ada_w[l], ada_b[l])
        ml = [t[:, None, :] for t in _split6(m[:B])]
        mc = [jnp.broadcast_to(t[None], (B, 1, D)) for t in _split6(m[B:B + 1])]
        return ml, mc

    def ffn(xx, mm, l, final=None):
        return conv_ffn_residual(xx, norm2_g[l] * (1.0 + mm[4]), mm[3], w_up[l],
                                 ffn_conv_w[l], ffn_conv_b[l], w_down[l], mm[5], final)

    w_up = [layer_weight_bf16(ffn_w_up, l) for l in range(depth)]
    w_down = [layer_weight_bf16(ffn_w_down, l) for l in range(depth)]

    ml, mc = mods(0)
    w_in = layer_weight_bf16(ev_w_in, 0)
    seg = jnp.kron(jnp.eye(256 // HEAD_DIM, dtype=F32), jnp.ones((HEAD_DIM, HEAD_DIM), F32)).astype(BF16)
    scale = HEAD_DIM ** -0.5
    cq_l, sq_l = _rope_tables(L, ev_q_gain[0], scale, True)
    ck_l, sk_l = _rope_tables(L, ev_k_gain[0], 1.0, True)
    cq_c, sq_c = _rope_tables(Lc, ev_q_gain[0], scale, False)
    ck_c, sk_c = _rope_tables(Lc, ev_k_gain[0], 1.0, False)
    q_l, *kv_l, glu_l = even_in_proj(x, norm1_g[0] * (1.0 + ml[1]), ml[0], w_in, seg,
                                     cq_l, sq_l, ck_l, sk_l)
    q_c, *kv_c, glu_c = even_in_proj(ctx, norm1_g[0] * (1.0 + mc[1]), mc[0], w_in, seg,
                                     cq_c, sq_c, ck_c, sk_c)
    attn_l = attention(q_l, [kv_c, kv_l])
    attn_c = attention(q_c, [kv_c])
    conv_l = dwconv(glu_l, ev_conv_w[0], ev_conv_b[0], ev_ln_g[0], ev_ln_b[0], out_dtype=BF16)
    conv_c = dwconv(glu_c, ev_conv_w[0], ev_conv_b[0], ev_ln_g[0], ev_ln_b[0], out_dtype=BF16)
    wo = layer_weight_bf16(ev_w_out, 0)
    wo1 = wo[:Q_W]
    wo2 = wo[Q_W:]
    x = out_proj_residual(x, attn_l, conv_l, wo1, wo2, ml[2])
    ctx = out_proj_residual(ctx, attn_c, conv_c, wo1, wo2, mc[2])
    x = ffn(x, ml, 0)
    ctx = ffn(ctx, mc, 0)

    ml, mc = mods(1)
    w_in = layer_weight_bf16(od_w_in, 0)
    p_lat = odd_in_proj(x, norm1_g[1] * (1.0 + ml[1]), ml[0], w_in, od_short_w[0], od_short_b[0])
    p_ctx = norm_mod_matmul(ctx, norm1_g[1] * (1.0 + mc[1]), mc[0], w_in[:, HY_IN:])
    d_lat = hgrn2_mixer(p_lat, HY_IN, p_ctx, 0, lower_bounds[1], od_gnorm_g[0])

    hs, hd, nyq = hyena_filter_taps(L, od_filt_w1[0], od_filt_b1[0], od_filt_w2[0], od_filt_b2[0],
                                    od_filt_w3[0], od_filt_freq[0])
    F, G = _dft_tables(L)
    A, Bt = hyena_spectrum(F, hs, hd, nyq)
    z1 = hyena_conv(p_lat, 0, p_lat, 1, F, G, A, Bt, 0, od_hyena_skip[0][0], F32)
    c_lat = hyena_conv(z1, 0, p_lat, 2, F, G, A, Bt, 1, od_hyena_skip[0][1], BF16)

    wo = layer_weight_bf16(od_w_out, 0)
    x = out_proj_residual(x, c_lat, d_lat, wo[:HY_DIM], wo[HY_DIM:], ml[2])
    return ffn(x, ml, 1, final_g)
```

```python
import functools
import math

import jax
import jax.numpy as jnp
from jax import lax
from jax.experimental import pallas as pl
from jax.experimental.pallas import tpu as pltpu

F32 = jnp.float32
BF16 = jnp.bfloat16

NORM_EPS = 1e-6
GRID_W = 64
ROPE_THETA = 10000.0
HEAD_DIM = 64
N_Q_HEADS = 12
N_KV_HEADS = 4
Q_W = N_Q_HEADS * HEAD_DIM
KV_W = N_KV_HEADS * HEAD_DIM
CONF_DIM = 512
HY_DIM = 512
HY_IN = 3 * HY_DIM
HG_HEADS = 4
HG_DIM = 128
HG_W = HG_HEADS * HG_DIM
HG_CHUNK = 32
C_FAST_DECAY = 0.3
C_SLOW_DECAY = 1.5
C_DECAY_TARGET = 1e-2
V7X_VMEM_LIMIT = 56 * 1024 * 1024
LANES = 128


def _cparams(sem):
    return pltpu.CompilerParams(dimension_semantics=sem, vmem_limit_bytes=V7X_VMEM_LIMIT)


def _tile(n, prefs):
    for p in prefs:
        if n % p == 0:
            return p
    return n


def _norm_mod(x, gs, sh):
    r = lax.rsqrt(jnp.mean(x * x, axis=-1, keepdims=True) + NORM_EPS)
    return x * r * gs + sh


def _cast_kernel(w_ref, o_ref):
    o_ref[...] = w_ref[0].astype(o_ref.dtype)


def layer_weight_bf16(w, layer, col0=0):
    _, R, C = w.shape
    tr = _tile(R, (256, 128, 64, 32, 16))
    tc = math.gcd(C, col0) if col0 else C
    return pl.pallas_call(
        _cast_kernel,
        grid=(R // tr, (C - col0) // tc),
        in_specs=[pl.BlockSpec((1, tr, tc), lambda i, j: (layer, i, j + col0 // tc))],
        out_specs=pl.BlockSpec((tr, tc), lambda i, j: (i, j)),
        out_shape=jax.ShapeDtypeStruct((R, C - col0), BF16),
        compiler_params=_cparams(("parallel", "parallel")),
        name="layer_weight_bf16",
    )(w)


def _ada_kernel(c_ref, w_ref, b_ref, o_ref):
    c = c_ref[...]
    s = (c * jax.nn.sigmoid(c)).astype(BF16)
    o_ref[...] = jnp.dot(s, w_ref[...].astype(BF16), preferred_element_type=F32) + b_ref[...]


def ada_modulation(cs, w, b):
    R, D = cs.shape
    N = w.shape[1]
    tn = _tile(N, (1024, 512, 256, 128))
    return pl.pallas_call(
        _ada_kernel,
        grid=(N // tn,),
        in_specs=[pl.BlockSpec((R, D), lambda j: (0, 0)),
                  pl.BlockSpec((D, tn), lambda j: (0, j)),
                  pl.BlockSpec((1, tn), lambda j: (0, j))],
        out_specs=pl.BlockSpec((R, tn), lambda j: (0, j)),
        out_shape=jax.ShapeDtypeStruct((R, N), F32),
        compiler_params=_cparams(("arbitrary",)),
        name="ada_modulation",
    )(cs, w, b.reshape(1, N))


def _nm_mm_kernel(x_ref, gs_ref, sh_ref, w_ref, o_ref, *, cn):
    h = _norm_mod(x_ref[0], gs_ref[0], sh_ref[0]).astype(BF16)
    for c0 in range(0, w_ref.shape[1], cn):
        o_ref[0, :, c0:c0 + cn] = jnp.dot(h, w_ref[:, c0:c0 + cn],
                                          preferred_element_type=F32).astype(o_ref.dtype)


def norm_mod_matmul(x, gs, sh, w, out_dtype=F32):
    B, L, D = x.shape
    N = w.shape[1]
    tm = _tile(L, (512, 256, 128))
    cn = _tile(N, (512, 256, 128))
    return pl.pallas_call(
        functools.partial(_nm_mm_kernel, cn=cn),
        grid=(B, L // tm),
        in_specs=[pl.BlockSpec((1, tm, D), lambda b, i: (b, i, 0)),
                  pl.BlockSpec((1, 1, D), lambda b, i: (b, 0, 0)),
                  pl.BlockSpec((1, 1, D), lambda b, i: (b, 0, 0)),
                  pl.BlockSpec((D, N), lambda b, i: (0, 0), pipeline_mode=pl.Buffered(1))],
        out_specs=pl.BlockSpec((1, tm, N), lambda b, i: (b, i, 0)),
        out_shape=jax.ShapeDtypeStruct((B, L, N), out_dtype),
        compiler_params=_cparams(("parallel", "parallel")),
        name="norm_mod_matmul",
    )(x, gs, sh, w)


def _odd_in_kernel(x_ref, xp_ref, xn_ref, gs_ref, sh_ref, w_ref, cw_ref, cb_ref, o_ref, *, cn, conv_cols):
    tm = x_ref.shape[1]
    i = pl.program_id(1)
    has_prev = (i > 0).astype(F32)
    has_next = (i < pl.num_programs(1) - 1).astype(F32)
    gs = gs_ref[0]
    sh = sh_ref[0]
    h = _norm_mod(x_ref[0], gs, sh).astype(BF16)
    halo = jnp.concatenate([_norm_mod(xp_ref[0], gs, sh), _norm_mod(xn_ref[0], gs, sh)],
                           axis=0).astype(BF16)
    row = lax.broadcasted_iota(jnp.int32, (tm, cn), 0)
    first = row == 0
    last = row == tm - 1
    hcat = jnp.concatenate([h, halo], axis=0)

    def proj(c0):
        lhs = hcat if c0 < conv_cols else h
        return jnp.dot(lhs, w_ref[:, c0:c0 + cn], preferred_element_type=F32)

    starts = list(range(0, w_ref.shape[1], cn))
    r = proj(starts[0])
    for idx, c0 in enumerate(starts):
        r_next = proj(starts[idx + 1]) if idx + 1 < len(starts) else None
        if c0 < conv_cols:
            p = r[:tm]
            pm1 = jnp.where(first, r[tm + 7:tm + 8, :] * has_prev, pltpu.roll(p, 1, 0))
            pp1 = jnp.where(last, r[tm + 8:tm + 9, :] * has_next, pltpu.roll(p, tm - 1, 0))
            r = (cw_ref[0:1, c0:c0 + cn] * pm1 + cw_ref[1:2, c0:c0 + cn] * p
                 + cw_ref[2:3, c0:c0 + cn] * pp1 + cb_ref[:, c0:c0 + cn])
        o_ref[0, :, c0:c0 + cn] = r
        r = r_next


def odd_in_proj(x, gs, sh, w, cw, cb):
    B, L, D = x.shape
    N = w.shape[1]
    conv_cols = cw.shape[1]
    tm = _tile(L, (512, 256, 128))
    cn = _tile(math.gcd(N, conv_cols), (512, 256, 128))
    nb8 = L // 8
    t8 = tm // 8
    const = lambda b, i: (0, 0)
    return pl.pallas_call(
        functools.partial(_odd_in_kernel, cn=cn, conv_cols=conv_cols),
        grid=(B, L // tm),
        in_specs=[pl.BlockSpec((1, tm, D), lambda b, i: (b, i, 0)),
                  pl.BlockSpec((1, 8, D), lambda b, i: (b, jnp.maximum(i * t8 - 1, 0), 0)),
                  pl.BlockSpec((1, 8, D), lambda b, i: (b, jnp.minimum((i + 1) * t8, nb8 - 1), 0)),
                  pl.BlockSpec((1, 1, D), lambda b, i: (b, 0, 0)),
                  pl.BlockSpec((1, 1, D), lambda b, i: (b, 0, 0)),
                  pl.BlockSpec((D, N), const, pipeline_mode=pl.Buffered(1)),
                  pl.BlockSpec((3, conv_cols), const),
                  pl.BlockSpec((1, conv_cols), const)],
        out_specs=pl.BlockSpec((1, tm, N), lambda b, i: (b, i, 0)),
        out_shape=jax.ShapeDtypeStruct((B, L, N), F32),
        compiler_params=_cparams(("parallel", "arbitrary")),
        name="odd_in_proj",
    )(x, x, x, gs, sh, w, cw, cb.reshape(1, conv_cols))


def _even_in_kernel(x_ref, gs_ref, sh_ref, w_ref, seg_ref, cq_ref, sq_ref, ck_ref, sk_ref,
                    q_ref, klo_ref, khi_ref, vlo_ref, vhi_ref, glu_ref):
    tm = x_ref.shape[1]
    h = _norm_mod(x_ref[0], gs_ref[0], sh_ref[0]).astype(BF16)
    seg = seg_ref[...]
    lane = lax.broadcasted_iota(jnp.int32, (tm, LANES), 1)
    even = (lane & 1) == 0
    low = lane < HEAD_DIM

    def proj(col0, width):
        return jnp.dot(h, w_ref[:, col0:col0 + width], preferred_element_type=F32)

    def head_norm_rope(y, c_ref, s_ref):
        sq = y * y
        hi = sq.astype(BF16)
        lo = (sq - hi.astype(F32)).astype(BF16)
        ss = (jnp.dot(hi, seg, preferred_element_type=F32)
              + jnp.dot(lo, seg, preferred_element_type=F32))
        yn = y * lax.rsqrt(ss * (1.0 / HEAD_DIM) + NORM_EPS)
        out = []
        for j in range(2):
            c = yn[:, LANES * j:LANES * (j + 1)]
            sw = jnp.where(even, pltpu.roll(c, LANES - 1, 1), pltpu.roll(c, 1, 1))
            out.append(c * c_ref[...] + sw * s_ref[...])
        return out

    def put_padded(c, j, lo_ref, hi_ref):
        r = pltpu.roll(c, HEAD_DIM, 1)
        dt = lo_ref.dtype
        lo_ref[0, :, LANES * 2 * j:LANES * (2 * j + 1)] = jnp.where(low, c, 0.0).astype(dt)
        hi_ref[0, :, LANES * 2 * j:LANES * (2 * j + 1)] = jnp.where(low, 0.0, r).astype(dt)
        lo_ref[0, :, LANES * (2 * j + 1):LANES * (2 * j + 2)] = jnp.where(low, r, 0.0).astype(dt)
        hi_ref[0, :, LANES * (2 * j + 1):LANES * (2 * j + 2)] = jnp.where(low, 0.0, c).astype(dt)

    n_q = Q_W // 256
    n_k = KV_W // 256
    g0 = Q_W + 2 * KV_W
    y = proj(0, 256)
    for t in range(n_q + n_k):
        y_next = proj(256 * (t + 1), 256) if t + 1 < n_q + n_k else None
        if t < n_q:
            for j, c in enumerate(head_norm_rope(y, cq_ref, sq_ref)):
                q_ref[0, :, 256 * t + LANES * j:256 * t + LANES * (j + 1)] = c.astype(q_ref.dtype)
        else:
            for j, c in enumerate(head_norm_rope(y, ck_ref, sk_ref)):
                put_padded(c, 2 * (t - n_q) + j, klo_ref, khi_ref)
        y = y_next
    v = proj(Q_W + KV_W, KV_W)
    for j in range(KV_W // LANES):
        put_padded(v[:, LANES * j:LANES * (j + 1)], j, vlo_ref, vhi_ref)
    a = proj(g0, CONF_DIM)
    gate = proj(g0 + CONF_DIM, CONF_DIM)
    glu_ref[0] = a * jax.nn.sigmoid(gate)


def even_in_proj(x, gs, sh, w, seg, cq, sq, ck, sk):
    B, L, D = x.shape
    N = w.shape[1]
    tm = _tile(L, (512, 256, 128))
    row = lambda b, i: (b, i, 0)
    tab = pl.BlockSpec((tm, LANES), lambda b, i: (i, 0))
    KP = N_KV_HEADS * LANES
    kv_spec = pl.BlockSpec((1, tm, KP), row)
    kv_shape = jax.ShapeDtypeStruct((B, L, KP), BF16)
    return pl.pallas_call(
        _even_in_kernel,
        grid=(B, L // tm),
        in_specs=[pl.BlockSpec((1, tm, D), row),
                  pl.BlockSpec((1, 1, D), lambda b, i: (b, 0, 0)),
                  pl.BlockSpec((1, 1, D), lambda b, i: (b, 0, 0)),
                  pl.BlockSpec((D, N), lambda b, i: (0, 0), pipeline_mode=pl.Buffered(1)),
                  pl.BlockSpec((256, 256), lambda b, i: (0, 0)),
                  tab, tab, tab, tab],
        out_specs=[pl.BlockSpec((1, tm, Q_W), row), kv_spec, kv_spec, kv_spec, kv_spec,
                   pl.BlockSpec((1, tm, CONF_DIM), row)],
        out_shape=[jax.ShapeDtypeStruct((B, L, Q_W), BF16), kv_shape, kv_shape, kv_shape, kv_shape,
                   jax.ShapeDtypeStruct((B, L, CONF_DIM), F32)],
        compiler_params=_cparams(("parallel", "parallel")),
        name="even_in_proj",
    )(x, gs, sh, w, seg, cq, sq, ck, sk)


def _attn_kernel(q_ref, *refs, n_seg):
    o_ref = refs[4 * n_seg]
    group = N_Q_HEADS // N_KV_HEADS
    nt = (((1,), (1,)), ((), ()))

    def scores(hq):
        pr, pos = divmod(hq, 2)
        q2 = q_ref[0, :, LANES * pr:LANES * (pr + 1)]
        g = hq // group
        cols = slice(LANES * g, LANES * (g + 1))
        return [lax.dot_general(q2, refs[4 * t + pos][0, :, cols], nt, preferred_element_type=F32)
                for t in range(n_seg)]

    ss = scores(0)
    acc = None
    for hq in range(N_Q_HEADS):
        nxt = scores(hq + 1) if hq + 1 < N_Q_HEADS else None
        pr, pos = divmod(hq, 2)
        g = hq // group
        cols = slice(LANES * g, LANES * (g + 1))
        m = functools.reduce(jnp.maximum, [jnp.max(s, axis=-1, keepdims=True) for s in ss])
        ps = [jnp.exp(s - m) for s in ss]
        l = functools.reduce(jnp.add, [jnp.sum(p, axis=-1, keepdims=True) for p in ps])
        o = functools.reduce(jnp.add, [
            jnp.dot(p.astype(BF16), refs[4 * t + 2 + pos][0, :, cols], preferred_element_type=F32)
            for t, p in enumerate(ps)])
        o = o / l
        if pos == 0:
            acc = o
        else:
            o_ref[0, :, LANES * pr:LANES * (pr + 1)] = (acc + o).astype(o_ref.dtype)
        ss = nxt


def attention(q, segments):
    B, Lq, _ = q.shape
    tq = _tile(Lq, (256, 128))
    KP = N_KV_HEADS * LANES
    specs, args = [], []
    for seg in segments:
        for t in seg:
            specs.append(pl.BlockSpec((1, t.shape[1], KP), lambda b, i: (b, 0, 0)))
            args.append(t)
    return pl.pallas_call(
        functools.partial(_attn_kernel, n_seg=len(segments)),
        grid=(B, Lq // tq),
        in_specs=[pl.BlockSpec((1, tq, Q_W), lambda b, i: (b, i, 0))] + specs,
        out_specs=pl.BlockSpec((1, tq, Q_W), lambda b, i: (b, i, 0)),
        out_shape=jax.ShapeDtypeStruct((B, Lq, Q_W), BF16),
        compiler_params=_cparams(("parallel", "parallel")),
        name="attention",
    )(q, *args)


_CONV_PAD = 16


def _dwconv_kernel(x_ref, w_ref, b_ref, g_ref, beta_ref, o_ref, xp_ref, *, K, TR, ln_swish):
    L = x_ref.shape[1]
    C = x_ref.shape[2]
    half = (K - 1) // 2
    xp_ref[0:_CONV_PAD, :] = jnp.zeros((_CONV_PAD, C), F32)
    xp_ref[_CONV_PAD + L:, :] = jnp.zeros((_CONV_PAD, C), F32)
    xp_ref[_CONV_PAD:_CONV_PAD + L, :] = x_ref[0]

    groups = {}
    for k in range(K):
        q8, r8 = divmod(_CONV_PAD - half + k, 8)
        groups.setdefault(r8, []).append((k, q8))
    R = TR + 8
    CW = LANES if C % LANES == 0 else C

    def tile(t, carry):
        r0 = pl.multiple_of(t * TR, TR)
        parts = []
        for c0 in range(0, C, CW):
            acc = None
            for r8, taps in sorted(groups.items()):
                part = None
                for k, q8 in taps:
                    term = (xp_ref[pl.ds(pl.multiple_of(r0 + 8 * q8, 8), R), c0:c0 + CW]
                            * w_ref[k:k + 1, c0:c0 + CW])
                    part = term if part is None else part + term
                if r8:
                    part = pltpu.roll(part, R - r8, 0)
                part = part[:TR]
                acc = part if acc is None else acc + part
            parts.append(acc + b_ref[:, c0:c0 + CW])
        acc = parts[0] if len(parts) == 1 else jnp.concatenate(parts, axis=-1)
        if ln_swish:
            mu = jnp.mean(acc, axis=-1, keepdims=True)
            d = acc - mu
            var = jnp.mean(d * d, axis=-1, keepdims=True)
            y = d * lax.rsqrt(var + NORM_EPS) * g_ref[...] + beta_ref[...]
            acc = y * jax.nn.sigmoid(y)
        o_ref[0, pl.ds(r0, TR), :] = acc.astype(o_ref.dtype)
        return carry

    lax.fori_loop(0, L // TR, tile, 0)


def dwconv(x, w, b, g=None, beta=None, *, out_dtype=F32):
    B, L, _ = x.shape
    K, C = w.shape
    ln_swish = g is not None
    tc = C if ln_swish else _tile(C, (512, 256, 128))
    TR = _tile(L, (256, 128, 64, 32, 16, 8))
    if g is None:
        g = jnp.ones((C,), F32)
        beta = jnp.zeros((C,), F32)
    vec = pl.BlockSpec((1, tc), lambda bb, j: (0, j))
    return pl.pallas_call(
        functools.partial(_dwconv_kernel, K=K, TR=TR, ln_swish=ln_swish),
        grid=(B, C // tc),
        in_specs=[pl.BlockSpec((1, L, tc), lambda bb, j: (bb, 0, j)),
                  pl.BlockSpec((K, tc), lambda bb, j: (0, j)),
                  vec, vec, vec],
        out_specs=pl.BlockSpec((1, L, tc), lambda bb, j: (bb, 0, j)),
        out_shape=jax.ShapeDtypeStruct((B, L, C), out_dtype),
        scratch_shapes=[pltpu.VMEM((L + 2 * _CONV_PAD, tc), F32)],
        compiler_params=_cparams(("parallel", "parallel")),
        name="dwconv%d" % K,
    )(x, w, b.reshape(1, C), g.reshape(1, C), beta.reshape(1, C))


def _out_proj_kernel(x_ref, a1_ref, a2_ref, w1_ref, w2_ref, g_ref, o_ref):
    y = jnp.dot(a1_ref[0], w1_ref[...], preferred_element_type=F32)
    y = y + jnp.dot(a2_ref[0], w2_ref[...], preferred_element_type=F32)
    o_ref[0] = x_ref[0] + g_ref[0] * y


def out_proj_residual(x, a1, a2, w1, w2, gate):
    B, L, D = x.shape
    K1, K2 = a1.shape[2], a2.shape[2]
    tm = _tile(L, (512, 256, 128))
    row = lambda b, i: (b, i, 0)
    return pl.pallas_call(
        _out_proj_kernel,
        grid=(B, L // tm),
        in_specs=[pl.BlockSpec((1, tm, D), row),
                  pl.BlockSpec((1, tm, K1), row),
                  pl.BlockSpec((1, tm, K2), row),
                  pl.BlockSpec((K1, D), lambda b, i: (0, 0)),
                  pl.BlockSpec((K2, D), lambda b, i: (0, 0)),
                  pl.BlockSpec((1, 1, D), lambda b, i: (b, 0, 0))],
        out_specs=pl.BlockSpec((1, tm, D), row),
        out_shape=jax.ShapeDtypeStruct((B, L, D), F32),
        compiler_params=_cparams(("parallel", "parallel")),
        name="out_proj_residual",
    )(x, a1, a2, w1, w2, gate)


_FFN_CHUNK = 256


def _ffn_kernel(x_ref, xp_ref, xn_ref, gs_ref, sh_ref, wup_ref, cw_ref, cb_ref, wdn_ref, g5_ref,
                fg_ref, o_ref, acc_ref, *, final_norm):
    tm = x_ref.shape[1]
    FF = wdn_ref.shape[0]
    i = pl.program_id(1)
    has_prev = (i > 0).astype(F32)
    has_next = (i < pl.num_programs(1) - 1).astype(F32)
    gs = gs_ref[0]
    sh = sh_ref[0]
    x = x_ref[0]
    h = _norm_mod(x, gs, sh).astype(BF16)
    halo = jnp.concatenate([_norm_mod(xp_ref[0], gs, sh), _norm_mod(xn_ref[0], gs, sh)],
                           axis=0).astype(BF16)
    row = lax.broadcasted_iota(jnp.int32, (tm, _FFN_CHUNK), 0)
    first = row == 0
    last = row == tm - 1

    hcat = jnp.concatenate([h, halo], axis=0)

    def up(c0):
        r = jnp.dot(hcat, wup_ref[:, c0:c0 + _FFN_CHUNK], preferred_element_type=F32)
        return r[:tm], r[tm:]

    def conv3(c0, p, ph):
        prev_row = ph[7:8, :] * has_prev
        next_row = ph[8:9, :] * has_next
        pm1 = jnp.where(first, prev_row, pltpu.roll(p, 1, 0))
        pp1 = jnp.where(last, next_row, pltpu.roll(p, tm - 1, 0))
        return (cw_ref[0:1, c0:c0 + _FFN_CHUNK] * pm1 + cw_ref[1:2, c0:c0 + _FFN_CHUNK] * p
                + cw_ref[2:3, c0:c0 + _FFN_CHUNK] * pp1 + cb_ref[:, c0:c0 + _FFN_CHUNK])

    n_chunks = FF // _FFN_CHUNK
    ahead = 2
    ups = [(up(c * _FFN_CHUNK), up(FF + c * _FFN_CHUNK)) for c in range(min(ahead, n_chunks))]
    for c in range(n_chunks):
        if c + ahead < n_chunks:
            ups.append((up((c + ahead) * _FFN_CHUNK), up(FF + (c + ahead) * _FFN_CHUNK)))
        val = conv3(c * _FFN_CHUNK, *ups[c][0])
        gate = conv3(FF + c * _FFN_CHUNK, *ups[c][1])
        ups[c] = None
        act = (gate * jax.nn.sigmoid(gate) * val).astype(BF16)
        part = jnp.dot(act, wdn_ref[c * _FFN_CHUNK:(c + 1) * _FFN_CHUNK, :],
                       preferred_element_type=F32)
        if c == 0:
            acc_ref[...] = part
        else:
            acc_ref[...] += part
    y = x + g5_ref[0] * acc_ref[...]
    if final_norm:
        y = y * lax.rsqrt(jnp.mean(y * y, axis=-1, keepdims=True) + NORM_EPS) * fg_ref[...]
    o_ref[0] = y


def conv_ffn_residual(x, gs, sh, w_up, cw, cb, w_dn, gate, final_g=None):
    B, L, D = x.shape
    FF = w_dn.shape[0]
    assert FF % _FFN_CHUNK == 0
    tm = _tile(L, (256, 128))
    nb8 = L // 8
    t8 = tm // 8
    final_norm = final_g is not None
    if final_g is None:
        final_g = jnp.ones((D,), F32)
    row = lambda b, i: (b, i, 0)
    const = lambda b, i: (0, 0)
    return pl.pallas_call(
        functools.partial(_ffn_kernel, final_norm=final_norm),
        grid=(B, L // tm),
        in_specs=[pl.BlockSpec((1, tm, D), row),
                  pl.BlockSpec((1, 8, D), lambda b, i: (b, jnp.maximum(i * t8 - 1, 0), 0)),
                  pl.BlockSpec((1, 8, D), lambda b, i: (b, jnp.minimum((i + 1) * t8, nb8 - 1), 0)),
                  pl.BlockSpec((1, 1, D), lambda b, i: (b, 0, 0)),
                  pl.BlockSpec((1, 1, D), lambda b, i: (b, 0, 0)),
                  pl.BlockSpec((D, 2 * FF), const, pipeline_mode=pl.Buffered(1)),
                  pl.BlockSpec((3, 2 * FF), const),
                  pl.BlockSpec((1, 2 * FF), const),
                  pl.BlockSpec((FF, D), const, pipeline_mode=pl.Buffered(1)),
                  pl.BlockSpec((1, 1, D), lambda b, i: (b, 0, 0)),
                  pl.BlockSpec((1, D), const)],
        out_specs=pl.BlockSpec((1, tm, D), row),
        out_shape=jax.ShapeDtypeStruct((B, L, D), F32),
        scratch_shapes=[pltpu.VMEM((tm, D), F32)],
        compiler_params=_cparams(("parallel", "arbitrary")),
        name="conv_ffn",
    )(x, x, x, gs, sh, w_up, cw, cb.reshape(1, 2 * FF), w_dn, gate, final_g.reshape(1, D))


def _hy_filter_kernel(feat_ref, w1_ref, b1_ref, w2_ref, b2_ref, w3_ref, fr_ref, dec_ref,
                      hs_ref, hd_ref, nyq_ref):
    L = feat_ref.shape[0]
    fr = fr_ref[...]
    z = jnp.dot(feat_ref[...].astype(BF16), w1_ref[...].astype(BF16), preferred_element_type=F32)
    hdn = jnp.sin(fr * (z + b1_ref[...]))
    z = jnp.dot(hdn.astype(BF16), w2_ref[...].astype(BF16), preferred_element_type=F32)
    hdn = jnp.sin(fr * (z + b2_ref[...])).astype(BF16)
    row = lax.broadcasted_iota(jnp.int32, (L, HY_DIM), 0)
    not_first = (row > 0).astype(F32)
    sign = jnp.where((row & 1) == 0, 1.0, -1.0).astype(F32)
    dec = dec_ref[...]
    n_order = w3_ref.shape[1] // (2 * HY_DIM)
    for o in range(n_order):
        c0 = o * 2 * HY_DIM
        hf = jnp.dot(hdn, w3_ref[:, c0:c0 + HY_DIM].astype(BF16), preferred_element_type=F32) * dec
        hb = (jnp.dot(hdn, w3_ref[:, c0 + HY_DIM:c0 + 2 * HY_DIM].astype(BF16),
                      preferred_element_type=F32) * dec * not_first)
        inv = 1.0 / (jnp.sum(jnp.abs(hf), axis=0, keepdims=True)
                     + jnp.sum(jnp.abs(hb), axis=0, keepdims=True))
        hs = (hf + hb) * inv
        hd = (hf - hb) * inv
        hs_ref[:, o * HY_DIM:(o + 1) * HY_DIM] = hs
        hd_ref[:, o * HY_DIM:(o + 1) * HY_DIM] = hd
        nyq_ref[:, o * HY_DIM:(o + 1) * HY_DIM] = jnp.sum(hs * sign, axis=0, keepdims=True)


def hyena_filter_taps(L, w1, b1, w2, b2, w3, freq):
    emb = w1.shape[0]
    bands = (emb - 1) // 2
    t = jnp.linspace(0.0, 1.0, L, dtype=F32)[:, None]
    fb = jnp.linspace(1e-4, bands - 1, bands, dtype=F32)
    wpos = (2.0 * math.pi / L) * jnp.arange(L, dtype=F32)[:, None]
    feats = jnp.concatenate([t, jnp.cos(fb * wpos), -jnp.sin(fb * wpos)], axis=-1)
    feats = jnp.pad(feats, ((0, 0), (0, LANES - emb)))
    w1p = jnp.pad(w1, ((0, LANES - emb), (0, 0)))
    deltas = jnp.abs(jnp.linspace(math.log(C_DECAY_TARGET) / C_SLOW_DECAY,
                                  math.log(C_DECAY_TARGET) / C_FAST_DECAY, HY_DIM, dtype=F32))
    dec = jnp.exp(-t * deltas)
    hid = w1.shape[1]
    NO = w3.shape[1] // 2
    full = lambda shape: pl.BlockSpec(shape, lambda: tuple(0 for _ in shape))
    return pl.pallas_call(
        _hy_filter_kernel,
        in_specs=[full((L, LANES)), full((LANES, hid)), full((1, hid)), full((hid, hid)),
                  full((1, hid)), full((hid, 2 * NO)), full((1, hid)), full((L, HY_DIM))],
        out_specs=[full((L, NO)), full((L, NO)), full((1, NO))],
        out_shape=[jax.ShapeDtypeStruct((L, NO), F32), jax.ShapeDtypeStruct((L, NO), F32),
                   jax.ShapeDtypeStruct((1, NO), F32)],
        compiler_params=pltpu.CompilerParams(vmem_limit_bytes=V7X_VMEM_LIMIT),
        name="hyena_filter",
    )(feats, w1p, b1.reshape(1, hid), w2, b2.reshape(1, hid), w3, freq.reshape(1, hid), dec)


_DFT_SPLIT = 64


def _dft_tables(L):
    N = 2 * L
    n = jnp.arange(L, dtype=jnp.int32)

    def cos_sin(mult):
        ang = ((mult[:, None] * n[None, :]) & (N - 1)).astype(F32) * (2.0 * math.pi / N)
        return jnp.cos(ang), jnp.sin(ang)

    ca, sa = cos_sin(jnp.arange(L // _DFT_SPLIT, dtype=jnp.int32) * _DFT_SPLIT)
    cb, sb = cos_sin(jnp.arange(_DFT_SPLIT, dtype=jnp.int32))
    nyq = jnp.where((n & 1) == 0, 1.0, -1.0).astype(F32)
    re = (ca[:, None, :] * cb[None] - sa[:, None, :] * sb[None]).reshape(L, L)
    im = -(sa[:, None, :] * cb[None] + ca[:, None, :] * sb[None]).reshape(L, L)
    im = jnp.where(jnp.arange(L)[:, None] == 0, nyq[None, :], im)
    f = jnp.concatenate([re, im], axis=0).astype(BF16)
    cat, sat, cbt, sbt = ca.T, sa.T, cb.T, sb.T
    re_t = (cat[:, :, None] * cbt[:, None, :] - sat[:, :, None] * sbt[:, None, :]).reshape(L, L)
    im_t = -(sat[:, :, None] * cbt[:, None, :] + cat[:, :, None] * sbt[:, None, :]).reshape(L, L)
    im_t = jnp.where(jnp.arange(L)[None, :] == 0, nyq[:, None], im_t)
    g = jnp.concatenate([re_t, im_t], axis=1).astype(BF16)
    return f, g


def _hy_spec_kernel(fre_ref, fim_ref, hs_ref, hd_ref, nyq_ref, a_ref, b_ref):
    kb = fre_ref.shape[0]
    L = fre_ref.shape[1]
    NO = hs_ref.shape[1]
    krow = lax.broadcasted_iota(jnp.int32, (kb, NO), 0) + pl.program_id(0) * kb
    dc = krow == 0
    scale = jnp.where(dc, 1.0, 2.0).astype(F32) * (1.0 / (2 * L))
    hr = jnp.dot(fre_ref[...], hs_ref[...].astype(BF16), preferred_element_type=F32)
    hi = jnp.dot(fim_ref[...], hd_ref[...].astype(BF16), preferred_element_type=F32)
    a_ref[...] = hr * scale
    b_ref[...] = jnp.where(dc, nyq_ref[...], hi) * scale


def hyena_spectrum(F, hs, hd, nyq):
    L, NO = hs.shape
    kb = _tile(L, (256, 128))
    nk = L // kb
    return pl.pallas_call(
        _hy_spec_kernel,
        grid=(nk,),
        in_specs=[pl.BlockSpec((kb, L), lambda i: (i, 0)),
                  pl.BlockSpec((kb, L), lambda i: (i + nk, 0)),
                  pl.BlockSpec((L, NO), lambda i: (0, 0)),
                  pl.BlockSpec((L, NO), lambda i: (0, 0)),
                  pl.BlockSpec((1, NO), lambda i: (0, 0))],
        out_specs=[pl.BlockSpec((kb, NO), lambda i: (i, 0)),
                   pl.BlockSpec((kb, NO), lambda i: (i, 0))],
        out_shape=[jax.ShapeDtypeStruct((L, NO), F32), jax.ShapeDtypeStruct((L, NO), F32)],
        compiler_params=_cparams(("arbitrary",)),
        name="hyena_spectrum",
    )(F, F, hs, hd, nyq)


def _hy_conv_kernel(z_ref, gate_ref, fre_ref, fim_ref, gre_ref, gim_ref, a_ref, b_ref, skip_ref,
                    o_ref, zb_ref, y_ref):
    kb = fre_ref.shape[0]
    C = z_ref.shape[2]
    kstep = pl.program_id(1)

    @pl.when(kstep == 0)
    def _():
        zb_ref[...] = z_ref[0].astype(BF16)
        y_ref[...] = jnp.zeros(y_ref.shape, F32)

    zb = zb_ref[...]
    NH = 2
    hb = kb // NH
    fwd = []
    for hf in range(NH):
        rows = slice(hf * hb, (hf + 1) * hb)
        fwd.append((jnp.dot(fre_ref[rows, :], zb, preferred_element_type=F32),
                    jnp.dot(fim_ref[rows, :], zb, preferred_element_type=F32)))
    for hf in range(NH):
        rows = slice(hf * hb, (hf + 1) * hb)
        ur, ui = fwd[hf]
        a = a_ref[rows, :]
        b = b_ref[rows, :]
        krow = lax.broadcasted_iota(jnp.int32, (hb, C), 0) + (kstep * kb + hf * hb)
        dc = krow == 0
        vr = jnp.where(dc, ur * a, ur * a - ui * b)
        vi = jnp.where(dc, ui * b, ur * b + ui * a)
        y_ref[...] += (jnp.dot(gre_ref[:, rows], vr.astype(BF16), preferred_element_type=F32)
                       + jnp.dot(gim_ref[:, rows], vi.astype(BF16), preferred_element_type=F32))

    @pl.when(kstep == pl.num_programs(1) - 1)
    def _():
        o_ref[0] = (gate_ref[0] * (y_ref[...] + skip_ref[...] * z_ref[0])).astype(o_ref.dtype)


def hyena_conv(z_src, z_col, gate_src, gate_col, F, G, A, Bt, spec_col, skip, out_dtype):
    B, L, _ = z_src.shape
    C = HY_DIM
    kb = _tile(L, (512, 256))
    nk = L // kb
    return pl.pallas_call(
        _hy_conv_kernel,
        grid=(B, nk),
        in_specs=[pl.BlockSpec((1, L, C), lambda b, k: (b, 0, z_col)),
                  pl.BlockSpec((1, L, C), lambda b, k: (b, 0, gate_col)),
                  pl.BlockSpec((kb, L), lambda b, k: (k, 0)),
                  pl.BlockSpec((kb, L), lambda b, k: (k + nk, 0)),
                  pl.BlockSpec((L, kb), lambda b, k: (0, k)),
                  pl.BlockSpec((L, kb), lambda b, k: (0, k + nk)),
                  pl.BlockSpec((kb, C), lambda b, k: (k, spec_col)),
                  pl.BlockSpec((kb, C), lambda b, k: (k, spec_col)),
                  pl.BlockSpec((1, C), lambda b, k: (0, 0))],
        out_specs=pl.BlockSpec((1, L, C), lambda b, k: (b, 0, 0)),
        out_shape=jax.ShapeDtypeStruct((B, L, C), out_dtype),
        scratch_shapes=[pltpu.VMEM((L, C), BF16), pltpu.VMEM((L, C), F32)],
        compiler_params=_cparams(("parallel", "arbitrary")),
        name="hyena_conv",
    )(z_src, gate_src, F, F, G, G, A, Bt, skip.reshape(1, C))


HG_GROUP = 128


def _hgrn2_kernel(ql_ref, ffl_ref, fbl_ref, il_ref, gl_ref, qc_ref, ffc_ref, fbc_ref, ic_ref,
                  lb_ref, gn_ref, o_ref, vb_ref, qdf_ref, qdb_ref, qgf_ref, qgb_ref,
                  kuf_ref, kub_ref, kgf_ref, kgb_ref, kdf_ref, kdb_ref, decf_ref, decb_ref,
                  oif_ref, oib_ref, stf_ref, stb_ref):
    C = HG_CHUNK
    GR = HG_GROUP
    G2 = 2 * C
    L = ql_ref.shape[1]
    Lc = qc_ref.shape[1]
    r = lax.broadcasted_iota(jnp.int32, (GR, GR), 0)
    c = lax.broadcasted_iota(jnp.int32, (GR, GR), 1)
    same = (r // C) == (c // C)
    same_group = (r // G2) == (c // G2)
    row_odd = ((lax.broadcasted_iota(jnp.int32, (GR, HG_DIM), 0) // C) & 1) == 1
    dirs = []
    for d, (incl, excl) in enumerate(((c <= r, c > r), (c >= r, c < r))):
        mask = same & incl
        cross = same_group & jnp.logical_not(same) & incl
        second = row_odd if d == 0 else jnp.logical_not(row_odd)
        dirs.append((lb_ref[d:d + 1, :], mask.astype(BF16), (same & excl).astype(BF16), mask, cross,
                     second))
    f_refs_lat = (ffl_ref, fbl_ref)
    f_refs_ctx = (ffc_ref, fbc_ref)
    qd_refs = (qdf_ref, qdb_ref)
    qg_refs = (qgf_ref, qgb_ref)
    ku_refs = (kuf_ref, kub_ref)
    kg_refs = (kgf_ref, kgb_ref)
    kd_refs = (kdf_ref, kdb_ref)
    dec_refs = (decf_ref, decb_ref)
    oi_refs = (oif_ref, oib_ref)
    st_refs = (stf_ref, stb_ref)
    nt = (((1,), (1,)), ((), ()))

    def gates(q_ref, f_refs, i_ref, n_rows, row_off, with_out):
        def body(gi, carry):
            r0 = pl.multiple_of(gi * GR, GR)
            ro = pl.multiple_of(row_off + gi * GR, GR)
            q = q_ref[0, pl.ds(r0, GR), :]
            vb_ref[pl.ds(ro, GR), :] = i_ref[0, pl.ds(r0, GR), :].astype(BF16)
            for d, (lbd, t_in, t_ex, _, _, second) in enumerate(dirs):
                f = lbd + (1.0 - lbd) * jax.nn.sigmoid(f_refs[d][0, pl.ds(r0, GR), :])
                k = 1.0 - f
                lg = jnp.log(f)
                hi = lg.astype(BF16)
                lo = (lg - hi.astype(F32)).astype(BF16)
                b = (jnp.dot(t_in, hi, preferred_element_type=F32)
                     + jnp.dot(t_in, lo, preferred_element_type=F32))
                rest = (jnp.dot(t_ex, hi, preferred_element_type=F32)
                        + jnp.dot(t_ex, lo, preferred_element_type=F32))
                tot = b + rest
                other = jnp.where(row_odd, pltpu.roll(tot, C, 0), pltpu.roll(tot, GR - C, 0))
                qd_refs[d][pl.ds(ro, GR), :] = (q * jnp.exp(b)).astype(BF16)
                qg_refs[d][pl.ds(ro, GR), :] = (
                    q * jnp.exp(b + jnp.where(second, other, 0.0))).astype(BF16)
                ku_refs[d][pl.ds(ro, GR), :] = (k * jnp.exp(rest)).astype(BF16)
                kg_refs[d][pl.ds(ro, GR), :] = (
                    k * jnp.exp(rest + jnp.where(second, 0.0, other))).astype(BF16)
                dec_refs[d][pl.ds(ro, GR), :] = tot + other
                if with_out:
                    kd_refs[d][pl.ds(r0, GR), :] = (k * jnp.exp(-b)).astype(BF16)
            return carry

        lax.fori_loop(0, n_rows // GR, body, 0, unroll=2)

    def intra():
        U = 4 if (L // GR) % 4 == 0 else 1

        def body(gi, carry):
            jobs = []
            for u in range(U):
                r0 = pl.multiple_of((gi * U + u) * GR, GR)
                ro = pl.multiple_of(Lc + (gi * U + u) * GR, GR)
                for d in range(2):
                    qd = qd_refs[d][pl.ds(ro, GR), :]
                    sc = lax.dot_general(qd, kd_refs[d][pl.ds(r0, GR), :], nt,
                                         preferred_element_type=F32)
                    sx = lax.dot_general(qd, ku_refs[d][pl.ds(ro, GR), :], nt,
                                         preferred_element_type=F32)
                    jobs.append((d, r0, ro, sc, sx))
            for d, r0, ro, sc, sx in jobs:
                sc = jnp.where(dirs[d][3], sc, jnp.where(dirs[d][4], sx, 0.0)).astype(BF16)
                oi_refs[d][pl.ds(r0, GR), :] = jnp.dot(sc, vb_ref[pl.ds(ro, GR), :],
                                                       preferred_element_type=F32)
            return carry

        lax.fori_loop(0, L // (GR * U), body, 0)

    def recurrence(n_rows, row_off, keep, carry):
        n = n_rows // G2

        def body(j, states):
            new = []
            for d, st in enumerate(states):
                jj = j if d == 0 else n - 1 - j
                ro = pl.multiple_of(row_off + jj * G2, G2)
                if keep:
                    st_refs[d][pl.ds(pl.multiple_of(jj * HG_DIM, HG_DIM), HG_DIM), :] = st.astype(BF16)
                upd = lax.dot_general(vb_ref[pl.ds(ro, G2), :], kg_refs[d][pl.ds(ro, G2), :],
                                      (((0,), (0,)), ((), ())), preferred_element_type=F32)
                new.append(st * jnp.exp(dec_refs[d][pl.ds(ro, 1), :]) + upd)
            return tuple(new)

        return lax.fori_loop(0, n, body, carry, unroll=8)

    def inter():
        def body(j, carry):
            r0 = pl.multiple_of(j * G2, G2)
            ro = pl.multiple_of(Lc + j * G2, G2)
            so = pl.multiple_of(j * HG_DIM, HG_DIM)
            for d in range(2):
                o = lax.dot_general(qg_refs[d][pl.ds(ro, G2), :], st_refs[d][pl.ds(so, HG_DIM), :], nt,
                                    preferred_element_type=F32)
                oi_refs[d][pl.ds(r0, G2), :] += o
            return carry

        lax.fori_loop(0, L // G2, body, 0, unroll=8)

    gates(qc_ref, f_refs_ctx, ic_ref, Lc, 0, False)
    gates(ql_ref, f_refs_lat, il_ref, L, Lc, True)
    intra()
    zero = jnp.zeros((HG_DIM, HG_DIM), F32)
    carry = recurrence(Lc, 0, False, (zero, zero))
    recurrence(L, Lc, True, carry)
    inter()
    o = oif_ref[...] + oib_ref[...]
    o = o * lax.rsqrt(jnp.mean(o * o, axis=-1, keepdims=True) + NORM_EPS) * gn_ref[...]
    g = gl_ref[0]
    o_ref[0] = (o * (g * jax.nn.sigmoid(g))).astype(o_ref.dtype)


def hgrn2_mixer(p_lat, lat_col0, p_ctx, ctx_col0, lb, gn_g):
    B, L, _ = p_lat.shape
    Lc = p_ctx.shape[1]
    lc0 = lat_col0 // HG_DIM
    cc0 = ctx_col0 // HG_DIM

    def lat(seg):
        return pl.BlockSpec((1, L, HG_DIM), lambda b, h: (b, 0, lc0 + HG_HEADS * seg + h))

    def ctx(seg):
        return pl.BlockSpec((1, Lc, HG_DIM), lambda b, h: (b, 0, cc0 + HG_HEADS * seg + h))

    return pl.pallas_call(
        _hgrn2_kernel,
        grid=(B, HG_HEADS),
        in_specs=[lat(0), lat(1), lat(2), lat(3), lat(4), ctx(0), ctx(1), ctx(2), ctx(3),
                  pl.BlockSpec((2, HG_DIM), lambda b, h: (0, h)),
                  pl.BlockSpec((1, HG_DIM), lambda b, h: (0, 0))],
        out_specs=pl.BlockSpec((1, L, HG_DIM), lambda b, h: (b, 0, h)),
        out_shape=jax.ShapeDtypeStruct((B, L, HG_W), BF16),
        scratch_shapes=([pltpu.VMEM((Lc + L, HG_DIM), BF16)] * 9
                        + [pltpu.VMEM((L, HG_DIM), BF16)] * 2
                        + [pltpu.VMEM((Lc + L, HG_DIM), F32)] * 2
                        + [pltpu.VMEM((L, HG_DIM), F32)] * 2
                        + [pltpu.VMEM((L // (2 * HG_CHUNK) * HG_DIM, HG_DIM), BF16)] * 2),
        compiler_params=_cparams(("parallel", "parallel")),
        name="hgrn2",
    )(p_lat, p_lat, p_lat, p_lat, p_lat, p_ctx, p_ctx, p_ctx, p_ctx, lb, gn_g.reshape(1, HG_DIM))


def _rope_tables(L, gain, scale, rotate):
    g2 = jnp.tile(gain.astype(F32), LANES // HEAD_DIM)[None, :] * scale
    gsw = g2.reshape(1, LANES // 2, 2)[:, :, ::-1].reshape(1, LANES)
    if not rotate:
        return jnp.broadcast_to(g2, (L, LANES)), jnp.zeros((L, LANES), F32)
    rows = L // GRID_W
    row = jnp.repeat(jnp.arange(rows, dtype=F32), GRID_W)
    col = jnp.tile(jnp.arange(GRID_W, dtype=F32), rows)
    axis_dim = HEAD_DIM // 2
    inv_freq = ROPE_THETA ** (-jnp.arange(0, axis_dim, 2, dtype=F32) / axis_dim)
    ang = jnp.concatenate([row[:, None] * inv_freq, col[:, None] * inv_freq], axis=-1)
    cos = jnp.tile(jnp.repeat(jnp.cos(ang), 2, axis=-1), (1, LANES // HEAD_DIM))
    sin = jnp.tile(jnp.repeat(jnp.sin(ang), 2, axis=-1), (1, LANES // HEAD_DIM))
    sign = jnp.tile(jnp.array([-1.0, 1.0], F32), LANES // 2)[None, :]
    return cos * g2, sin * sign * gsw


def _split6(m):
    return jnp.split(m, 6, axis=-1)


def kernel(x, c, ctx, c_ctx, ada_w, ada_b, norm1_g, norm2_g, final_g, ev_w_in, ev_q_gain, ev_k_gain,
           ev_conv_w, ev_conv_b, ev_ln_g, ev_ln_b, ev_w_out, od_w_in, od_short_w, od_short_b,
           od_filt_w1, od_filt_b1, od_filt_w2, od_filt_b2, od_filt_w3, od_filt_freq, od_hyena_skip,
           od_lower_bound, od_gnorm_g, od_w_out, ffn_w_up, ffn_conv_w, ffn_conv_b, ffn_w_down):
    B, L, D = x.shape
    Lc = ctx.shape[1]
    depth = ada_w.shape[0]
    assert depth == 2, "layer 0 (attention + conformer) followed by layer 1 (hyena + hgrn2)"

    lb_soft = jax.nn.softmax(od_lower_bound.astype(F32), axis=0)
    lower_bounds = jnp.cumsum(lb_soft, axis=0) - lb_soft[0:1]

    cond = jnp.concatenate([c, c_ctx[None, :], jnp.zeros((7, D), F32)], axis=0)

    def mods(l):
        m = ada_modulation(cond, ada_w[l], ada_b[l])
        ml = [t[:, None, :] for t in _split6(m[:B])]
        mc = [jnp.broadcast_to(t[None], (B, 1, D)) for t in _split6(m[B:B + 1])]
        return ml, mc

    def ffn(xx, mm, l, final=None):
        return conv_ffn_residual(xx, norm2_g[l] * (1.0 + mm[4]), mm[3], w_up[l],
                                 ffn_conv_w[l], ffn_conv_b[l], w_down[l], mm[5], final)

    w_up = [layer_weight_bf16(ffn_w_up, l) for l in range(depth)]
    w_down = [layer_weight_bf16(ffn_w_down, l) for l in range(depth)]

    ml, mc = mods(0)
    w_in = layer_weight_bf16(ev_w_in, 0)
    seg = jnp.kron(jnp.eye(256 // HEAD_DIM, dtype=F32), jnp.ones((HEAD_DIM, HEAD_DIM), F32)).astype(BF16)
    scale = HEAD_DIM ** -0.5
    cq_l, sq_l = _rope_tables(L, ev_q_gain[0], scale, True)
    ck_l, sk_l = _rope_tables(L, ev_k_gain[0], 1.0, True)
    cq_c, sq_c = _rope_tables(Lc, ev_q_gain[0], scale, False)
    ck_c, sk_c = _rope_tables(Lc, ev_k_gain[0], 1.0, False)
    q_l, *kv_l, glu_l = even_in_proj(x, norm1_g[0] * (1.0 + ml[1]), ml[0], w_in, seg,
                                     cq_l, sq_l, ck_l, sk_l)
    q_c, *kv_c, glu_c = even_in_proj(ctx, norm1_g[0] * (1.0 + mc[1]), mc[0], w_in, seg,
                                     cq_c, sq_c, ck_c, sk_c)
    attn_l = attention(q_l, [kv_c, kv_l])
    attn_c = attention(q_c, [kv_c])
    conv_l = dwconv(glu_l, ev_conv_w[0], ev_conv_b[0], ev_ln_g[0], ev_ln_b[0], out_dtype=BF16)
    conv_c = dwconv(glu_c, ev_conv_w[0], ev_conv_b[0], ev_ln_g[0], ev_ln_b[0], out_dtype=BF16)
    wo = layer_weight_bf16(ev_w_out, 0)
    wo1 = wo[:Q_W]
    wo2 = wo[Q_W:]
    x = out_proj_residual(x, attn_l, conv_l, wo1, wo2, ml[2])
    ctx = out_proj_residual(ctx, attn_c, conv_c, wo1, wo2, mc[2])
    x = ffn(x, ml, 0)
    ctx = ffn(ctx, mc, 0)

    ml, mc = mods(1)
    w_in = layer_weight_bf16(od_w_in, 0)
    p_lat = odd_in_proj(x, norm1_g[1] * (1.0 + ml[1]), ml[0], w_in, od_short_w[0], od_short_b[0])
    p_ctx = norm_mod_matmul(ctx, norm1_g[1] * (1.0 + mc[1]), mc[0],
                            layer_weight_bf16(od_w_in, 0, HY_IN))
    d_lat = hgrn2_mixer(p_lat, HY_IN, p_ctx, 0, lower_bounds[1], od_gnorm_g[0])

    hs, hd, nyq = hyena_filter_taps(L, od_filt_w1[0], od_filt_b1[0], od_filt_w2[0], od_filt_b2[0],
                                    od_filt_w3[0], od_filt_freq[0])
    F, G = _dft_tables(L)
    A, Bt = hyena_spectrum(F, hs, hd, nyq)
    z1 = hyena_conv(p_lat, 0, p_lat, 1, F, G, A, Bt, 0, od_hyena_skip[0][0], F32)
    c_lat = hyena_conv(z1, 0, p_lat, 2, F, G, A, Bt, 1, od_hyena_skip[0][1], BF16)

    wo = layer_weight_bf16(od_w_out, 0)
    x = out_proj_residual(x, c_lat, d_lat, wo[:HY_DIM], wo[HY_DIM:], ml[2])
    return ffn(x, ml, 1, final_g)
```

```python
import functools
import math

import jax
import jax.numpy as jnp
from jax import lax
from jax.experimental import pallas as pl
from jax.experimental.pallas import tpu as pltpu

F32 = jnp.float32
BF16 = jnp.bfloat16

NORM_EPS = 1e-6
GRID_W = 64
ROPE_THETA = 10000.0
HEAD_DIM = 64
N_Q_HEADS = 12
N_KV_HEADS = 4
Q_W = N_Q_HEADS * HEAD_DIM
KV_W = N_KV_HEADS * HEAD_DIM
CONF_DIM = 512
HY_DIM = 512
HY_IN = 3 * HY_DIM
HG_HEADS = 4
HG_DIM = 128
HG_W = HG_HEADS * HG_DIM
HG_CHUNK = 32
C_FAST_DECAY = 0.3
C_SLOW_DECAY = 1.5
C_DECAY_TARGET = 1e-2
V7X_VMEM_LIMIT = 56 * 1024 * 1024
LANES = 128


def _cparams(sem):
    return pltpu.CompilerParams(dimension_semantics=sem, vmem_limit_bytes=V7X_VMEM_LIMIT)


def _tile(n, prefs):
    for p in prefs:
        if n % p == 0:
            return p
    return n


def _norm_mod(x, gs, sh):
    r = lax.rsqrt(jnp.mean(x * x, axis=-1, keepdims=True) + NORM_EPS)
    return x * r * gs + sh


def _cast_kernel(w_ref, o_ref):
    o_ref[...] = w_ref[0].astype(o_ref.dtype)


def layer_weight_bf16(w, layer, col0=0):
    _, R, C = w.shape
    tr = _tile(R, (256, 128, 64, 32, 16))
    tc = math.gcd(C, col0) if col0 else C
    return pl.pallas_call(
        _cast_kernel,
        grid=(R // tr, (C - col0) // tc),
        in_specs=[pl.BlockSpec((1, tr, tc), lambda i, j: (layer, i, j + col0 // tc))],
        out_specs=pl.BlockSpec((tr, tc), lambda i, j: (i, j)),
        out_shape=jax.ShapeDtypeStruct((R, C - col0), BF16),
        compiler_params=_cparams(("parallel", "parallel")),
        name="layer_weight_bf16",
    )(w)


def _ada_kernel(c_ref, w_ref, b_ref, o_ref):
    c = c_ref[...]
    s = (c * jax.nn.sigmoid(c)).astype(BF16)
    o_ref[...] = jnp.dot(s, w_ref[...].astype(BF16), preferred_element_type=F32) + b_ref[...]


def ada_modulation(cs, w, b):
    R, D = cs.shape
    N = w.shape[1]
    tn = _tile(N, (1024, 512, 256, 128))
    return pl.pallas_call(
        _ada_kernel,
        grid=(N // tn,),
        in_specs=[pl.BlockSpec((R, D), lambda j: (0, 0)),
                  pl.BlockSpec((D, tn), lambda j: (0, j)),
                  pl.BlockSpec((1, tn), lambda j: (0, j))],
        out_specs=pl.BlockSpec((R, tn), lambda j: (0, j)),
        out_shape=jax.ShapeDtypeStruct((R, N), F32),
        compiler_params=_cparams(("arbitrary",)),
        name="ada_modulation",
    )(cs, w, b.reshape(1, N))


def _nm_mm_kernel(x_ref, gs_ref, sh_ref, w_ref, o_ref, *, cn):
    h = _norm_mod(x_ref[0], gs_ref[0], sh_ref[0]).astype(BF16)
    for c0 in range(0, w_ref.shape[1], cn):
        o_ref[0, :, c0:c0 + cn] = jnp.dot(h, w_ref[:, c0:c0 + cn],
                                          preferred_element_type=F32).astype(o_ref.dtype)


def norm_mod_matmul(x, gs, sh, w, out_dtype=F32):
    B, L, D = x.shape
    N = w.shape[1]
    tm = _tile(L, (512, 256, 128))
    cn = _tile(N, (512, 256, 128))
    return pl.pallas_call(
        functools.partial(_nm_mm_kernel, cn=cn),
        grid=(B, L // tm),
        in_specs=[pl.BlockSpec((1, tm, D), lambda b, i: (b, i, 0)),
                  pl.BlockSpec((1, 1, D), lambda b, i: (b, 0, 0)),
                  pl.BlockSpec((1, 1, D), lambda b, i: (b, 0, 0)),
                  pl.BlockSpec((D, N), lambda b, i: (0, 0), pipeline_mode=pl.Buffered(1))],
        out_specs=pl.BlockSpec((1, tm, N), lambda b, i: (b, i, 0)),
        out_shape=jax.ShapeDtypeStruct((B, L, N), out_dtype),
        compiler_params=_cparams(("parallel", "parallel")),
        name="norm_mod_matmul",
    )(x, gs, sh, w)


def _odd_in_kernel(x_ref, xp_ref, xn_ref, gs_ref, sh_ref, w_ref, cw_ref, cb_ref, o_ref, *, cn, conv_cols):
    tm = x_ref.shape[1]
    i = pl.program_id(1)
    has_prev = (i > 0).astype(F32)
    has_next = (i < pl.num_programs(1) - 1).astype(F32)
    gs = gs_ref[0]
    sh = sh_ref[0]
    h = _norm_mod(x_ref[0], gs, sh).astype(BF16)
    halo = jnp.concatenate([_norm_mod(xp_ref[0], gs, sh), _norm_mod(xn_ref[0], gs, sh)],
                           axis=0).astype(BF16)
    row = lax.broadcasted_iota(jnp.int32, (tm, cn), 0)
    first = row == 0
    last = row == tm - 1
    hcat = jnp.concatenate([h, halo], axis=0)

    def proj(c0):
        lhs = hcat if c0 < conv_cols else h
        return jnp.dot(lhs, w_ref[:, c0:c0 + cn], preferred_element_type=F32)

    starts = list(range(0, w_ref.shape[1], cn))
    r = proj(starts[0])
    for idx, c0 in enumerate(starts):
        r_next = proj(starts[idx + 1]) if idx + 1 < len(starts) else None
        if c0 < conv_cols:
            p = r[:tm]
            pm1 = jnp.where(first, r[tm + 7:tm + 8, :] * has_prev, pltpu.roll(p, 1, 0))
            pp1 = jnp.where(last, r[tm + 8:tm + 9, :] * has_next, pltpu.roll(p, tm - 1, 0))
            r = (cw_ref[0:1, c0:c0 + cn] * pm1 + cw_ref[1:2, c0:c0 + cn] * p
                 + cw_ref[2:3, c0:c0 + cn] * pp1 + cb_ref[:, c0:c0 + cn])
        o_ref[0, :, c0:c0 + cn] = r
        r = r_next


def odd_in_proj(x, gs, sh, w, cw, cb):
    B, L, D = x.shape
    N = w.shape[1]
    conv_cols = cw.shape[1]
    tm = _tile(L, (512, 256, 128))
    cn = _tile(math.gcd(N, conv_cols), (512, 256, 128))
    nb8 = L // 8
    t8 = tm // 8
    const = lambda b, i: (0, 0)
    return pl.pallas_call(
        functools.partial(_odd_in_kernel, cn=cn, conv_cols=conv_cols),
        grid=(B, L // tm),
        in_specs=[pl.BlockSpec((1, tm, D), lambda b, i: (b, i, 0)),
                  pl.BlockSpec((1, 8, D), lambda b, i: (b, jnp.maximum(i * t8 - 1, 0), 0)),
                  pl.BlockSpec((1, 8, D), lambda b, i: (b, jnp.minimum((i + 1) * t8, nb8 - 1), 0)),
                  pl.BlockSpec((1, 1, D), lambda b, i: (b, 0, 0)),
                  pl.BlockSpec((1, 1, D), lambda b, i: (b, 0, 0)),
                  pl.BlockSpec((D, N), const, pipeline_mode=pl.Buffered(1)),
                  pl.BlockSpec((3, conv_cols), const),
                  pl.BlockSpec((1, conv_cols), const)],
        out_specs=pl.BlockSpec((1, tm, N), lambda b, i: (b, i, 0)),
        out_shape=jax.ShapeDtypeStruct((B, L, N), F32),
        compiler_params=_cparams(("parallel", "arbitrary")),
        name="odd_in_proj",
    )(x, x, x, gs, sh, w, cw, cb.reshape(1, conv_cols))


def _even_in_kernel(x_ref, gs_ref, sh_ref, w_ref, seg_ref, cq_ref, sq_ref, ck_ref, sk_ref,
                    q_ref, klo_ref, khi_ref, vlo_ref, vhi_ref, glu_ref):
    tm = x_ref.shape[1]
    h = _norm_mod(x_ref[0], gs_ref[0], sh_ref[0]).astype(BF16)
    seg = seg_ref[...]
    lane = lax.broadcasted_iota(jnp.int32, (tm, LANES), 1)
    even = (lane & 1) == 0
    low = lane < HEAD_DIM

    def proj(col0, width):
        return jnp.dot(h, w_ref[:, col0:col0 + width], preferred_element_type=F32)

    def head_norm_rope(y, c_ref, s_ref):
        sq = y * y
        hi = sq.astype(BF16)
        lo = (sq - hi.astype(F32)).astype(BF16)
        ss = (jnp.dot(hi, seg, preferred_element_type=F32)
              + jnp.dot(lo, seg, preferred_element_type=F32))
        yn = y * lax.rsqrt(ss * (1.0 / HEAD_DIM) + NORM_EPS)
        out = []
        for j in range(2):
            c = yn[:, LANES * j:LANES * (j + 1)]
            sw = jnp.where(even, pltpu.roll(c, LANES - 1, 1), pltpu.roll(c, 1, 1))
            out.append(c * c_ref[...] + sw * s_ref[...])
        return out

    def put_padded(c, j, lo_ref, hi_ref):
        r = pltpu.roll(c, HEAD_DIM, 1)
        dt = lo_ref.dtype
        lo_ref[0, :, LANES * 2 * j:LANES * (2 * j + 1)] = jnp.where(low, c, 0.0).astype(dt)
        hi_ref[0, :, LANES * 2 * j:LANES * (2 * j + 1)] = jnp.where(low, 0.0, r).astype(dt)
        lo_ref[0, :, LANES * (2 * j + 1):LANES * (2 * j + 2)] = jnp.where(low, r, 0.0).astype(dt)
        hi_ref[0, :, LANES * (2 * j + 1):LANES * (2 * j + 2)] = jnp.where(low, 0.0, c).astype(dt)

    n_q = Q_W // 256
    n_k = KV_W // 256
    g0 = Q_W + 2 * KV_W
    y = proj(0, 256)
    for t in range(n_q + n_k):
        y_next = proj(256 * (t + 1), 256) if t + 1 < n_q + n_k else None
        if t < n_q:
            for j, c in enumerate(head_norm_rope(y, cq_ref, sq_ref)):
                q_ref[0, :, 256 * t + LANES * j:256 * t + LANES * (j + 1)] = c.astype(q_ref.dtype)
        else:
            for j, c in enumerate(head_norm_rope(y, ck_ref, sk_ref)):
                put_padded(c, 2 * (t - n_q) + j, klo_ref, khi_ref)
        y = y_next
    v = proj(Q_W + KV_W, KV_W)
    for j in range(KV_W // LANES):
        put_padded(v[:, LANES * j:LANES * (j + 1)], j, vlo_ref, vhi_ref)
    a = proj(g0, CONF_DIM)
    gate = proj(g0 + CONF_DIM, CONF_DIM)
    glu_ref[0] = a * jax.nn.sigmoid(gate)


def even_in_proj(x, gs, sh, w, seg, cq, sq, ck, sk):
    B, L, D = x.shape
    N = w.shape[1]
    tm = _tile(L, (512, 256, 128))
    row = lambda b, i: (b, i, 0)
    tab = pl.BlockSpec((tm, LANES), lambda b, i: (i, 0))
    KP = N_KV_HEADS * LANES
    kv_spec = pl.BlockSpec((1, tm, KP), row)
    kv_shape = jax.ShapeDtypeStruct((B, L, KP), BF16)
    return pl.pallas_call(
        _even_in_kernel,
        grid=(B, L // tm),
        in_specs=[pl.BlockSpec((1, tm, D), row),
                  pl.BlockSpec((1, 1, D), lambda b, i: (b, 0, 0)),
                  pl.BlockSpec((1, 1, D), lambda b, i: (b, 0, 0)),
                  pl.BlockSpec((D, N), lambda b, i: (0, 0), pipeline_mode=pl.Buffered(1)),
                  pl.BlockSpec((256, 256), lambda b, i: (0, 0)),
                  tab, tab, tab, tab],
        out_specs=[pl.BlockSpec((1, tm, Q_W), row), kv_spec, kv_spec, kv_spec, kv_spec,
                   pl.BlockSpec((1, tm, CONF_DIM), row)],
        out_shape=[jax.ShapeDtypeStruct((B, L, Q_W), BF16), kv_shape, kv_shape, kv_shape, kv_shape,
                   jax.ShapeDtypeStruct((B, L, CONF_DIM), F32)],
        compiler_params=_cparams(("parallel", "parallel")),
        name="even_in_proj",
    )(x, gs, sh, w, seg, cq, sq, ck, sk)


def _attn_kernel(q_ref, *refs, n_seg):
    o_ref = refs[4 * n_seg]
    group = N_Q_HEADS // N_KV_HEADS
    nt = (((1,), (1,)), ((), ()))

    def scores(hq):
        pr, pos = divmod(hq, 2)
        q2 = q_ref[0, :, LANES * pr:LANES * (pr + 1)]
        g = hq // group
        cols = slice(LANES * g, LANES * (g + 1))
        return [lax.dot_general(q2, refs[4 * t + pos][0, :, cols], nt, preferred_element_type=F32)
                for t in range(n_seg)]

    ss = scores(0)
    acc = None
    for hq in range(N_Q_HEADS):
        nxt = scores(hq + 1) if hq + 1 < N_Q_HEADS else None
        pr, pos = divmod(hq, 2)
        g = hq // group
        cols = slice(LANES * g, LANES * (g + 1))
        m = functools.reduce(jnp.maximum, [jnp.max(s, axis=-1, keepdims=True) for s in ss])
        ps = [jnp.exp(s - m) for s in ss]
        l = functools.reduce(jnp.add, [jnp.sum(p, axis=-1, keepdims=True) for p in ps])
        o = functools.reduce(jnp.add, [
            jnp.dot(p.astype(BF16), refs[4 * t + 2 + pos][0, :, cols], preferred_element_type=F32)
            for t, p in enumerate(ps)])
        o = o / l
        if pos == 0:
            acc = o
        else:
            o_ref[0, :, LANES * pr:LANES * (pr + 1)] = (acc + o).astype(o_ref.dtype)
        ss = nxt


def attention(q, segments):
    B, Lq, _ = q.shape
    tq = _tile(Lq, (256, 128))
    KP = N_KV_HEADS * LANES
    specs, args = [], []
    for seg in segments:
        for t in seg:
            specs.append(pl.BlockSpec((1, t.shape[1], KP), lambda b, i: (b, 0, 0)))
            args.append(t)
    return pl.pallas_call(
        functools.partial(_attn_kernel, n_seg=len(segments)),
        grid=(B, Lq // tq),
        in_specs=[pl.BlockSpec((1, tq, Q_W), lambda b, i: (b, i, 0))] + specs,
        out_specs=pl.BlockSpec((1, tq, Q_W), lambda b, i: (b, i, 0)),
        out_shape=jax.ShapeDtypeStruct((B, Lq, Q_W), BF16),
        compiler_params=_cparams(("parallel", "parallel")),
        name="attention",
    )(q, *args)


_CONV_PAD = 16


def _dwconv_kernel(x_ref, w_ref, b_ref, g_ref, beta_ref, o_ref, xp_ref, *, K, TR, ln_swish):
    L = x_ref.shape[1]
    C = x_ref.shape[2]
    half = (K - 1) // 2
    xp_ref[0:_CONV_PAD, :] = jnp.zeros((_CONV_PAD, C), F32)
    xp_ref[_CONV_PAD + L:, :] = jnp.zeros((_CONV_PAD, C), F32)
    xp_ref[_CONV_PAD:_CONV_PAD + L, :] = x_ref[0]

    groups = {}
    for k in range(K):
        q8, r8 = divmod(_CONV_PAD - half + k, 8)
        groups.setdefault(r8, []).append((k, q8))
    R = TR + 8
    CW = LANES if C % LANES == 0 else C

    def tile(t, carry):
        r0 = pl.multiple_of(t * TR, TR)
        parts = []
        for c0 in range(0, C, CW):
            acc = None
            for r8, taps in sorted(groups.items()):
                part = None
                for k, q8 in taps:
                    term = (xp_ref[pl.ds(pl.multiple_of(r0 + 8 * q8, 8), R), c0:c0 + CW]
                            * w_ref[k:k + 1, c0:c0 + CW])
                    part = term if part is None else part + term
                if r8:
                    part = pltpu.roll(part, R - r8, 0)
                part = part[:TR]
                acc = part if acc is None else acc + part
            parts.append(acc + b_ref[:, c0:c0 + CW])
        acc = parts[0] if len(parts) == 1 else jnp.concatenate(parts, axis=-1)
        if ln_swish:
            mu = jnp.mean(acc, axis=-1, keepdims=True)
            d = acc - mu
            var = jnp.mean(d * d, axis=-1, keepdims=True)
            y = d * lax.rsqrt(var + NORM_EPS) * g_ref[...] + beta_ref[...]
            acc = y * jax.nn.sigmoid(y)
        o_ref[0, pl.ds(r0, TR), :] = acc.astype(o_ref.dtype)
        return carry

    lax.fori_loop(0, L // TR, tile, 0)


def dwconv(x, w, b, g=None, beta=None, *, out_dtype=F32):
    B, L, _ = x.shape
    K, C = w.shape
    ln_swish = g is not None
    tc = C if ln_swish else _tile(C, (512, 256, 128))
    TR = _tile(L, (256, 128, 64, 32, 16, 8))
    if g is None:
        g = jnp.ones((C,), F32)
        beta = jnp.zeros((C,), F32)
    vec = pl.BlockSpec((1, tc), lambda bb, j: (0, j))
    return pl.pallas_call(
        functools.partial(_dwconv_kernel, K=K, TR=TR, ln_swish=ln_swish),
        grid=(B, C // tc),
        in_specs=[pl.BlockSpec((1, L, tc), lambda bb, j: (bb, 0, j)),
                  pl.BlockSpec((K, tc), lambda bb, j: (0, j)),
                  vec, vec, vec],
        out_specs=pl.BlockSpec((1, L, tc), lambda bb, j: (bb, 0, j)),
        out_shape=jax.ShapeDtypeStruct((B, L, C), out_dtype),
        scratch_shapes=[pltpu.VMEM((L + 2 * _CONV_PAD, tc), F32)],
        compiler_params=_cparams(("parallel", "parallel")),
        name="dwconv%d" % K,
    )(x, w, b.reshape(1, C), g.reshape(1, C), beta.reshape(1, C))


def _out_proj_kernel(x_ref, a1_ref, a2_ref, w1_ref, w2_ref, g_ref, o_ref):
    y = jnp.dot(a1_ref[0], w1_ref[...], preferred_element_type=F32)
    y = y + jnp.dot(a2_ref[0], w2_ref[...], preferred_element_type=F32)
    o_ref[0] = x_ref[0] + g_ref[0] * y


def out_proj_residual(x, a1, a2, w1, w2, gate):
    B, L, D = x.shape
    K1, K2 = a1.shape[2], a2.shape[2]
    tm = _tile(L, (1024, 512, 256, 128))
    row = lambda b, i: (b, i, 0)
    return pl.pallas_call(
        _out_proj_kernel,
        grid=(B, L // tm),
        in_specs=[pl.BlockSpec((1, tm, D), row),
                  pl.BlockSpec((1, tm, K1), row),
                  pl.BlockSpec((1, tm, K2), row),
                  pl.BlockSpec((K1, D), lambda b, i: (0, 0)),
                  pl.BlockSpec((K2, D), lambda b, i: (0, 0)),
                  pl.BlockSpec((1, 1, D), lambda b, i: (b, 0, 0))],
        out_specs=pl.BlockSpec((1, tm, D), row),
        out_shape=jax.ShapeDtypeStruct((B, L, D), F32),
        compiler_params=_cparams(("parallel", "parallel")),
        name="out_proj_residual",
    )(x, a1, a2, w1, w2, gate)


_FFN_CHUNK = 256


def _ffn_kernel(x_ref, xp_ref, xn_ref, gs_ref, sh_ref, wup_ref, cw_ref, cb_ref, wdn_ref, g5_ref,
                fg_ref, o_ref, acc_ref, *, final_norm):
    tm = x_ref.shape[1]
    FF = wdn_ref.shape[0]
    i = pl.program_id(1)
    has_prev = (i > 0).astype(F32)
    has_next = (i < pl.num_programs(1) - 1).astype(F32)
    gs = gs_ref[0]
    sh = sh_ref[0]
    x = x_ref[0]
    h = _norm_mod(x, gs, sh).astype(BF16)
    halo = jnp.concatenate([_norm_mod(xp_ref[0], gs, sh), _norm_mod(xn_ref[0], gs, sh)],
                           axis=0).astype(BF16)
    row = lax.broadcasted_iota(jnp.int32, (tm, _FFN_CHUNK), 0)
    first = row == 0
    last = row == tm - 1

    hcat = jnp.concatenate([h, halo], axis=0)

    def up(c0):
        r = jnp.dot(hcat, wup_ref[:, c0:c0 + _FFN_CHUNK], preferred_element_type=F32)
        return r[:tm], r[tm:]

    def conv3(c0, p, ph):
        prev_row = ph[7:8, :] * has_prev
        next_row = ph[8:9, :] * has_next
        pm1 = jnp.where(first, prev_row, pltpu.roll(p, 1, 0))
        pp1 = jnp.where(last, next_row, pltpu.roll(p, tm - 1, 0))
        return (cw_ref[0:1, c0:c0 + _FFN_CHUNK] * pm1 + cw_ref[1:2, c0:c0 + _FFN_CHUNK] * p
                + cw_ref[2:3, c0:c0 + _FFN_CHUNK] * pp1 + cb_ref[:, c0:c0 + _FFN_CHUNK])

    n_chunks = FF // _FFN_CHUNK
    ahead = 3
    ups = [(up(c * _FFN_CHUNK), up(FF + c * _FFN_CHUNK)) for c in range(min(ahead, n_chunks))]
    for c in range(n_chunks):
        if c + ahead < n_chunks:
            ups.append((up((c + ahead) * _FFN_CHUNK), up(FF + (c + ahead) * _FFN_CHUNK)))
        val = conv3(c * _FFN_CHUNK, *ups[c][0])
        gate = conv3(FF + c * _FFN_CHUNK, *ups[c][1])
        ups[c] = None
        act = (gate * jax.nn.sigmoid(gate) * val).astype(BF16)
        part = jnp.dot(act, wdn_ref[c * _FFN_CHUNK:(c + 1) * _FFN_CHUNK, :],
                       preferred_element_type=F32)
        if c == 0:
            acc_ref[...] = part
        else:
            acc_ref[...] += part
    y = x + g5_ref[0] * acc_ref[...]
    if final_norm:
        y = y * lax.rsqrt(jnp.mean(y * y, axis=-1, keepdims=True) + NORM_EPS) * fg_ref[...]
    o_ref[0] = y


def conv_ffn_residual(x, gs, sh, w_up, cw, cb, w_dn, gate, final_g=None):
    B, L, D = x.shape
    FF = w_dn.shape[0]
    assert FF % _FFN_CHUNK == 0
    tm = _tile(L, (256, 128))
    nb8 = L // 8
    t8 = tm // 8
    final_norm = final_g is not None
    if final_g is None:
        final_g = jnp.ones((D,), F32)
    row = lambda b, i: (b, i, 0)
    const = lambda b, i: (0, 0)
    return pl.pallas_call(
        functools.partial(_ffn_kernel, final_norm=final_norm),
        grid=(B, L // tm),
        in_specs=[pl.BlockSpec((1, tm, D), row),
                  pl.BlockSpec((1, 8, D), lambda b, i: (b, jnp.maximum(i * t8 - 1, 0), 0)),
                  pl.BlockSpec((1, 8, D), lambda b, i: (b, jnp.minimum((i + 1) * t8, nb8 - 1), 0)),
                  pl.BlockSpec((1, 1, D), lambda b, i: (b, 0, 0)),
                  pl.BlockSpec((1, 1, D), lambda b, i: (b, 0, 0)),
                  pl.BlockSpec((D, 2 * FF), const, pipeline_mode=pl.Buffered(1)),
                  pl.BlockSpec((3, 2 * FF), const),
                  pl.BlockSpec((1, 2 * FF), const),
                  pl.BlockSpec((FF, D), const, pipeline_mode=pl.Buffered(1)),
                  pl.BlockSpec((1, 1, D), lambda b, i: (b, 0, 0)),
                  pl.BlockSpec((1, D), const)],
        out_specs=pl.BlockSpec((1, tm, D), row),
        out_shape=jax.ShapeDtypeStruct((B, L, D), F32),
        scratch_shapes=[pltpu.VMEM((tm, D), F32)],
        compiler_params=_cparams(("parallel", "arbitrary")),
        name="conv_ffn",
    )(x, x, x, gs, sh, w_up, cw, cb.reshape(1, 2 * FF), w_dn, gate, final_g.reshape(1, D))


def _hy_filter_kernel(feat_ref, w1_ref, b1_ref, w2_ref, b2_ref, w3_ref, fr_ref, dec_ref,
                      hs_ref, hd_ref, nyq_ref):
    L = feat_ref.shape[0]
    fr = fr_ref[...]
    z = jnp.dot(feat_ref[...].astype(BF16), w1_ref[...].astype(BF16), preferred_element_type=F32)
    hdn = jnp.sin(fr * (z + b1_ref[...]))
    z = jnp.dot(hdn.astype(BF16), w2_ref[...].astype(BF16), preferred_element_type=F32)
    hdn = jnp.sin(fr * (z + b2_ref[...])).astype(BF16)
    row = lax.broadcasted_iota(jnp.int32, (L, HY_DIM), 0)
    not_first = (row > 0).astype(F32)
    sign = jnp.where((row & 1) == 0, 1.0, -1.0).astype(F32)
    dec = dec_ref[...]
    n_order = w3_ref.shape[1] // (2 * HY_DIM)
    for o in range(n_order):
        c0 = o * 2 * HY_DIM
        hf = jnp.dot(hdn, w3_ref[:, c0:c0 + HY_DIM].astype(BF16), preferred_element_type=F32) * dec
        hb = (jnp.dot(hdn, w3_ref[:, c0 + HY_DIM:c0 + 2 * HY_DIM].astype(BF16),
                      preferred_element_type=F32) * dec * not_first)
        inv = 1.0 / (jnp.sum(jnp.abs(hf), axis=0, keepdims=True)
                     + jnp.sum(jnp.abs(hb), axis=0, keepdims=True))
        hs = (hf + hb) * inv
        hd = (hf - hb) * inv
        hs_ref[:, o * HY_DIM:(o + 1) * HY_DIM] = hs
        hd_ref[:, o * HY_DIM:(o + 1) * HY_DIM] = hd
        nyq_ref[:, o * HY_DIM:(o + 1) * HY_DIM] = jnp.sum(hs * sign, axis=0, keepdims=True)


def hyena_filter_taps(L, w1, b1, w2, b2, w3, freq):
    emb = w1.shape[0]
    bands = (emb - 1) // 2
    t = jnp.linspace(0.0, 1.0, L, dtype=F32)[:, None]
    fb = jnp.linspace(1e-4, bands - 1, bands, dtype=F32)
    wpos = (2.0 * math.pi / L) * jnp.arange(L, dtype=F32)[:, None]
    feats = jnp.concatenate([t, jnp.cos(fb * wpos), -jnp.sin(fb * wpos)], axis=-1)
    feats = jnp.pad(feats, ((0, 0), (0, LANES - emb)))
    w1p = jnp.pad(w1, ((0, LANES - emb), (0, 0)))
    deltas = jnp.abs(jnp.linspace(math.log(C_DECAY_TARGET) / C_SLOW_DECAY,
                                  math.log(C_DECAY_TARGET) / C_FAST_DECAY, HY_DIM, dtype=F32))
    dec = jnp.exp(-t * deltas)
    hid = w1.shape[1]
    NO = w3.shape[1] // 2
    full = lambda shape: pl.BlockSpec(shape, lambda: tuple(0 for _ in shape))
    return pl.pallas_call(
        _hy_filter_kernel,
        in_specs=[full((L, LANES)), full((LANES, hid)), full((1, hid)), full((hid, hid)),
                  full((1, hid)), full((hid, 2 * NO)), full((1, hid)), full((L, HY_DIM))],
        out_specs=[full((L, NO)), full((L, NO)), full((1, NO))],
        out_shape=[jax.ShapeDtypeStruct((L, NO), F32), jax.ShapeDtypeStruct((L, NO), F32),
                   jax.ShapeDtypeStruct((1, NO), F32)],
        compiler_params=pltpu.CompilerParams(vmem_limit_bytes=V7X_VMEM_LIMIT),
        name="hyena_filter",
    )(feats, w1p, b1.reshape(1, hid), w2, b2.reshape(1, hid), w3, freq.reshape(1, hid), dec)


_DFT_SPLIT = 64


def _dft_tables(L):
    N = 2 * L
    n = jnp.arange(L, dtype=jnp.int32)

    def cos_sin(mult):
        ang = ((mult[:, None] * n[None, :]) & (N - 1)).astype(F32) * (2.0 * math.pi / N)
        return jnp.cos(ang), jnp.sin(ang)

    ca, sa = cos_sin(jnp.arange(L // _DFT_SPLIT, dtype=jnp.int32) * _DFT_SPLIT)
    cb, sb = cos_sin(jnp.arange(_DFT_SPLIT, dtype=jnp.int32))
    nyq = jnp.where((n & 1) == 0, 1.0, -1.0).astype(F32)
    re = (ca[:, None, :] * cb[None] - sa[:, None, :] * sb[None]).reshape(L, L)
    im = -(sa[:, None, :] * cb[None] + ca[:, None, :] * sb[None]).reshape(L, L)
    im = jnp.where(jnp.arange(L)[:, None] == 0, nyq[None, :], im)
    f = jnp.concatenate([re, im], axis=0).astype(BF16)
    cat, sat, cbt, sbt = ca.T, sa.T, cb.T, sb.T
    re_t = (cat[:, :, None] * cbt[:, None, :] - sat[:, :, None] * sbt[:, None, :]).reshape(L, L)
    im_t = -(sat[:, :, None] * cbt[:, None, :] + cat[:, :, None] * sbt[:, None, :]).reshape(L, L)
    im_t = jnp.where(jnp.arange(L)[None, :] == 0, nyq[:, None], im_t)
    g = jnp.concatenate([re_t, im_t], axis=1).astype(BF16)
    return f, g


def _hy_spec_kernel(fre_ref, fim_ref, hs_ref, hd_ref, nyq_ref, a_ref, b_ref):
    kb = fre_ref.shape[0]
    L = fre_ref.shape[1]
    NO = hs_ref.shape[1]
    krow = lax.broadcasted_iota(jnp.int32, (kb, NO), 0) + pl.program_id(0) * kb
    dc = krow == 0
    scale = jnp.where(dc, 1.0, 2.0).astype(F32) * (1.0 / (2 * L))
    hr = jnp.dot(fre_ref[...], hs_ref[...].astype(BF16), preferred_element_type=F32)
    hi = jnp.dot(fim_ref[...], hd_ref[...].astype(BF16), preferred_element_type=F32)
    a_ref[...] = hr * scale
    b_ref[...] = jnp.where(dc, nyq_ref[...], hi) * scale


def hyena_spectrum(F, hs, hd, nyq):
    L, NO = hs.shape
    kb = _tile(L, (256, 128))
    nk = L // kb
    return pl.pallas_call(
        _hy_spec_kernel,
        grid=(nk,),
        in_specs=[pl.BlockSpec((kb, L), lambda i: (i, 0)),
                  pl.BlockSpec((kb, L), lambda i: (i + nk, 0)),
                  pl.BlockSpec((L, NO), lambda i: (0, 0)),
                  pl.BlockSpec((L, NO), lambda i: (0, 0)),
                  pl.BlockSpec((1, NO), lambda i: (0, 0))],
        out_specs=[pl.BlockSpec((kb, NO), lambda i: (i, 0)),
                   pl.BlockSpec((kb, NO), lambda i: (i, 0))],
        out_shape=[jax.ShapeDtypeStruct((L, NO), F32), jax.ShapeDtypeStruct((L, NO), F32)],
        compiler_params=_cparams(("arbitrary",)),
        name="hyena_spectrum",
    )(F, F, hs, hd, nyq)


def _hy_conv_kernel(z_ref, gate_ref, fre_ref, fim_ref, gre_ref, gim_ref, a_ref, b_ref, skip_ref,
                    o_ref, zb_ref, y_ref):
    kb = fre_ref.shape[0]
    C = z_ref.shape[2]
    kstep = pl.program_id(1)

    @pl.when(kstep == 0)
    def _():
        zb_ref[...] = z_ref[0].astype(BF16)
        y_ref[...] = jnp.zeros(y_ref.shape, F32)

    zb = zb_ref[...]
    NH = 2
    hb = kb // NH
    fwd = []
    for hf in range(NH):
        rows = slice(hf * hb, (hf + 1) * hb)
        fwd.append((jnp.dot(fre_ref[rows, :], zb, preferred_element_type=F32),
                    jnp.dot(fim_ref[rows, :], zb, preferred_element_type=F32)))
    for hf in range(NH):
        rows = slice(hf * hb, (hf + 1) * hb)
        ur, ui = fwd[hf]
        a = a_ref[rows, :]
        b = b_ref[rows, :]
        krow = lax.broadcasted_iota(jnp.int32, (hb, C), 0) + (kstep * kb + hf * hb)
        dc = krow == 0
        vr = jnp.where(dc, ur * a, ur * a - ui * b)
        vi = jnp.where(dc, ui * b, ur * b + ui * a)
        y_ref[...] += (jnp.dot(gre_ref[:, rows], vr.astype(BF16), preferred_element_type=F32)
                       + jnp.dot(gim_ref[:, rows], vi.astype(BF16), preferred_element_type=F32))

    @pl.when(kstep == pl.num_programs(1) - 1)
    def _():
        o_ref[0] = (gate_ref[0] * (y_ref[...] + skip_ref[...] * z_ref[0])).astype(o_ref.dtype)


def hyena_conv(z_src, z_col, gate_src, gate_col, F, G, A, Bt, spec_col, skip, out_dtype):
    B, L, _ = z_src.shape
    C = HY_DIM
    kb = _tile(L, (512, 256))
    nk = L // kb
    return pl.pallas_call(
        _hy_conv_kernel,
        grid=(B, nk),
        in_specs=[pl.BlockSpec((1, L, C), lambda b, k: (b, 0, z_col)),
                  pl.BlockSpec((1, L, C), lambda b, k: (b, 0, gate_col)),
                  pl.BlockSpec((kb, L), lambda b, k: (k, 0)),
                  pl.BlockSpec((kb, L), lambda b, k: (k + nk, 0)),
                  pl.BlockSpec((L, kb), lambda b, k: (0, k)),
                  pl.BlockSpec((L, kb), lambda b, k: (0, k + nk)),
                  pl.BlockSpec((kb, C), lambda b, k: (k, spec_col)),
                  pl.BlockSpec((kb, C), lambda b, k: (k, spec_col)),
                  pl.BlockSpec((1, C), lambda b, k: (0, 0))],
        out_specs=pl.BlockSpec((1, L, C), lambda b, k: (b, 0, 0)),
        out_shape=jax.ShapeDtypeStruct((B, L, C), out_dtype),
        scratch_shapes=[pltpu.VMEM((L, C), BF16), pltpu.VMEM((L, C), F32)],
        compiler_params=_cparams(("parallel", "arbitrary")),
        name="hyena_conv",
    )(z_src, gate_src, F, F, G, G, A, Bt, skip.reshape(1, C))


HG_GROUP = 128


def _hgrn2_kernel(ql_ref, ffl_ref, fbl_ref, il_ref, gl_ref, qc_ref, ffc_ref, fbc_ref, ic_ref,
                  lb_ref, gn_ref, o_ref, vb_ref, qdf_ref, qdb_ref, qgf_ref, qgb_ref,
                  kuf_ref, kub_ref, kgf_ref, kgb_ref, kdf_ref, kdb_ref, decf_ref, decb_ref,
                  oif_ref, oib_ref, stf_ref, stb_ref):
    C = HG_CHUNK
    GR = HG_GROUP
    G2 = 2 * C
    L = ql_ref.shape[1]
    Lc = qc_ref.shape[1]
    r = lax.broadcasted_iota(jnp.int32, (GR, GR), 0)
    c = lax.broadcasted_iota(jnp.int32, (GR, GR), 1)
    same = (r // C) == (c // C)
    same_group = (r // G2) == (c // G2)
    row_odd = ((lax.broadcasted_iota(jnp.int32, (GR, HG_DIM), 0) // C) & 1) == 1
    dirs = []
    for d, (incl, excl) in enumerate(((c <= r, c > r), (c >= r, c < r))):
        mask = same & incl
        cross = same_group & jnp.logical_not(same) & incl
        second = row_odd if d == 0 else jnp.logical_not(row_odd)
        dirs.append((lb_ref[d:d + 1, :], mask.astype(BF16), (same & excl).astype(BF16), mask, cross,
                     second))
    f_refs_lat = (ffl_ref, fbl_ref)
    f_refs_ctx = (ffc_ref, fbc_ref)
    qd_refs = (qdf_ref, qdb_ref)
    qg_refs = (qgf_ref, qgb_ref)
    ku_refs = (kuf_ref, kub_ref)
    kg_refs = (kgf_ref, kgb_ref)
    kd_refs = (kdf_ref, kdb_ref)
    dec_refs = (decf_ref, decb_ref)
    oi_refs = (oif_ref, oib_ref)
    st_refs = (stf_ref, stb_ref)
    nt = (((1,), (1,)), ((), ()))

    def gates(q_ref, f_refs, i_ref, n_rows, row_off, with_out):
        def body(gi, carry):
            r0 = pl.multiple_of(gi * GR, GR)
            ro = pl.multiple_of(row_off + gi * GR, GR)
            q = q_ref[0, pl.ds(r0, GR), :]
            vb_ref[pl.ds(ro, GR), :] = i_ref[0, pl.ds(r0, GR), :].astype(BF16)
            for d, (lbd, t_in, t_ex, _, _, second) in enumerate(dirs):
                f = lbd + (1.0 - lbd) * jax.nn.sigmoid(f_refs[d][0, pl.ds(r0, GR), :])
                k = 1.0 - f
                lg = jnp.log(f)
                hi = lg.astype(BF16)
                lo = (lg - hi.astype(F32)).astype(BF16)
                b = (jnp.dot(t_in, hi, preferred_element_type=F32)
                     + jnp.dot(t_in, lo, preferred_element_type=F32))
                rest = (jnp.dot(t_ex, hi, preferred_element_type=F32)
                        + jnp.dot(t_ex, lo, preferred_element_type=F32))
                tot = b + rest
                other = jnp.where(row_odd, pltpu.roll(tot, C, 0), pltpu.roll(tot, GR - C, 0))
                qd_refs[d][pl.ds(ro, GR), :] = (q * jnp.exp(b)).astype(BF16)
                qg_refs[d][pl.ds(ro, GR), :] = (
                    q * jnp.exp(b + jnp.where(second, other, 0.0))).astype(BF16)
                ku_refs[d][pl.ds(ro, GR), :] = (k * jnp.exp(rest)).astype(BF16)
                kg_refs[d][pl.ds(ro, GR), :] = (
                    k * jnp.exp(rest + jnp.where(second, 0.0, other))).astype(BF16)
                dec_refs[d][pl.ds(ro, GR), :] = tot + other
                if with_out:
                    kd_refs[d][pl.ds(r0, GR), :] = (k * jnp.exp(-b)).astype(BF16)
            return carry

        lax.fori_loop(0, n_rows // GR, body, 0, unroll=2)

    def intra():
        U = 4 if (L // GR) % 4 == 0 else 1

        def body(gi, carry):
            jobs = []
            for u in range(U):
                r0 = pl.multiple_of((gi * U + u) * GR, GR)
                ro = pl.multiple_of(Lc + (gi * U + u) * GR, GR)
                for d in range(2):
                    qd = qd_refs[d][pl.ds(ro, GR), :]
                    sc = lax.dot_general(qd, kd_refs[d][pl.ds(r0, GR), :], nt,
                                         preferred_element_type=F32)
                    sx = lax.dot_general(qd, ku_refs[d][pl.ds(ro, GR), :], nt,
                                         preferred_element_type=F32)
                    jobs.append((d, r0, ro, sc, sx))
            for d, r0, ro, sc, sx in jobs:
                sc = jnp.where(dirs[d][3], sc, jnp.where(dirs[d][4], sx, 0.0)).astype(BF16)
                oi_refs[d][pl.ds(r0, GR), :] = jnp.dot(sc, vb_ref[pl.ds(ro, GR), :],
                                                       preferred_element_type=F32)
            return carry

        lax.fori_loop(0, L // (GR * U), body, 0)

    def recurrence(n_rows, row_off, keep, carry):
        n = n_rows // G2

        def body(j, states):
            new = []
            for d, st in enumerate(states):
                jj = j if d == 0 else n - 1 - j
                ro = pl.multiple_of(row_off + jj * G2, G2)
                if keep:
                    st_refs[d][pl.ds(pl.multiple_of(jj * HG_DIM, HG_DIM), HG_DIM), :] = st.astype(BF16)
                upd = lax.dot_general(vb_ref[pl.ds(ro, G2), :], kg_refs[d][pl.ds(ro, G2), :],
                                      (((0,), (0,)), ((), ())), preferred_element_type=F32)
                new.append(st * jnp.exp(dec_refs[d][pl.ds(ro, 1), :]) + upd)
            return tuple(new)

        return lax.fori_loop(0, n, body, carry, unroll=8)

    def inter():
        def body(j, carry):
            r0 = pl.multiple_of(j * G2, G2)
            ro = pl.multiple_of(Lc + j * G2, G2)
            so = pl.multiple_of(j * HG_DIM, HG_DIM)
            for d in range(2):
                o = lax.dot_general(qg_refs[d][pl.ds(ro, G2), :], st_refs[d][pl.ds(so, HG_DIM), :], nt,
                                    preferred_element_type=F32)
                oi_refs[d][pl.ds(r0, G2), :] += o
            return carry

        lax.fori_loop(0, L // G2, body, 0, unroll=8)

    gates(qc_ref, f_refs_ctx, ic_ref, Lc, 0, False)
    gates(ql_ref, f_refs_lat, il_ref, L, Lc, True)
    intra()
    zero = jnp.zeros((HG_DIM, HG_DIM), F32)
    carry = recurrence(Lc, 0, False, (zero, zero))
    recurrence(L, Lc, True, carry)
    inter()
    o = oif_ref[...] + oib_ref[...]
    o = o * lax.rsqrt(jnp.mean(o * o, axis=-1, keepdims=True) + NORM_EPS) * gn_ref[...]
    g = gl_ref[0]
    o_ref[0] = (o * (g * jax.nn.sigmoid(g))).astype(o_ref.dtype)


def hgrn2_mixer(p_lat, lat_col0, p_ctx, ctx_col0, lb, gn_g):
    B, L, _ = p_lat.shape
    Lc = p_ctx.shape[1]
    lc0 = lat_col0 // HG_DIM
    cc0 = ctx_col0 // HG_DIM

    def lat(seg):
        return pl.BlockSpec((1, L, HG_DIM), lambda b, h: (b, 0, lc0 + HG_HEADS * seg + h))

    def ctx(seg):
        return pl.BlockSpec((1, Lc, HG_DIM), lambda b, h: (b, 0, cc0 + HG_HEADS * seg + h))

    return pl.pallas_call(
        _hgrn2_kernel,
        grid=(B, HG_HEADS),
        in_specs=[lat(0), lat(1), lat(2), lat(3), lat(4), ctx(0), ctx(1), ctx(2), ctx(3),
                  pl.BlockSpec((2, HG_DIM), lambda b, h: (0, h)),
                  pl.BlockSpec((1, HG_DIM), lambda b, h: (0, 0))],
        out_specs=pl.BlockSpec((1, L, HG_DIM), lambda b, h: (b, 0, h)),
        out_shape=jax.ShapeDtypeStruct((B, L, HG_W), BF16),
        scratch_shapes=([pltpu.VMEM((Lc + L, HG_DIM), BF16)] * 9
                        + [pltpu.VMEM((L, HG_DIM), BF16)] * 2
                        + [pltpu.VMEM((Lc + L, HG_DIM), F32)] * 2
                        + [pltpu.VMEM((L, HG_DIM), F32)] * 2
                        + [pltpu.VMEM((L // (2 * HG_CHUNK) * HG_DIM, HG_DIM), BF16)] * 2),
        compiler_params=_cparams(("parallel", "parallel")),
        name="hgrn2",
    )(p_lat, p_lat, p_lat, p_lat, p_lat, p_ctx, p_ctx, p_ctx, p_ctx, lb, gn_g.reshape(1, HG_DIM))


def _rope_tables(L, gain, scale, rotate):
    g2 = jnp.tile(gain.astype(F32), LANES // HEAD_DIM)[None, :] * scale
    gsw = g2.reshape(1, LANES // 2, 2)[:, :, ::-1].reshape(1, LANES)
    if not rotate:
        return jnp.broadcast_to(g2, (L, LANES)), jnp.zeros((L, LANES), F32)
    rows = L // GRID_W
    row = jnp.repeat(jnp.arange(rows, dtype=F32), GRID_W)
    col = jnp.tile(jnp.arange(GRID_W, dtype=F32), rows)
    axis_dim = HEAD_DIM // 2
    inv_freq = ROPE_THETA ** (-jnp.arange(0, axis_dim, 2, dtype=F32) / axis_dim)
    ang = jnp.concatenate([row[:, None] * inv_freq, col[:, None] * inv_freq], axis=-1)
    cos = jnp.tile(jnp.repeat(jnp.cos(ang), 2, axis=-1), (1, LANES // HEAD_DIM))
    sin = jnp.tile(jnp.repeat(jnp.sin(ang), 2, axis=-1), (1, LANES // HEAD_DIM))
    sign = jnp.tile(jnp.array([-1.0, 1.0], F32), LANES // 2)[None, :]
    return cos * g2, sin * sign * gsw


def _split6(m):
    return jnp.split(m, 6, axis=-1)


def kernel(x, c, ctx, c_ctx, ada_w, ada_b, norm1_g, norm2_g, final_g, ev_w_in, ev_q_gain, ev_k_gain,
           ev_conv_w, ev_conv_b, ev_ln_g, ev_ln_b, ev_w_out, od_w_in, od_short_w, od_short_b,
           od_filt_w1, od_filt_b1, od_filt_w2, od_filt_b2, od_filt_w3, od_filt_freq, od_hyena_skip,
           od_lower_bound, od_gnorm_g, od_w_out, ffn_w_up, ffn_conv_w, ffn_conv_b, ffn_w_down):
    B, L, D = x.shape
    Lc = ctx.shape[1]
    depth = ada_w.shape[0]
    assert depth == 2, "layer 0 (attention + conformer) followed by layer 1 (hyena + hgrn2)"

    lb_soft = jax.nn.softmax(od_lower_bound.astype(F32), axis=0)
    lower_bounds = jnp.cumsum(lb_soft, axis=0) - lb_soft[0:1]

    cond = jnp.concatenate([c, c_ctx[None, :], jnp.zeros((7, D), F32)], axis=0)

    def mods(l):
        m = ada_modulation(cond, ada_w[l], ada_b[l])
        ml = [t[:, None, :] for t in _split6(m[:B])]
        mc = [jnp.broadcast_to(t[None], (B, 1, D)) for t in _split6(m[B:B + 1])]
        return ml, mc

    def ffn(xx, mm, l, final=None):
        return conv_ffn_residual(xx, norm2_g[l] * (1.0 + mm[4]), mm[3], w_up[l],
                                 ffn_conv_w[l], ffn_conv_b[l], w_down[l], mm[5], final)

    w_up = [layer_weight_bf16(ffn_w_up, l) for l in range(depth)]
    w_down = [layer_weight_bf16(ffn_w_down, l) for l in range(depth)]

    ml, mc = mods(0)
    w_in = layer_weight_bf16(ev_w_in, 0)
    seg = jnp.kron(jnp.eye(256 // HEAD_DIM, dtype=F32), jnp.ones((HEAD_DIM, HEAD_DIM), F32)).astype(BF16)
    scale = HEAD_DIM ** -0.5
    cq_l, sq_l = _rope_tables(L, ev_q_gain[0], scale, True)
    ck_l, sk_l = _rope_tables(L, ev_k_gain[0], 1.0, True)
    cq_c, sq_c = _rope_tables(Lc, ev_q_gain[0], scale, False)
    ck_c, sk_c = _rope_tables(Lc, ev_k_gain[0], 1.0, False)
    q_l, *kv_l, glu_l = even_in_proj(x, norm1_g[0] * (1.0 + ml[1]), ml[0], w_in, seg,
                                     cq_l, sq_l, ck_l, sk_l)
    q_c, *kv_c, glu_c = even_in_proj(ctx, norm1_g[0] * (1.0 + mc[1]), mc[0], w_in, seg,
                                     cq_c, sq_c, ck_c, sk_c)
    attn_l = attention(q_l, [kv_c, kv_l])
    attn_c = attention(q_c, [kv_c])
    conv_l = dwconv(glu_l, ev_conv_w[0], ev_conv_b[0], ev_ln_g[0], ev_ln_b[0], out_dtype=BF16)
    conv_c = dwconv(glu_c, ev_conv_w[0], ev_conv_b[0], ev_ln_g[0], ev_ln_b[0], out_dtype=BF16)
    wo = layer_weight_bf16(ev_w_out, 0)
    wo1 = wo[:Q_W]
    wo2 = wo[Q_W:]
    x = out_proj_residual(x, attn_l, conv_l, wo1, wo2, ml[2])
    ctx = out_proj_residual(ctx, attn_c, conv_c, wo1, wo2, mc[2])
    x = ffn(x, ml, 0)
    ctx = ffn(ctx, mc, 0)

    ml, mc = mods(1)
    w_in = layer_weight_bf16(od_w_in, 0)
    p_lat = odd_in_proj(x, norm1_g[1] * (1.0 + ml[1]), ml[0], w_in, od_short_w[0], od_short_b[0])
    p_ctx = norm_mod_matmul(ctx, norm1_g[1] * (1.0 + mc[1]), mc[0],
                            layer_weight_bf16(od_w_in, 0, HY_IN))
    d_lat = hgrn2_mixer(p_lat, HY_IN, p_ctx, 0, lower_bounds[1], od_gnorm_g[0])

    hs, hd, nyq = hyena_filter_taps(L, od_filt_w1[0], od_filt_b1[0], od_filt_w2[0], od_filt_b2[0],
                                    od_filt_w3[0], od_filt_freq[0])
    F, G = _dft_tables(L)
    A, Bt = hyena_spectrum(F, hs, hd, nyq)
    z1 = hyena_conv(p_lat, 0, p_lat, 1, F, G, A, Bt, 0, od_hyena_skip[0][0], F32)
    c_lat = hyena_conv(z1, 0, p_lat, 2, F, G, A, Bt, 1, od_hyena_skip[0][1], BF16)

    wo = layer_weight_bf16(od_w_out, 0)
    x = out_proj_residual(x, c_lat, d_lat, wo[:HY_DIM], wo[HY_DIM:], ml[2])
    return ffn(x, ml, 1, final_g)
```

```python
import functools
import math

import jax
import jax.numpy as jnp
from jax import lax
from jax.experimental import pallas as pl
from jax.experimental.pallas import tpu as pltpu

F32 = jnp.float32
BF16 = jnp.bfloat16

NORM_EPS = 1e-6
GRID_W = 64
ROPE_THETA = 10000.0
HEAD_DIM = 64
N_Q_HEADS = 12
N_KV_HEADS = 4
Q_W = N_Q_HEADS * HEAD_DIM
KV_W = N_KV_HEADS * HEAD_DIM
CONF_DIM = 512
HY_DIM = 512
HY_IN = 3 * HY_DIM
HG_HEADS = 4
HG_DIM = 128
HG_W = HG_HEADS * HG_DIM
HG_CHUNK = 32
C_FAST_DECAY = 0.3
C_SLOW_DECAY = 1.5
C_DECAY_TARGET = 1e-2
V7X_VMEM_LIMIT = 56 * 1024 * 1024
LANES = 128


def _cparams(sem):
    return pltpu.CompilerParams(dimension_semantics=sem, vmem_limit_bytes=V7X_VMEM_LIMIT)


def _tile(n, prefs):
    for p in prefs:
        if n % p == 0:
            return p
    return n


def _norm_mod(x, gs, sh):
    r = lax.rsqrt(jnp.mean(x * x, axis=-1, keepdims=True) + NORM_EPS)
    return x * r * gs + sh


def _cast_kernel(w_ref, o_ref):
    o_ref[...] = w_ref[0].astype(o_ref.dtype)


def layer_weight_bf16(w, layer, col0=0):
    _, R, C = w.shape
    tr = _tile(R, (256, 128, 64, 32, 16))
    tc = math.gcd(C, col0) if col0 else C
    return pl.pallas_call(
        _cast_kernel,
        grid=(R // tr, (C - col0) // tc),
        in_specs=[pl.BlockSpec((1, tr, tc), lambda i, j: (layer, i, j + col0 // tc))],
        out_specs=pl.BlockSpec((tr, tc), lambda i, j: (i, j)),
        out_shape=jax.ShapeDtypeStruct((R, C - col0), BF16),
        compiler_params=_cparams(("parallel", "parallel")),
        name="layer_weight_bf16",
    )(w)


def _ada_kernel(c_ref, w_ref, b_ref, o_ref):
    c = c_ref[...]
    s = (c * jax.nn.sigmoid(c)).astype(BF16)
    o_ref[...] = jnp.dot(s, w_ref[...].astype(BF16), preferred_element_type=F32) + b_ref[...]


def ada_modulation(cs, w, b):
    R, D = cs.shape
    N = w.shape[1]
    tn = _tile(N, (1024, 512, 256, 128))
    return pl.pallas_call(
        _ada_kernel,
        grid=(N // tn,),
        in_specs=[pl.BlockSpec((R, D), lambda j: (0, 0)),
                  pl.BlockSpec((D, tn), lambda j: (0, j)),
                  pl.BlockSpec((1, tn), lambda j: (0, j))],
        out_specs=pl.BlockSpec((R, tn), lambda j: (0, j)),
        out_shape=jax.ShapeDtypeStruct((R, N), F32),
        compiler_params=_cparams(("arbitrary",)),
        name="ada_modulation",
    )(cs, w, b.reshape(1, N))


def _nm_mm_kernel(x_ref, gs_ref, sh_ref, w_ref, o_ref, *, cn):
    h = _norm_mod(x_ref[0], gs_ref[0], sh_ref[0]).astype(BF16)
    for c0 in range(0, w_ref.shape[1], cn):
        o_ref[0, :, c0:c0 + cn] = jnp.dot(h, w_ref[:, c0:c0 + cn],
                                          preferred_element_type=F32).astype(o_ref.dtype)


def norm_mod_matmul(x, gs, sh, w, out_dtype=F32):
    B, L, D = x.shape
    N = w.shape[1]
    tm = _tile(L, (512, 256, 128))
    cn = _tile(N, (512, 256, 128))
    return pl.pallas_call(
        functools.partial(_nm_mm_kernel, cn=cn),
        grid=(B, L // tm),
        in_specs=[pl.BlockSpec((1, tm, D), lambda b, i: (b, i, 0)),
                  pl.BlockSpec((1, 1, D), lambda b, i: (b, 0, 0)),
                  pl.BlockSpec((1, 1, D), lambda b, i: (b, 0, 0)),
                  pl.BlockSpec((D, N), lambda b, i: (0, 0), pipeline_mode=pl.Buffered(1))],
        out_specs=pl.BlockSpec((1, tm, N), lambda b, i: (b, i, 0)),
        out_shape=jax.ShapeDtypeStruct((B, L, N), out_dtype),
        compiler_params=_cparams(("parallel", "parallel")),
        name="norm_mod_matmul",
    )(x, gs, sh, w)


def _odd_in_kernel(x_ref, xp_ref, xn_ref, gs_ref, sh_ref, w_ref, cw_ref, cb_ref, o_ref, *, cn, conv_cols):
    tm = x_ref.shape[1]
    i = pl.program_id(1)
    has_prev = (i > 0).astype(F32)
    has_next = (i < pl.num_programs(1) - 1).astype(F32)
    gs = gs_ref[0]
    sh = sh_ref[0]
    h = _norm_mod(x_ref[0], gs, sh).astype(BF16)
    halo = jnp.concatenate([_norm_mod(xp_ref[0], gs, sh), _norm_mod(xn_ref[0], gs, sh)],
                           axis=0).astype(BF16)
    row = lax.broadcasted_iota(jnp.int32, (tm, cn), 0)
    first = row == 0
    last = row == tm - 1
    hcat = jnp.concatenate([h, halo], axis=0)

    def proj(c0):
        lhs = hcat if c0 < conv_cols else h
        return jnp.dot(lhs, w_ref[:, c0:c0 + cn], preferred_element_type=F32)

    starts = list(range(0, w_ref.shape[1], cn))
    r = proj(starts[0])
    for idx, c0 in enumerate(starts):
        r_next = proj(starts[idx + 1]) if idx + 1 < len(starts) else None
        if c0 < conv_cols:
            p = r[:tm]
            pm1 = jnp.where(first, r[tm + 7:tm + 8, :] * has_prev, pltpu.roll(p, 1, 0))
            pp1 = jnp.where(last, r[tm + 8:tm + 9, :] * has_next, pltpu.roll(p, tm - 1, 0))
            r = (cw_ref[0:1, c0:c0 + cn] * pm1 + cw_ref[1:2, c0:c0 + cn] * p
                 + cw_ref[2:3, c0:c0 + cn] * pp1 + cb_ref[:, c0:c0 + cn])
        o_ref[0, :, c0:c0 + cn] = r
        r = r_next


def odd_in_proj(x, gs, sh, w, cw, cb):
    B, L, D = x.shape
    N = w.shape[1]
    conv_cols = cw.shape[1]
    tm = _tile(L, (1024, 512, 256, 128))
    cn = _tile(math.gcd(N, conv_cols), (512, 256, 128))
    nb8 = L // 8
    t8 = tm // 8
    const = lambda b, i: (0, 0)
    return pl.pallas_call(
        functools.partial(_odd_in_kernel, cn=cn, conv_cols=conv_cols),
        grid=(B, L // tm),
        in_specs=[pl.BlockSpec((1, tm, D), lambda b, i: (b, i, 0)),
                  pl.BlockSpec((1, 8, D), lambda b, i: (b, jnp.maximum(i * t8 - 1, 0), 0)),
                  pl.BlockSpec((1, 8, D), lambda b, i: (b, jnp.minimum((i + 1) * t8, nb8 - 1), 0)),
                  pl.BlockSpec((1, 1, D), lambda b, i: (b, 0, 0)),
                  pl.BlockSpec((1, 1, D), lambda b, i: (b, 0, 0)),
                  pl.BlockSpec((D, N), const, pipeline_mode=pl.Buffered(1)),
                  pl.BlockSpec((3, conv_cols), const),
                  pl.BlockSpec((1, conv_cols), const)],
        out_specs=pl.BlockSpec((1, tm, N), lambda b, i: (b, i, 0)),
        out_shape=jax.ShapeDtypeStruct((B, L, N), F32),
        compiler_params=_cparams(("parallel", "arbitrary")),
        name="odd_in_proj",
    )(x, x, x, gs, sh, w, cw, cb.reshape(1, conv_cols))


def _even_in_kernel(x_ref, gs_ref, sh_ref, w_ref, seg_ref, cq_ref, sq_ref, ck_ref, sk_ref,
                    q_ref, klo_ref, khi_ref, vlo_ref, vhi_ref, glu_ref):
    tm = x_ref.shape[1]
    h = _norm_mod(x_ref[0], gs_ref[0], sh_ref[0]).astype(BF16)
    seg = seg_ref[...]
    lane = lax.broadcasted_iota(jnp.int32, (tm, LANES), 1)
    even = (lane & 1) == 0
    low = lane < HEAD_DIM

    def proj(col0, width):
        return jnp.dot(h, w_ref[:, col0:col0 + width], preferred_element_type=F32)

    def head_norm_rope(y, c_ref, s_ref):
        sq = y * y
        hi = sq.astype(BF16)
        lo = (sq - hi.astype(F32)).astype(BF16)
        ss = (jnp.dot(hi, seg, preferred_element_type=F32)
              + jnp.dot(lo, seg, preferred_element_type=F32))
        yn = y * lax.rsqrt(ss * (1.0 / HEAD_DIM) + NORM_EPS)
        out = []
        for j in range(2):
            c = yn[:, LANES * j:LANES * (j + 1)]
            sw = jnp.where(even, pltpu.roll(c, LANES - 1, 1), pltpu.roll(c, 1, 1))
            out.append(c * c_ref[...] + sw * s_ref[...])
        return out

    def put_padded(c, j, lo_ref, hi_ref):
        r = pltpu.roll(c, HEAD_DIM, 1)
        dt = lo_ref.dtype
        lo_ref[0, :, LANES * 2 * j:LANES * (2 * j + 1)] = jnp.where(low, c, 0.0).astype(dt)
        hi_ref[0, :, LANES * 2 * j:LANES * (2 * j + 1)] = jnp.where(low, 0.0, r).astype(dt)
        lo_ref[0, :, LANES * (2 * j + 1):LANES * (2 * j + 2)] = jnp.where(low, r, 0.0).astype(dt)
        hi_ref[0, :, LANES * (2 * j + 1):LANES * (2 * j + 2)] = jnp.where(low, 0.0, c).astype(dt)

    n_q = Q_W // 256
    n_k = KV_W // 256
    g0 = Q_W + 2 * KV_W
    y = proj(0, 256)
    for t in range(n_q + n_k):
        y_next = proj(256 * (t + 1), 256) if t + 1 < n_q + n_k else None
        if t < n_q:
            for j, c in enumerate(head_norm_rope(y, cq_ref, sq_ref)):
                q_ref[0, :, 256 * t + LANES * j:256 * t + LANES * (j + 1)] = c.astype(q_ref.dtype)
        else:
            for j, c in enumerate(head_norm_rope(y, ck_ref, sk_ref)):
                put_padded(c, 2 * (t - n_q) + j, klo_ref, khi_ref)
        y = y_next
    v = proj(Q_W + KV_W, KV_W)
    for j in range(KV_W // LANES):
        put_padded(v[:, LANES * j:LANES * (j + 1)], j, vlo_ref, vhi_ref)
    a = proj(g0, CONF_DIM)
    gate = proj(g0 + CONF_DIM, CONF_DIM)
    glu_ref[0] = a * jax.nn.sigmoid(gate)


def even_in_proj(x, gs, sh, w, seg, cq, sq, ck, sk):
    B, L, D = x.shape
    N = w.shape[1]
    tm = _tile(L, (1024, 512, 256, 128))
    row = lambda b, i: (b, i, 0)
    tab = pl.BlockSpec((tm, LANES), lambda b, i: (i, 0))
    KP = N_KV_HEADS * LANES
    kv_spec = pl.BlockSpec((1, tm, KP), row)
    kv_shape = jax.ShapeDtypeStruct((B, L, KP), BF16)
    return pl.pallas_call(
        _even_in_kernel,
        grid=(B, L // tm),
        in_specs=[pl.BlockSpec((1, tm, D), row),
                  pl.BlockSpec((1, 1, D), lambda b, i: (b, 0, 0)),
                  pl.BlockSpec((1, 1, D), lambda b, i: (b, 0, 0)),
                  pl.BlockSpec((D, N), lambda b, i: (0, 0), pipeline_mode=pl.Buffered(1)),
                  pl.BlockSpec((256, 256), lambda b, i: (0, 0)),
                  tab, tab, tab, tab],
        out_specs=[pl.BlockSpec((1, tm, Q_W), row), kv_spec, kv_spec, kv_spec, kv_spec,
                   pl.BlockSpec((1, tm, CONF_DIM), row)],
        out_shape=[jax.ShapeDtypeStruct((B, L, Q_W), BF16), kv_shape, kv_shape, kv_shape, kv_shape,
                   jax.ShapeDtypeStruct((B, L, CONF_DIM), F32)],
        compiler_params=_cparams(("parallel", "parallel")),
        name="even_in_proj",
    )(x, gs, sh, w, seg, cq, sq, ck, sk)


def _attn_kernel(q_ref, *refs, n_seg):
    o_ref = refs[4 * n_seg]
    group = N_Q_HEADS // N_KV_HEADS
    nt = (((1,), (1,)), ((), ()))

    def scores(hq):
        pr, pos = divmod(hq, 2)
        q2 = q_ref[0, :, LANES * pr:LANES * (pr + 1)]
        g = hq // group
        cols = slice(LANES * g, LANES * (g + 1))
        return [lax.dot_general(q2, refs[4 * t + pos][0, :, cols], nt, preferred_element_type=F32)
                for t in range(n_seg)]

    ss = scores(0)
    acc = None
    for hq in range(N_Q_HEADS):
        nxt = scores(hq + 1) if hq + 1 < N_Q_HEADS else None
        pr, pos = divmod(hq, 2)
        g = hq // group
        cols = slice(LANES * g, LANES * (g + 1))
        m = functools.reduce(jnp.maximum, [jnp.max(s, axis=-1, keepdims=True) for s in ss])
        ps = [jnp.exp(s - m) for s in ss]
        l = functools.reduce(jnp.add, [jnp.sum(p, axis=-1, keepdims=True) for p in ps])
        o = functools.reduce(jnp.add, [
            jnp.dot(p.astype(BF16), refs[4 * t + 2 + pos][0, :, cols], preferred_element_type=F32)
            for t, p in enumerate(ps)])
        o = o / l
        if pos == 0:
            acc = o
        else:
            o_ref[0, :, LANES * pr:LANES * (pr + 1)] = (acc + o).astype(o_ref.dtype)
        ss = nxt


def attention(q, segments):
    B, Lq, _ = q.shape
    tq = _tile(Lq, (256, 128))
    KP = N_KV_HEADS * LANES
    specs, args = [], []
    for seg in segments:
        for t in seg:
            specs.append(pl.BlockSpec((1, t.shape[1], KP), lambda b, i: (b, 0, 0)))
            args.append(t)
    return pl.pallas_call(
        functools.partial(_attn_kernel, n_seg=len(segments)),
        grid=(B, Lq // tq),
        in_specs=[pl.BlockSpec((1, tq, Q_W), lambda b, i: (b, i, 0))] + specs,
        out_specs=pl.BlockSpec((1, tq, Q_W), lambda b, i: (b, i, 0)),
        out_shape=jax.ShapeDtypeStruct((B, Lq, Q_W), BF16),
        compiler_params=_cparams(("parallel", "parallel")),
        name="attention",
    )(q, *args)


_CONV_PAD = 16


def _dwconv_kernel(x_ref, w_ref, b_ref, g_ref, beta_ref, o_ref, xp_ref, *, K, TR, ln_swish):
    L = x_ref.shape[1]
    C = x_ref.shape[2]
    half = (K - 1) // 2
    xp_ref[0:_CONV_PAD, :] = jnp.zeros((_CONV_PAD, C), F32)
    xp_ref[_CONV_PAD + L:, :] = jnp.zeros((_CONV_PAD, C), F32)
    xp_ref[_CONV_PAD:_CONV_PAD + L, :] = x_ref[0]

    groups = {}
    for k in range(K):
        q8, r8 = divmod(_CONV_PAD - half + k, 8)
        groups.setdefault(r8, []).append((k, q8))
    R = TR + 8
    CW = LANES if C % LANES == 0 else C

    def tile(t, carry):
        r0 = pl.multiple_of(t * TR, TR)
        parts = []
        for c0 in range(0, C, CW):
            acc = None
            for r8, taps in sorted(groups.items()):
                part = None
                for k, q8 in taps:
                    term = (xp_ref[pl.ds(pl.multiple_of(r0 + 8 * q8, 8), R), c0:c0 + CW]
                            * w_ref[k:k + 1, c0:c0 + CW])
                    part = term if part is None else part + term
                if r8:
                    part = pltpu.roll(part, R - r8, 0)
                part = part[:TR]
                acc = part if acc is None else acc + part
            parts.append(acc + b_ref[:, c0:c0 + CW])
        acc = parts[0] if len(parts) == 1 else jnp.concatenate(parts, axis=-1)
        if ln_swish:
            mu = jnp.mean(acc, axis=-1, keepdims=True)
            d = acc - mu
            var = jnp.mean(d * d, axis=-1, keepdims=True)
            y = d * lax.rsqrt(var + NORM_EPS) * g_ref[...] + beta_ref[...]
            acc = y * jax.nn.sigmoid(y)
        o_ref[0, pl.ds(r0, TR), :] = acc.astype(o_ref.dtype)
        return carry

    lax.fori_loop(0, L // TR, tile, 0)


def dwconv(x, w, b, g=None, beta=None, *, out_dtype=F32):
    B, L, _ = x.shape
    K, C = w.shape
    ln_swish = g is not None
    tc = C if ln_swish else _tile(C, (512, 256, 128))
    TR = _tile(L, (256, 128, 64, 32, 16, 8))
    if g is None:
        g = jnp.ones((C,), F32)
        beta = jnp.zeros((C,), F32)
    vec = pl.BlockSpec((1, tc), lambda bb, j: (0, j))
    return pl.pallas_call(
        functools.partial(_dwconv_kernel, K=K, TR=TR, ln_swish=ln_swish),
        grid=(B, C // tc),
        in_specs=[pl.BlockSpec((1, L, tc), lambda bb, j: (bb, 0, j)),
                  pl.BlockSpec((K, tc), lambda bb, j: (0, j)),
                  vec, vec, vec],
        out_specs=pl.BlockSpec((1, L, tc), lambda bb, j: (bb, 0, j)),
        out_shape=jax.ShapeDtypeStruct((B, L, C), out_dtype),
        scratch_shapes=[pltpu.VMEM((L + 2 * _CONV_PAD, tc), F32)],
        compiler_params=_cparams(("parallel", "parallel")),
        name="dwconv%d" % K,
    )(x, w, b.reshape(1, C), g.reshape(1, C), beta.reshape(1, C))


def _out_proj_kernel(x_ref, a1_ref, a2_ref, w1_ref, w2_ref, g_ref, o_ref):
    y = jnp.dot(a1_ref[0], w1_ref[...], preferred_element_type=F32)
    y = y + jnp.dot(a2_ref[0], w2_ref[...], preferred_element_type=F32)
    o_ref[0] = x_ref[0] + g_ref[0] * y


def out_proj_residual(x, a1, a2, w1, w2, gate):
    B, L, D = x.shape
    K1, K2 = a1.shape[2], a2.shape[2]
    tm = _tile(L, (2048, 1024, 512, 256, 128))
    row = lambda b, i: (b, i, 0)
    return pl.pallas_call(
        _out_proj_kernel,
        grid=(B, L // tm),
        in_specs=[pl.BlockSpec((1, tm, D), row),
                  pl.BlockSpec((1, tm, K1), row),
                  pl.BlockSpec((1, tm, K2), row),
                  pl.BlockSpec((K1, D), lambda b, i: (0, 0)),
                  pl.BlockSpec((K2, D), lambda b, i: (0, 0)),
                  pl.BlockSpec((1, 1, D), lambda b, i: (b, 0, 0))],
        out_specs=pl.BlockSpec((1, tm, D), row),
        out_shape=jax.ShapeDtypeStruct((B, L, D), F32),
        compiler_params=_cparams(("parallel", "parallel")),
        name="out_proj_residual",
    )(x, a1, a2, w1, w2, gate)


_FFN_CHUNK = 256


def _ffn_kernel(x_ref, xp_ref, xn_ref, gs_ref, sh_ref, wup_ref, cw_ref, cb_ref, wdn_ref, g5_ref,
                fg_ref, o_ref, acc_ref, *, final_norm):
    tm = x_ref.shape[1]
    FF = wdn_ref.shape[0]
    i = pl.program_id(1)
    has_prev = (i > 0).astype(F32)
    has_next = (i < pl.num_programs(1) - 1).astype(F32)
    gs = gs_ref[0]
    sh = sh_ref[0]
    x = x_ref[0]
    h = _norm_mod(x, gs, sh).astype(BF16)
    halo = jnp.concatenate([_norm_mod(xp_ref[0], gs, sh), _norm_mod(xn_ref[0], gs, sh)],
                           axis=0).astype(BF16)
    row = lax.broadcasted_iota(jnp.int32, (tm, _FFN_CHUNK), 0)
    first = row == 0
    last = row == tm - 1

    hcat = jnp.concatenate([h, halo], axis=0)

    def up(c0):
        r = jnp.dot(hcat, wup_ref[:, c0:c0 + _FFN_CHUNK], preferred_element_type=F32)
        return r[:tm], r[tm:]

    def conv3(c0, p, ph):
        prev_row = ph[7:8, :] * has_prev
        next_row = ph[8:9, :] * has_next
        pm1 = jnp.where(first, prev_row, pltpu.roll(p, 1, 0))
        pp1 = jnp.where(last, next_row, pltpu.roll(p, tm - 1, 0))
        return (cw_ref[0:1, c0:c0 + _FFN_CHUNK] * pm1 + cw_ref[1:2, c0:c0 + _FFN_CHUNK] * p
                + cw_ref[2:3, c0:c0 + _FFN_CHUNK] * pp1 + cb_ref[:, c0:c0 + _FFN_CHUNK])

    n_chunks = FF // _FFN_CHUNK
    ahead = 3
    ups = [(up(c * _FFN_CHUNK), up(FF + c * _FFN_CHUNK)) for c in range(min(ahead, n_chunks))]
    for c in range(n_chunks):
        if c + ahead < n_chunks:
            ups.append((up((c + ahead) * _FFN_CHUNK), up(FF + (c + ahead) * _FFN_CHUNK)))
        val = conv3(c * _FFN_CHUNK, *ups[c][0])
        gate = conv3(FF + c * _FFN_CHUNK, *ups[c][1])
        ups[c] = None
        act = (gate * jax.nn.sigmoid(gate) * val).astype(BF16)
        part = jnp.dot(act, wdn_ref[c * _FFN_CHUNK:(c + 1) * _FFN_CHUNK, :],
                       preferred_element_type=F32)
        if c == 0:
            acc_ref[...] = part
        else:
            acc_ref[...] += part
    y = x + g5_ref[0] * acc_ref[...]
    if final_norm:
        y = y * lax.rsqrt(jnp.mean(y * y, axis=-1, keepdims=True) + NORM_EPS) * fg_ref[...]
    o_ref[0] = y


def conv_ffn_residual(x, gs, sh, w_up, cw, cb, w_dn, gate, final_g=None):
    B, L, D = x.shape
    FF = w_dn.shape[0]
    assert FF % _FFN_CHUNK == 0
    tm = _tile(L, (256, 128))
    nb8 = L // 8
    t8 = tm // 8
    final_norm = final_g is not None
    if final_g is None:
        final_g = jnp.ones((D,), F32)
    row = lambda b, i: (b, i, 0)
    const = lambda b, i: (0, 0)
    return pl.pallas_call(
        functools.partial(_ffn_kernel, final_norm=final_norm),
        grid=(B, L // tm),
        in_specs=[pl.BlockSpec((1, tm, D), row),
                  pl.BlockSpec((1, 8, D), lambda b, i: (b, jnp.maximum(i * t8 - 1, 0), 0)),
                  pl.BlockSpec((1, 8, D), lambda b, i: (b, jnp.minimum((i + 1) * t8, nb8 - 1), 0)),
                  pl.BlockSpec((1, 1, D), lambda b, i: (b, 0, 0)),
                  pl.BlockSpec((1, 1, D), lambda b, i: (b, 0, 0)),
                  pl.BlockSpec((D, 2 * FF), const, pipeline_mode=pl.Buffered(1)),
                  pl.BlockSpec((3, 2 * FF), const),
                  pl.BlockSpec((1, 2 * FF), const),
                  pl.BlockSpec((FF, D), const, pipeline_mode=pl.Buffered(1)),
                  pl.BlockSpec((1, 1, D), lambda b, i: (b, 0, 0)),
                  pl.BlockSpec((1, D), const)],
        out_specs=pl.BlockSpec((1, tm, D), row),
        out_shape=jax.ShapeDtypeStruct((B, L, D), F32),
        scratch_shapes=[pltpu.VMEM((tm, D), F32)],
        compiler_params=_cparams(("parallel", "arbitrary")),
        name="conv_ffn",
    )(x, x, x, gs, sh, w_up, cw, cb.reshape(1, 2 * FF), w_dn, gate, final_g.reshape(1, D))


def _hy_filter_kernel(feat_ref, w1_ref, b1_ref, w2_ref, b2_ref, w3_ref, fr_ref, dec_ref,
                      hs_ref, hd_ref, nyq_ref):
    L = feat_ref.shape[0]
    fr = fr_ref[...]
    z = jnp.dot(feat_ref[...].astype(BF16), w1_ref[...].astype(BF16), preferred_element_type=F32)
    hdn = jnp.sin(fr * (z + b1_ref[...]))
    z = jnp.dot(hdn.astype(BF16), w2_ref[...].astype(BF16), preferred_element_type=F32)
    hdn = jnp.sin(fr * (z + b2_ref[...])).astype(BF16)
    row = lax.broadcasted_iota(jnp.int32, (L, HY_DIM), 0)
    not_first = (row > 0).astype(F32)
    sign = jnp.where((row & 1) == 0, 1.0, -1.0).astype(F32)
    dec = dec_ref[...]
    n_order = w3_ref.shape[1] // (2 * HY_DIM)
    for o in range(n_order):
        c0 = o * 2 * HY_DIM
        hf = jnp.dot(hdn, w3_ref[:, c0:c0 + HY_DIM].astype(BF16), preferred_element_type=F32) * dec
        hb = (jnp.dot(hdn, w3_ref[:, c0 + HY_DIM:c0 + 2 * HY_DIM].astype(BF16),
                      preferred_element_type=F32) * dec * not_first)
        inv = 1.0 / (jnp.sum(jnp.abs(hf), axis=0, keepdims=True)
                     + jnp.sum(jnp.abs(hb), axis=0, keepdims=True))
        hs = (hf + hb) * inv
        hd = (hf - hb) * inv
        hs_ref[:, o * HY_DIM:(o + 1) * HY_DIM] = hs
        hd_ref[:, o * HY_DIM:(o + 1) * HY_DIM] = hd
        nyq_ref[:, o * HY_DIM:(o + 1) * HY_DIM] = jnp.sum(hs * sign, axis=0, keepdims=True)


def hyena_filter_taps(L, w1, b1, w2, b2, w3, freq):
    emb = w1.shape[0]
    bands = (emb - 1) // 2
    t = jnp.linspace(0.0, 1.0, L, dtype=F32)[:, None]
    fb = jnp.linspace(1e-4, bands - 1, bands, dtype=F32)
    wpos = (2.0 * math.pi / L) * jnp.arange(L, dtype=F32)[:, None]
    feats = jnp.concatenate([t, jnp.cos(fb * wpos), -jnp.sin(fb * wpos)], axis=-1)
    feats = jnp.pad(feats, ((0, 0), (0, LANES - emb)))
    w1p = jnp.pad(w1, ((0, LANES - emb), (0, 0)))
    deltas = jnp.abs(jnp.linspace(math.log(C_DECAY_TARGET) / C_SLOW_DECAY,
                                  math.log(C_DECAY_TARGET) / C_FAST_DECAY, HY_DIM, dtype=F32))
    dec = jnp.exp(-t * deltas)
    hid = w1.shape[1]
    NO = w3.shape[1] // 2
    full = lambda shape: pl.BlockSpec(shape, lambda: tuple(0 for _ in shape))
    return pl.pallas_call(
        _hy_filter_kernel,
        in_specs=[full((L, LANES)), full((LANES, hid)), full((1, hid)), full((hid, hid)),
                  full((1, hid)), full((hid, 2 * NO)), full((1, hid)), full((L, HY_DIM))],
        out_specs=[full((L, NO)), full((L, NO)), full((1, NO))],
        out_shape=[jax.ShapeDtypeStruct((L, NO), F32), jax.ShapeDtypeStruct((L, NO), F32),
                   jax.ShapeDtypeStruct((1, NO), F32)],
        compiler_params=pltpu.CompilerParams(vmem_limit_bytes=V7X_VMEM_LIMIT),
        name="hyena_filter",
    )(feats, w1p, b1.reshape(1, hid), w2, b2.reshape(1, hid), w3, freq.reshape(1, hid), dec)


_DFT_SPLIT = 64


def _dft_tables(L):
    N = 2 * L
    n = jnp.arange(L, dtype=jnp.int32)

    def cos_sin(mult):
        ang = ((mult[:, None] * n[None, :]) & (N - 1)).astype(F32) * (2.0 * math.pi / N)
        return jnp.cos(ang), jnp.sin(ang)

    ca, sa = cos_sin(jnp.arange(L // _DFT_SPLIT, dtype=jnp.int32) * _DFT_SPLIT)
    cb, sb = cos_sin(jnp.arange(_DFT_SPLIT, dtype=jnp.int32))
    nyq = jnp.where((n & 1) == 0, 1.0, -1.0).astype(F32)
    re = (ca[:, None, :] * cb[None] - sa[:, None, :] * sb[None]).reshape(L, L)
    im = -(sa[:, None, :] * cb[None] + ca[:, None, :] * sb[None]).reshape(L, L)
    im = jnp.where(jnp.arange(L)[:, None] == 0, nyq[None, :], im)
    f = jnp.concatenate([re, im], axis=0).astype(BF16)
    cat, sat, cbt, sbt = ca.T, sa.T, cb.T, sb.T
    re_t = (cat[:, :, None] * cbt[:, None, :] - sat[:, :, None] * sbt[:, None, :]).reshape(L, L)
    im_t = -(sat[:, :, None] * cbt[:, None, :] + cat[:, :, None] * sbt[:, None, :]).reshape(L, L)
    im_t = jnp.where(jnp.arange(L)[None, :] == 0, nyq[:, None], im_t)
    g = jnp.concatenate([re_t, im_t], axis=1).astype(BF16)
    return f, g


def _hy_spec_kernel(fre_ref, fim_ref, hs_ref, hd_ref, nyq_ref, a_ref, b_ref):
    kb = fre_ref.shape[0]
    L = fre_ref.shape[1]
    NO = hs_ref.shape[1]
    krow = lax.broadcasted_iota(jnp.int32, (kb, NO), 0) + pl.program_id(0) * kb
    dc = krow == 0
    scale = jnp.where(dc, 1.0, 2.0).astype(F32) * (1.0 / (2 * L))
    hr = jnp.dot(fre_ref[...], hs_ref[...].astype(BF16), preferred_element_type=F32)
    hi = jnp.dot(fim_ref[...], hd_ref[...].astype(BF16), preferred_element_type=F32)
    a_ref[...] = hr * scale
    b_ref[...] = jnp.where(dc, nyq_ref[...], hi) * scale


def hyena_spectrum(F, hs, hd, nyq):
    L, NO = hs.shape
    kb = _tile(L, (256, 128))
    nk = L // kb
    return pl.pallas_call(
        _hy_spec_kernel,
        grid=(nk,),
        in_specs=[pl.BlockSpec((kb, L), lambda i: (i, 0)),
                  pl.BlockSpec((kb, L), lambda i: (i + nk, 0)),
                  pl.BlockSpec((L, NO), lambda i: (0, 0)),
                  pl.BlockSpec((L, NO), lambda i: (0, 0)),
                  pl.BlockSpec((1, NO), lambda i: (0, 0))],
        out_specs=[pl.BlockSpec((kb, NO), lambda i: (i, 0)),
                   pl.BlockSpec((kb, NO), lambda i: (i, 0))],
        out_shape=[jax.ShapeDtypeStruct((L, NO), F32), jax.ShapeDtypeStruct((L, NO), F32)],
        compiler_params=_cparams(("arbitrary",)),
        name="hyena_spectrum",
    )(F, F, hs, hd, nyq)


def _hy_conv_kernel(z_ref, gate_ref, fre_ref, fim_ref, gre_ref, gim_ref, a_ref, b_ref, skip_ref,
                    o_ref, zb_ref, y_ref):
    kb = fre_ref.shape[0]
    C = z_ref.shape[2]
    kstep = pl.program_id(1)

    @pl.when(kstep == 0)
    def _():
        zb_ref[...] = z_ref[0].astype(BF16)
        y_ref[...] = jnp.zeros(y_ref.shape, F32)

    zb = zb_ref[...]
    NH = 2
    hb = kb // NH
    fwd = []
    for hf in range(NH):
        rows = slice(hf * hb, (hf + 1) * hb)
        fwd.append((jnp.dot(fre_ref[rows, :], zb, preferred_element_type=F32),
                    jnp.dot(fim_ref[rows, :], zb, preferred_element_type=F32)))
    for hf in range(NH):
        rows = slice(hf * hb, (hf + 1) * hb)
        ur, ui = fwd[hf]
        a = a_ref[rows, :]
        b = b_ref[rows, :]
        krow = lax.broadcasted_iota(jnp.int32, (hb, C), 0) + (kstep * kb + hf * hb)
        dc = krow == 0
        vr = jnp.where(dc, ur * a, ur * a - ui * b)
        vi = jnp.where(dc, ui * b, ur * b + ui * a)
        y_ref[...] += (jnp.dot(gre_ref[:, rows], vr.astype(BF16), preferred_element_type=F32)
                       + jnp.dot(gim_ref[:, rows], vi.astype(BF16), preferred_element_type=F32))

    @pl.when(kstep == pl.num_programs(1) - 1)
    def _():
        o_ref[0] = (gate_ref[0] * (y_ref[...] + skip_ref[...] * z_ref[0])).astype(o_ref.dtype)


def hyena_conv(z_src, z_col, gate_src, gate_col, F, G, A, Bt, spec_col, skip, out_dtype):
    B, L, _ = z_src.shape
    C = HY_DIM
    kb = _tile(L, (512, 256))
    nk = L // kb
    return pl.pallas_call(
        _hy_conv_kernel,
        grid=(B, nk),
        in_specs=[pl.BlockSpec((1, L, C), lambda b, k: (b, 0, z_col)),
                  pl.BlockSpec((1, L, C), lambda b, k: (b, 0, gate_col)),
                  pl.BlockSpec((kb, L), lambda b, k: (k, 0)),
                  pl.BlockSpec((kb, L), lambda b, k: (k + nk, 0)),
                  pl.BlockSpec((L, kb), lambda b, k: (0, k)),
                  pl.BlockSpec((L, kb), lambda b, k: (0, k + nk)),
                  pl.BlockSpec((kb, C), lambda b, k: (k, spec_col)),
                  pl.BlockSpec((kb, C), lambda b, k: (k, spec_col)),
                  pl.BlockSpec((1, C), lambda b, k: (0, 0))],
        out_specs=pl.BlockSpec((1, L, C), lambda b, k: (b, 0, 0)),
        out_shape=jax.ShapeDtypeStruct((B, L, C), out_dtype),
        scratch_shapes=[pltpu.VMEM((L, C), BF16), pltpu.VMEM((L, C), F32)],
        compiler_params=_cparams(("parallel", "arbitrary")),
        name="hyena_conv",
    )(z_src, gate_src, F, F, G, G, A, Bt, skip.reshape(1, C))


HG_GROUP = 128


def _hgrn2_kernel(ql_ref, ffl_ref, fbl_ref, il_ref, gl_ref, qc_ref, ffc_ref, fbc_ref, ic_ref,
                  lb_ref, gn_ref, o_ref, vb_ref, qdf_ref, qdb_ref, qgf_ref, qgb_ref,
                  kuf_ref, kub_ref, kgf_ref, kgb_ref, kdf_ref, kdb_ref, decf_ref, decb_ref,
                  oif_ref, oib_ref, stf_ref, stb_ref):
    C = HG_CHUNK
    GR = HG_GROUP
    G2 = 2 * C
    L = ql_ref.shape[1]
    Lc = qc_ref.shape[1]
    r = lax.broadcasted_iota(jnp.int32, (GR, GR), 0)
    c = lax.broadcasted_iota(jnp.int32, (GR, GR), 1)
    same = (r // C) == (c // C)
    same_group = (r // G2) == (c // G2)
    row_odd = ((lax.broadcasted_iota(jnp.int32, (GR, HG_DIM), 0) // C) & 1) == 1
    dirs = []
    for d, (incl, excl) in enumerate(((c <= r, c > r), (c >= r, c < r))):
        mask = same & incl
        cross = same_group & jnp.logical_not(same) & incl
        second = row_odd if d == 0 else jnp.logical_not(row_odd)
        dirs.append((lb_ref[d:d + 1, :], mask.astype(BF16), (same & excl).astype(BF16), mask, cross,
                     second))
    f_refs_lat = (ffl_ref, fbl_ref)
    f_refs_ctx = (ffc_ref, fbc_ref)
    qd_refs = (qdf_ref, qdb_ref)
    qg_refs = (qgf_ref, qgb_ref)
    ku_refs = (kuf_ref, kub_ref)
    kg_refs = (kgf_ref, kgb_ref)
    kd_refs = (kdf_ref, kdb_ref)
    dec_refs = (decf_ref, decb_ref)
    oi_refs = (oif_ref, oib_ref)
    st_refs = (stf_ref, stb_ref)
    nt = (((1,), (1,)), ((), ()))

    def gates(q_ref, f_refs, i_ref, n_rows, row_off, with_out):
        def body(gi, carry):
            r0 = pl.multiple_of(gi * GR, GR)
            ro = pl.multiple_of(row_off + gi * GR, GR)
            q = q_ref[0, pl.ds(r0, GR), :]
            vb_ref[pl.ds(ro, GR), :] = i_ref[0, pl.ds(r0, GR), :].astype(BF16)
            for d, (lbd, t_in, t_ex, _, _, second) in enumerate(dirs):
                f = lbd + (1.0 - lbd) * jax.nn.sigmoid(f_refs[d][0, pl.ds(r0, GR), :])
                k = 1.0 - f
                lg = jnp.log(f)
                hi = lg.astype(BF16)
                lo = (lg - hi.astype(F32)).astype(BF16)
                b = (jnp.dot(t_in, hi, preferred_element_type=F32)
                     + jnp.dot(t_in, lo, preferred_element_type=F32))
                rest = (jnp.dot(t_ex, hi, preferred_element_type=F32)
                        + jnp.dot(t_ex, lo, preferred_element_type=F32))
                tot = b + rest
                other = jnp.where(row_odd, pltpu.roll(tot, C, 0), pltpu.roll(tot, GR - C, 0))
                qd_refs[d][pl.ds(ro, GR), :] = (q * jnp.exp(b)).astype(BF16)
                qg_refs[d][pl.ds(ro, GR), :] = (
                    q * jnp.exp(b + jnp.where(second, other, 0.0))).astype(BF16)
                ku_refs[d][pl.ds(ro, GR), :] = (k * jnp.exp(rest)).astype(BF16)
                kg_refs[d][pl.ds(ro, GR), :] = (
                    k * jnp.exp(rest + jnp.where(second, 0.0, other))).astype(BF16)
                dec_refs[d][pl.ds(ro, GR), :] = tot + other
                if with_out:
                    kd_refs[d][pl.ds(r0, GR), :] = (k * jnp.exp(-b)).astype(BF16)
            return carry

        lax.fori_loop(0, n_rows // GR, body, 0, unroll=2)

    def intra():
        U = 4 if (L // GR) % 4 == 0 else 1

        def body(gi, carry):
            jobs = []
            for u in range(U):
                r0 = pl.multiple_of((gi * U + u) * GR, GR)
                ro = pl.multiple_of(Lc + (gi * U + u) * GR, GR)
                for d in range(2):
                    qd = qd_refs[d][pl.ds(ro, GR), :]
                    sc = lax.dot_general(qd, kd_refs[d][pl.ds(r0, GR), :], nt,
                                         preferred_element_type=F32)
                    sx = lax.dot_general(qd, ku_refs[d][pl.ds(ro, GR), :], nt,
                                         preferred_element_type=F32)
                    jobs.append((d, r0, ro, sc, sx))
            for d, r0, ro, sc, sx in jobs:
                sc = jnp.where(dirs[d][3], sc, jnp.where(dirs[d][4], sx, 0.0)).astype(BF16)
                oi_refs[d][pl.ds(r0, GR), :] = jnp.dot(sc, vb_ref[pl.ds(ro, GR), :],
                                                       preferred_element_type=F32)
            return carry

        lax.fori_loop(0, L // (GR * U), body, 0)

    def recurrence(n_rows, row_off, keep, carry):
        n = n_rows // G2

        def body(j, states):
            new = []
            for d, st in enumerate(states):
                jj = j if d == 0 else n - 1 - j
                ro = pl.multiple_of(row_off + jj * G2, G2)
                if keep:
                    st_refs[d][pl.ds(pl.multiple_of(jj * HG_DIM, HG_DIM), HG_DIM), :] = st.astype(BF16)
                upd = lax.dot_general(vb_ref[pl.ds(ro, G2), :], kg_refs[d][pl.ds(ro, G2), :],
                                      (((0,), (0,)), ((), ())), preferred_element_type=F32)
                new.append(st * jnp.exp(dec_refs[d][pl.ds(ro, 1), :]) + upd)
            return tuple(new)

        return lax.fori_loop(0, n, body, carry, unroll=8)

    def inter():
        def body(j, carry):
            r0 = pl.multiple_of(j * G2, G2)
            ro = pl.multiple_of(Lc + j * G2, G2)
            so = pl.multiple_of(j * HG_DIM, HG_DIM)
            for d in range(2):
                o = lax.dot_general(qg_refs[d][pl.ds(ro, G2), :], st_refs[d][pl.ds(so, HG_DIM), :], nt,
                                    preferred_element_type=F32)
                oi_refs[d][pl.ds(r0, G2), :] += o
            return carry

        lax.fori_loop(0, L // G2, body, 0, unroll=8)

    gates(qc_ref, f_refs_ctx, ic_ref, Lc, 0, False)
    gates(ql_ref, f_refs_lat, il_ref, L, Lc, True)
    intra()
    zero = jnp.zeros((HG_DIM, HG_DIM), F32)
    carry = recurrence(Lc, 0, False, (zero, zero))
    recurrence(L, Lc, True, carry)
    inter()
    o = oif_ref[...] + oib_ref[...]
    o = o * lax.rsqrt(jnp.mean(o * o, axis=-1, keepdims=True) + NORM_EPS) * gn_ref[...]
    g = gl_ref[0]
    o_ref[0] = (o * (g * jax.nn.sigmoid(g))).astype(o_ref.dtype)


def hgrn2_mixer(p_lat, lat_col0, p_ctx, ctx_col0, lb, gn_g):
    B, L, _ = p_lat.shape
    Lc = p_ctx.shape[1]
    lc0 = lat_col0 // HG_DIM
    cc0 = ctx_col0 // HG_DIM

    def lat(seg):
        return pl.BlockSpec((1, L, HG_DIM), lambda b, h: (b, 0, lc0 + HG_HEADS * seg + h))

    def ctx(seg):
        return pl.BlockSpec((1, Lc, HG_DIM), lambda b, h: (b, 0, cc0 + HG_HEADS * seg + h))

    return pl.pallas_call(
        _hgrn2_kernel,
        grid=(B, HG_HEADS),
        in_specs=[lat(0), lat(1), lat(2), lat(3), lat(4), ctx(0), ctx(1), ctx(2), ctx(3),
                  pl.BlockSpec((2, HG_DIM), lambda b, h: (0, h)),
                  pl.BlockSpec((1, HG_DIM), lambda b, h: (0, 0))],
        out_specs=pl.BlockSpec((1, L, HG_DIM), lambda b, h: (b, 0, h)),
        out_shape=jax.ShapeDtypeStruct((B, L, HG_W), BF16),
        scratch_shapes=([pltpu.VMEM((Lc + L, HG_DIM), BF16)] * 9
                        + [pltpu.VMEM((L, HG_DIM), BF16)] * 2
                        + [pltpu.VMEM((Lc + L, HG_DIM), F32)] * 2
                        + [pltpu.VMEM((L, HG_DIM), F32)] * 2
                        + [pltpu.VMEM((L // (2 * HG_CHUNK) * HG_DIM, HG_DIM), BF16)] * 2),
        compiler_params=_cparams(("parallel", "parallel")),
        name="hgrn2",
    )(p_lat, p_lat, p_lat, p_lat, p_lat, p_ctx, p_ctx, p_ctx, p_ctx, lb, gn_g.reshape(1, HG_DIM))


def _rope_tables(L, gain, scale, rotate):
    g2 = jnp.tile(gain.astype(F32), LANES // HEAD_DIM)[None, :] * scale
    gsw = g2.reshape(1, LANES // 2, 2)[:, :, ::-1].reshape(1, LANES)
    if not rotate:
        return jnp.broadcast_to(g2, (L, LANES)), jnp.zeros((L, LANES), F32)
    rows = L // GRID_W
    row = jnp.repeat(jnp.arange(rows, dtype=F32), GRID_W)
    col = jnp.tile(jnp.arange(GRID_W, dtype=F32), rows)
    axis_dim = HEAD_DIM // 2
    inv_freq = ROPE_THETA ** (-jnp.arange(0, axis_dim, 2, dtype=F32) / axis_dim)
    ang = jnp.concatenate([row[:, None] * inv_freq, col[:, None] * inv_freq], axis=-1)
    cos = jnp.tile(jnp.repeat(jnp.cos(ang), 2, axis=-1), (1, LANES // HEAD_DIM))
    sin = jnp.tile(jnp.repeat(jnp.sin(ang), 2, axis=-1), (1, LANES // HEAD_DIM))
    sign = jnp.tile(jnp.array([-1.0, 1.0], F32), LANES // 2)[None, :]
    return cos * g2, sin * sign * gsw


def _split6(m):
    return jnp.split(m, 6, axis=-1)


def kernel(x, c, ctx, c_ctx, ada_w, ada_b, norm1_g, norm2_g, final_g, ev_w_in, ev_q_gain, ev_k_gain,
           ev_conv_w, ev_conv_b, ev_ln_g, ev_ln_b, ev_w_out, od_w_in, od_short_w, od_short_b,
           od_filt_w1, od_filt_b1, od_filt_w2, od_filt_b2, od_filt_w3, od_filt_freq, od_hyena_skip,
           od_lower_bound, od_gnorm_g, od_w_out, ffn_w_up, ffn_conv_w, ffn_conv_b, ffn_w_down):
    B, L, D = x.shape
    Lc = ctx.shape[1]
    depth = ada_w.shape[0]
    assert depth == 2, "layer 0 (attention + conformer) followed by layer 1 (hyena + hgrn2)"

    lb_soft = jax.nn.softmax(od_lower_bound.astype(F32), axis=0)
    lower_bounds = jnp.cumsum(lb_soft, axis=0) - lb_soft[0:1]

    cond = jnp.concatenate([c, c_ctx[None, :], jnp.zeros((7, D), F32)], axis=0)

    def mods(l):
        m = ada_modulation(cond, ada_w[l], ada_b[l])
        ml = [t[:, None, :] for t in _split6(m[:B])]
        mc = [jnp.broadcast_to(t[None], (B, 1, D)) for t in _split6(m[B:B + 1])]
        return ml, mc

    def ffn(xx, mm, l, final=None):
        return conv_ffn_residual(xx, norm2_g[l] * (1.0 + mm[4]), mm[3], w_up[l],
                                 ffn_conv_w[l], ffn_conv_b[l], w_down[l], mm[5], final)

    w_up = [layer_weight_bf16(ffn_w_up, l) for l in range(depth)]
    w_down = [layer_weight_bf16(ffn_w_down, l) for l in range(depth)]

    ml, mc = mods(0)
    w_in = layer_weight_bf16(ev_w_in, 0)
    seg = jnp.kron(jnp.eye(256 // HEAD_DIM, dtype=F32), jnp.ones((HEAD_DIM, HEAD_DIM), F32)).astype(BF16)
    scale = HEAD_DIM ** -0.5
    cq_l, sq_l = _rope_tables(L, ev_q_gain[0], scale, True)
    ck_l, sk_l = _rope_tables(L, ev_k_gain[0], 1.0, True)
    cq_c, sq_c = _rope_tables(Lc, ev_q_gain[0], scale, False)
    ck_c, sk_c = _rope_tables(Lc, ev_k_gain[0], 1.0, False)
    q_l, *kv_l, glu_l = even_in_proj(x, norm1_g[0] * (1.0 + ml[1]), ml[0], w_in, seg,
                                     cq_l, sq_l, ck_l, sk_l)
    q_c, *kv_c, glu_c = even_in_proj(ctx, norm1_g[0] * (1.0 + mc[1]), mc[0], w_in, seg,
                                     cq_c, sq_c, ck_c, sk_c)
    attn_l = attention(q_l, [kv_c, kv_l])
    attn_c = attention(q_c, [kv_c])
    conv_l = dwconv(glu_l, ev_conv_w[0], ev_conv_b[0], ev_ln_g[0], ev_ln_b[0], out_dtype=BF16)
    conv_c = dwconv(glu_c, ev_conv_w[0], ev_conv_b[0], ev_ln_g[0], ev_ln_b[0], out_dtype=BF16)
    wo = layer_weight_bf16(ev_w_out, 0)
    wo1 = wo[:Q_W]
    wo2 = wo[Q_W:]
    x = out_proj_residual(x, attn_l, conv_l, wo1, wo2, ml[2])
    ctx = out_proj_residual(ctx, attn_c, conv_c, wo1, wo2, mc[2])
    x = ffn(x, ml, 0)
    ctx = ffn(ctx, mc, 0)

    ml, mc = mods(1)
    w_in = layer_weight_bf16(od_w_in, 0)
    p_lat = odd_in_proj(x, norm1_g[1] * (1.0 + ml[1]), ml[0], w_in, od_short_w[0], od_short_b[0])
    p_ctx = norm_mod_matmul(ctx, norm1_g[1] * (1.0 + mc[1]), mc[0],
                            layer_weight_bf16(od_w_in, 0, HY_IN))
    d_lat = hgrn2_mixer(p_lat, HY_IN, p_ctx, 0, lower_bounds[1], od_gnorm_g[0])

    hs, hd, nyq = hyena_filter_taps(L, od_filt_w1[0], od_filt_b1[0], od_filt_w2[0], od_filt_b2[0],
                                    od_filt_w3[0], od_filt_freq[0])
    F, G = _dft_tables(L)
    A, Bt = hyena_spectrum(F, hs, hd, nyq)
    z1 = hyena_conv(p_lat, 0, p_lat, 1, F, G, A, Bt, 0, od_hyena_skip[0][0], F32)
    c_lat = hyena_conv(z1, 0, p_lat, 2, F, G, A, Bt, 1, od_hyena_skip[0][1], BF16)

    wo = layer_weight_bf16(od_w_out, 0)
    x = out_proj_residual(x, c_lat, d_lat, wo[:HY_DIM], wo[HY_DIM:], ml[2])
    return ffn(x, ml, 1, final_g)
```

```python
import functools
import math

import jax
import jax.numpy as jnp
from jax import lax
from jax.experimental import pallas as pl
from jax.experimental.pallas import tpu as pltpu

F32 = jnp.float32
BF16 = jnp.bfloat16

NORM_EPS = 1e-6
GRID_W = 64
ROPE_THETA = 10000.0
HEAD_DIM = 64
N_Q_HEADS = 12
N_KV_HEADS = 4
Q_W = N_Q_HEADS * HEAD_DIM
KV_W = N_KV_HEADS * HEAD_DIM
CONF_DIM = 512
HY_DIM = 512
HY_IN = 3 * HY_DIM
HG_HEADS = 4
HG_DIM = 128
HG_W = HG_HEADS * HG_DIM
HG_CHUNK = 32
C_FAST_DECAY = 0.3
C_SLOW_DECAY = 1.5
C_DECAY_TARGET = 1e-2
V7X_VMEM_LIMIT = 56 * 1024 * 1024
LANES = 128


def _cparams(sem):
    return pltpu.CompilerParams(dimension_semantics=sem, vmem_limit_bytes=V7X_VMEM_LIMIT)


def _tile(n, prefs):
    for p in prefs:
        if n % p == 0:
            return p
    return n


def _norm_mod(x, gs, sh):
    r = lax.rsqrt(jnp.mean(x * x, axis=-1, keepdims=True) + NORM_EPS)
    return x * r * gs + sh


def _cast_kernel(w_ref, o_ref):
    o_ref[...] = w_ref[0].astype(o_ref.dtype)


def layer_weight_bf16(w, layer, col0=0):
    _, R, C = w.shape
    tr = _tile(R, (256, 128, 64, 32, 16))
    tc = math.gcd(C, col0) if col0 else C
    return pl.pallas_call(
        _cast_kernel,
        grid=(R // tr, (C - col0) // tc),
        in_specs=[pl.BlockSpec((1, tr, tc), lambda i, j: (layer, i, j + col0 // tc))],
        out_specs=pl.BlockSpec((tr, tc), lambda i, j: (i, j)),
        out_shape=jax.ShapeDtypeStruct((R, C - col0), BF16),
        compiler_params=_cparams(("parallel", "parallel")),
        name="layer_weight_bf16",
    )(w)


def _ada_kernel(c_ref, w_ref, b_ref, o_ref):
    c = c_ref[...]
    s = (c * jax.nn.sigmoid(c)).astype(BF16)
    o_ref[...] = jnp.dot(s, w_ref[...].astype(BF16), preferred_element_type=F32) + b_ref[...]


def ada_modulation(cs, w, b):
    R, D = cs.shape
    N = w.shape[1]
    tn = _tile(N, (1024, 512, 256, 128))
    return pl.pallas_call(
        _ada_kernel,
        grid=(N // tn,),
        in_specs=[pl.BlockSpec((R, D), lambda j: (0, 0)),
                  pl.BlockSpec((D, tn), lambda j: (0, j)),
                  pl.BlockSpec((1, tn), lambda j: (0, j))],
        out_specs=pl.BlockSpec((R, tn), lambda j: (0, j)),
        out_shape=jax.ShapeDtypeStruct((R, N), F32),
        compiler_params=_cparams(("arbitrary",)),
        name="ada_modulation",
    )(cs, w, b.reshape(1, N))


def _nm_mm_kernel(x_ref, gs_ref, sh_ref, w_ref, o_ref, *, cn):
    h = _norm_mod(x_ref[0], gs_ref[0], sh_ref[0]).astype(BF16)
    for c0 in range(0, w_ref.shape[1], cn):
        o_ref[0, :, c0:c0 + cn] = jnp.dot(h, w_ref[:, c0:c0 + cn],
                                          preferred_element_type=F32).astype(o_ref.dtype)


def norm_mod_matmul(x, gs, sh, w, out_dtype=F32):
    B, L, D = x.shape
    N = w.shape[1]
    tm = _tile(L, (512, 256, 128))
    cn = _tile(N, (512, 256, 128))
    return pl.pallas_call(
        functools.partial(_nm_mm_kernel, cn=cn),
        grid=(B, L // tm),
        in_specs=[pl.BlockSpec((1, tm, D), lambda b, i: (b, i, 0)),
                  pl.BlockSpec((1, 1, D), lambda b, i: (b, 0, 0)),
                  pl.BlockSpec((1, 1, D), lambda b, i: (b, 0, 0)),
                  pl.BlockSpec((D, N), lambda b, i: (0, 0), pipeline_mode=pl.Buffered(1))],
        out_specs=pl.BlockSpec((1, tm, N), lambda b, i: (b, i, 0)),
        out_shape=jax.ShapeDtypeStruct((B, L, N), out_dtype),
        compiler_params=_cparams(("parallel", "parallel")),
        name="norm_mod_matmul",
    )(x, gs, sh, w)


def _odd_in_kernel(x_ref, xp_ref, xn_ref, gs_ref, sh_ref, w_ref, cw_ref, cb_ref, o_ref, *, cn, conv_cols):
    tm = x_ref.shape[1]
    i = pl.program_id(1)
    has_prev = (i > 0).astype(F32)
    has_next = (i < pl.num_programs(1) - 1).astype(F32)
    gs = gs_ref[0]
    sh = sh_ref[0]
    h = _norm_mod(x_ref[0], gs, sh).astype(BF16)
    halo = jnp.concatenate([_norm_mod(xp_ref[0], gs, sh), _norm_mod(xn_ref[0], gs, sh)],
                           axis=0).astype(BF16)
    row = lax.broadcasted_iota(jnp.int32, (tm, cn), 0)
    first = row == 0
    last = row == tm - 1
    hcat = jnp.concatenate([h, halo], axis=0)

    def proj(c0):
        lhs = hcat if c0 < conv_cols else h
        return jnp.dot(lhs, w_ref[:, c0:c0 + cn], preferred_element_type=F32)

    starts = list(range(0, w_ref.shape[1], cn))
    r = proj(starts[0])
    for idx, c0 in enumerate(starts):
        r_next = proj(starts[idx + 1]) if idx + 1 < len(starts) else None
        if c0 < conv_cols:
            p = r[:tm]
            pm1 = jnp.where(first, r[tm + 7:tm + 8, :] * has_prev, pltpu.roll(p, 1, 0))
            pp1 = jnp.where(last, r[tm + 8:tm + 9, :] * has_next, pltpu.roll(p, tm - 1, 0))
            r = (cw_ref[0:1, c0:c0 + cn] * pm1 + cw_ref[1:2, c0:c0 + cn] * p
                 + cw_ref[2:3, c0:c0 + cn] * pp1 + cb_ref[:, c0:c0 + cn])
        o_ref[0, :, c0:c0 + cn] = r
        r = r_next


def odd_in_proj(x, gs, sh, w, cw, cb):
    B, L, D = x.shape
    N = w.shape[1]
    conv_cols = cw.shape[1]
    tm = _tile(L, (1024, 512, 256, 128))
    cn = _tile(math.gcd(N, conv_cols), (512, 256, 128))
    nb8 = L // 8
    t8 = tm // 8
    const = lambda b, i: (0, 0)
    return pl.pallas_call(
        functools.partial(_odd_in_kernel, cn=cn, conv_cols=conv_cols),
        grid=(B, L // tm),
        in_specs=[pl.BlockSpec((1, tm, D), lambda b, i: (b, i, 0)),
                  pl.BlockSpec((1, 8, D), lambda b, i: (b, jnp.maximum(i * t8 - 1, 0), 0)),
                  pl.BlockSpec((1, 8, D), lambda b, i: (b, jnp.minimum((i + 1) * t8, nb8 - 1), 0)),
                  pl.BlockSpec((1, 1, D), lambda b, i: (b, 0, 0)),
                  pl.BlockSpec((1, 1, D), lambda b, i: (b, 0, 0)),
                  pl.BlockSpec((D, N), const, pipeline_mode=pl.Buffered(1)),
                  pl.BlockSpec((3, conv_cols), const),
                  pl.BlockSpec((1, conv_cols), const)],
        out_specs=pl.BlockSpec((1, tm, N), lambda b, i: (b, i, 0)),
        out_shape=jax.ShapeDtypeStruct((B, L, N), F32),
        compiler_params=_cparams(("parallel", "arbitrary")),
        name="odd_in_proj",
    )(x, x, x, gs, sh, w, cw, cb.reshape(1, conv_cols))


def _even_in_kernel(x_ref, gs_ref, sh_ref, w_ref, seg_ref, cq_ref, sq_ref, ck_ref, sk_ref,
                    q_ref, klo_ref, khi_ref, vlo_ref, vhi_ref, glu_ref):
    tm = x_ref.shape[1]
    h = _norm_mod(x_ref[0], gs_ref[0], sh_ref[0]).astype(BF16)
    seg = seg_ref[...]
    lane = lax.broadcasted_iota(jnp.int32, (tm, LANES), 1)
    even = (lane & 1) == 0
    low = lane < HEAD_DIM

    def proj(col0, width):
        return jnp.dot(h, w_ref[:, col0:col0 + width], preferred_element_type=F32)

    def head_norm_rope(y, c_ref, s_ref):
        sq = y * y
        hi = sq.astype(BF16)
        lo = (sq - hi.astype(F32)).astype(BF16)
        ss = (jnp.dot(hi, seg, preferred_element_type=F32)
              + jnp.dot(lo, seg, preferred_element_type=F32))
        yn = y * lax.rsqrt(ss * (1.0 / HEAD_DIM) + NORM_EPS)
        out = []
        for j in range(2):
            c = yn[:, LANES * j:LANES * (j + 1)]
            sw = jnp.where(even, pltpu.roll(c, LANES - 1, 1), pltpu.roll(c, 1, 1))
            out.append(c * c_ref[...] + sw * s_ref[...])
        return out

    def put_padded(c, j, lo_ref, hi_ref):
        r = pltpu.roll(c, HEAD_DIM, 1)
        dt = lo_ref.dtype
        lo_ref[0, :, LANES * 2 * j:LANES * (2 * j + 1)] = jnp.where(low, c, 0.0).astype(dt)
        hi_ref[0, :, LANES * 2 * j:LANES * (2 * j + 1)] = jnp.where(low, 0.0, r).astype(dt)
        lo_ref[0, :, LANES * (2 * j + 1):LANES * (2 * j + 2)] = jnp.where(low, r, 0.0).astype(dt)
        hi_ref[0, :, LANES * (2 * j + 1):LANES * (2 * j + 2)] = jnp.where(low, 0.0, c).astype(dt)

    n_q = Q_W // 256
    n_k = KV_W // 256
    g0 = Q_W + 2 * KV_W
    y = proj(0, 256)
    for t in range(n_q + n_k):
        y_next = proj(256 * (t + 1), 256) if t + 1 < n_q + n_k else None
        if t < n_q:
            for j, c in enumerate(head_norm_rope(y, cq_ref, sq_ref)):
                q_ref[0, :, 256 * t + LANES * j:256 * t + LANES * (j + 1)] = c.astype(q_ref.dtype)
        else:
            for j, c in enumerate(head_norm_rope(y, ck_ref, sk_ref)):
                put_padded(c, 2 * (t - n_q) + j, klo_ref, khi_ref)
        y = y_next
    v = proj(Q_W + KV_W, KV_W)
    for j in range(KV_W // LANES):
        put_padded(v[:, LANES * j:LANES * (j + 1)], j, vlo_ref, vhi_ref)
    a = proj(g0, CONF_DIM)
    gate = proj(g0 + CONF_DIM, CONF_DIM)
    glu_ref[0] = a * jax.nn.sigmoid(gate)


def even_in_proj(x, gs, sh, w, seg, cq, sq, ck, sk):
    B, L, D = x.shape
    N = w.shape[1]
    tm = _tile(L, (1024, 512, 256, 128))
    row = lambda b, i: (b, i, 0)
    tab = pl.BlockSpec((tm, LANES), lambda b, i: (i, 0))
    KP = N_KV_HEADS * LANES
    kv_spec = pl.BlockSpec((1, tm, KP), row)
    kv_shape = jax.ShapeDtypeStruct((B, L, KP), BF16)
    return pl.pallas_call(
        _even_in_kernel,
        grid=(B, L // tm),
        in_specs=[pl.BlockSpec((1, tm, D), row),
                  pl.BlockSpec((1, 1, D), lambda b, i: (b, 0, 0)),
                  pl.BlockSpec((1, 1, D), lambda b, i: (b, 0, 0)),
                  pl.BlockSpec((D, N), lambda b, i: (0, 0), pipeline_mode=pl.Buffered(1)),
                  pl.BlockSpec((256, 256), lambda b, i: (0, 0)),
                  tab, tab, tab, tab],
        out_specs=[pl.BlockSpec((1, tm, Q_W), row), kv_spec, kv_spec, kv_spec, kv_spec,
                   pl.BlockSpec((1, tm, CONF_DIM), row)],
        out_shape=[jax.ShapeDtypeStruct((B, L, Q_W), BF16), kv_shape, kv_shape, kv_shape, kv_shape,
                   jax.ShapeDtypeStruct((B, L, CONF_DIM), F32)],
        compiler_params=_cparams(("parallel", "parallel")),
        name="even_in_proj",
    )(x, gs, sh, w, seg, cq, sq, ck, sk)


def _attn_kernel(q_ref, *refs, n_seg):
    o_ref = refs[4 * n_seg]
    group = N_Q_HEADS // N_KV_HEADS
    nt = (((1,), (1,)), ((), ()))

    def scores(hq):
        pr, pos = divmod(hq, 2)
        q2 = q_ref[0, :, LANES * pr:LANES * (pr + 1)]
        g = hq // group
        cols = slice(LANES * g, LANES * (g + 1))
        return [lax.dot_general(q2, refs[4 * t + pos][0, :, cols], nt, preferred_element_type=F32)
                for t in range(n_seg)]

    ss = scores(0)
    acc = None
    for hq in range(N_Q_HEADS):
        nxt = scores(hq + 1) if hq + 1 < N_Q_HEADS else None
        pr, pos = divmod(hq, 2)
        g = hq // group
        cols = slice(LANES * g, LANES * (g + 1))
        m = functools.reduce(jnp.maximum, [jnp.max(s, axis=-1, keepdims=True) for s in ss])
        ps = [jnp.exp(s - m) for s in ss]
        l = functools.reduce(jnp.add, [jnp.sum(p, axis=-1, keepdims=True) for p in ps])
        o = functools.reduce(jnp.add, [
            jnp.dot(p.astype(BF16), refs[4 * t + 2 + pos][0, :, cols], preferred_element_type=F32)
            for t, p in enumerate(ps)])
        o = o / l
        if pos == 0:
            acc = o
        else:
            o_ref[0, :, LANES * pr:LANES * (pr + 1)] = (acc + o).astype(o_ref.dtype)
        ss = nxt


def attention(q, segments):
    B, Lq, _ = q.shape
    tq = _tile(Lq, (256, 128))
    KP = N_KV_HEADS * LANES
    specs, args = [], []
    for seg in segments:
        for t in seg:
            specs.append(pl.BlockSpec((1, t.shape[1], KP), lambda b, i: (b, 0, 0)))
            args.append(t)
    return pl.pallas_call(
        functools.partial(_attn_kernel, n_seg=len(segments)),
        grid=(B, Lq // tq),
        in_specs=[pl.BlockSpec((1, tq, Q_W), lambda b, i: (b, i, 0))] + specs,
        out_specs=pl.BlockSpec((1, tq, Q_W), lambda b, i: (b, i, 0)),
        out_shape=jax.ShapeDtypeStruct((B, Lq, Q_W), BF16),
        compiler_params=_cparams(("parallel", "parallel")),
        name="attention",
    )(q, *args)


_CONV_PAD = 16


def _dwconv_kernel(x_ref, w_ref, b_ref, g_ref, beta_ref, o_ref, xp_ref, *, K, TR, ln_swish):
    L = x_ref.shape[1]
    C = x_ref.shape[2]
    half = (K - 1) // 2
    xp_ref[0:_CONV_PAD, :] = jnp.zeros((_CONV_PAD, C), F32)
    xp_ref[_CONV_PAD + L:, :] = jnp.zeros((_CONV_PAD, C), F32)
    xp_ref[_CONV_PAD:_CONV_PAD + L, :] = x_ref[0]

    groups = {}
    for k in range(K):
        q8, r8 = divmod(_CONV_PAD - half + k, 8)
        groups.setdefault(r8, []).append((k, q8))
    R = TR + 8
    CW = LANES if C % LANES == 0 else C

    def tile(t, carry):
        r0 = pl.multiple_of(t * TR, TR)
        parts = []
        for c0 in range(0, C, CW):
            acc = None
            for r8, taps in sorted(groups.items()):
                part = None
                for k, q8 in taps:
                    term = (xp_ref[pl.ds(pl.multiple_of(r0 + 8 * q8, 8), R), c0:c0 + CW]
                            * w_ref[k:k + 1, c0:c0 + CW])
                    part = term if part is None else part + term
                if r8:
                    part = pltpu.roll(part, R - r8, 0)
                part = part[:TR]
                acc = part if acc is None else acc + part
            parts.append(acc + b_ref[:, c0:c0 + CW])
        acc = parts[0] if len(parts) == 1 else jnp.concatenate(parts, axis=-1)
        if ln_swish:
            mu = jnp.mean(acc, axis=-1, keepdims=True)
            d = acc - mu
            var = jnp.mean(d * d, axis=-1, keepdims=True)
            y = d * lax.rsqrt(var + NORM_EPS) * g_ref[...] + beta_ref[...]
            acc = y * jax.nn.sigmoid(y)
        o_ref[0, pl.ds(r0, TR), :] = acc.astype(o_ref.dtype)
        return carry

    lax.fori_loop(0, L // TR, tile, 0)


def dwconv(x, w, b, g=None, beta=None, *, out_dtype=F32):
    B, L, _ = x.shape
    K, C = w.shape
    ln_swish = g is not None
    tc = C if ln_swish else _tile(C, (512, 256, 128))
    TR = _tile(L, (256, 128, 64, 32, 16, 8))
    if g is None:
        g = jnp.ones((C,), F32)
        beta = jnp.zeros((C,), F32)
    vec = pl.BlockSpec((1, tc), lambda bb, j: (0, j))
    return pl.pallas_call(
        functools.partial(_dwconv_kernel, K=K, TR=TR, ln_swish=ln_swish),
        grid=(B, C // tc),
        in_specs=[pl.BlockSpec((1, L, tc), lambda bb, j: (bb, 0, j)),
                  pl.BlockSpec((K, tc), lambda bb, j: (0, j)),
                  vec, vec, vec],
        out_specs=pl.BlockSpec((1, L, tc), lambda bb, j: (bb, 0, j)),
        out_shape=jax.ShapeDtypeStruct((B, L, C), out_dtype),
        scratch_shapes=[pltpu.VMEM((L + 2 * _CONV_PAD, tc), F32)],
        compiler_params=_cparams(("parallel", "parallel")),
        name="dwconv%d" % K,
    )(x, w, b.reshape(1, C), g.reshape(1, C), beta.reshape(1, C))


def _out_proj_kernel(x_ref, a1_ref, a2_ref, w1_ref, w2_ref, g_ref, o_ref):
    y = jnp.dot(a1_ref[0], w1_ref[...], preferred_element_type=F32)
    y = y + jnp.dot(a2_ref[0], w2_ref[...], preferred_element_type=F32)
    o_ref[0] = x_ref[0] + g_ref[0] * y


def out_proj_residual(x, a1, a2, w1, w2, gate):
    B, L, D = x.shape
    K1, K2 = a1.shape[2], a2.shape[2]
    tm = _tile(L, (2048, 1024, 512, 256, 128))
    row = lambda b, i: (b, i, 0)
    return pl.pallas_call(
        _out_proj_kernel,
        grid=(B, L // tm),
        in_specs=[pl.BlockSpec((1, tm, D), row),
                  pl.BlockSpec((1, tm, K1), row),
                  pl.BlockSpec((1, tm, K2), row),
                  pl.BlockSpec((K1, D), lambda b, i: (0, 0)),
                  pl.BlockSpec((K2, D), lambda b, i: (0, 0)),
                  pl.BlockSpec((1, 1, D), lambda b, i: (b, 0, 0))],
        out_specs=pl.BlockSpec((1, tm, D), row),
        out_shape=jax.ShapeDtypeStruct((B, L, D), F32),
        compiler_params=_cparams(("parallel", "parallel")),
        name="out_proj_residual",
    )(x, a1, a2, w1, w2, gate)


_FFN_CHUNK = 256


def _ffn_kernel(x_ref, xp_ref, xn_ref, gs_ref, sh_ref, wup_ref, cw_ref, cb_ref, wdn_ref, g5_ref,
                fg_ref, o_ref, acc_ref, *, final_norm):
    tm = x_ref.shape[1]
    FF = wdn_ref.shape[0]
    i = pl.program_id(1)
    has_prev = (i > 0).astype(F32)
    has_next = (i < pl.num_programs(1) - 1).astype(F32)
    gs = gs_ref[0]
    sh = sh_ref[0]
    x = x_ref[0]
    h = _norm_mod(x, gs, sh).astype(BF16)
    halo = jnp.concatenate([_norm_mod(xp_ref[0], gs, sh), _norm_mod(xn_ref[0], gs, sh)],
                           axis=0).astype(BF16)
    row = lax.broadcasted_iota(jnp.int32, (tm, _FFN_CHUNK), 0)
    first = row == 0
    last = row == tm - 1

    hcat = jnp.concatenate([h, halo], axis=0)

    def up(c0):
        r = jnp.dot(hcat, wup_ref[:, c0:c0 + _FFN_CHUNK], preferred_element_type=F32)
        return r[:tm], r[tm:]

    def conv3(c0, p, ph):
        prev_row = ph[7:8, :] * has_prev
        next_row = ph[8:9, :] * has_next
        pm1 = jnp.where(first, prev_row, pltpu.roll(p, 1, 0))
        pp1 = jnp.where(last, next_row, pltpu.roll(p, tm - 1, 0))
        return (cw_ref[0:1, c0:c0 + _FFN_CHUNK] * pm1 + cw_ref[1:2, c0:c0 + _FFN_CHUNK] * p
                + cw_ref[2:3, c0:c0 + _FFN_CHUNK] * pp1 + cb_ref[:, c0:c0 + _FFN_CHUNK])

    n_chunks = FF // _FFN_CHUNK
    ahead = 3
    ups = [(up(c * _FFN_CHUNK), up(FF + c * _FFN_CHUNK)) for c in range(min(ahead, n_chunks))]
    for c in range(n_chunks):
        if c + ahead < n_chunks:
            ups.append((up((c + ahead) * _FFN_CHUNK), up(FF + (c + ahead) * _FFN_CHUNK)))
        val = conv3(c * _FFN_CHUNK, *ups[c][0])
        gate = conv3(FF + c * _FFN_CHUNK, *ups[c][1])
        ups[c] = None
        act = (gate * jax.nn.sigmoid(gate) * val).astype(BF16)
        part = jnp.dot(act, wdn_ref[c * _FFN_CHUNK:(c + 1) * _FFN_CHUNK, :],
                       preferred_element_type=F32)
        if c == 0:
            acc_ref[...] = part
        else:
            acc_ref[...] += part
    y = x + g5_ref[0] * acc_ref[...]
    if final_norm:
        y = y * lax.rsqrt(jnp.mean(y * y, axis=-1, keepdims=True) + NORM_EPS) * fg_ref[...]
    o_ref[0] = y


def conv_ffn_residual(x, gs, sh, w_up, cw, cb, w_dn, gate, final_g=None):
    B, L, D = x.shape
    FF = w_dn.shape[0]
    assert FF % _FFN_CHUNK == 0
    tm = _tile(L, (256, 128))
    nb8 = L // 8
    t8 = tm // 8
    final_norm = final_g is not None
    if final_g is None:
        final_g = jnp.ones((D,), F32)
    row = lambda b, i: (b, i, 0)
    const = lambda b, i: (0, 0)
    return pl.pallas_call(
        functools.partial(_ffn_kernel, final_norm=final_norm),
        grid=(B, L // tm),
        in_specs=[pl.BlockSpec((1, tm, D), row),
                  pl.BlockSpec((1, 8, D), lambda b, i: (b, jnp.maximum(i * t8 - 1, 0), 0)),
                  pl.BlockSpec((1, 8, D), lambda b, i: (b, jnp.minimum((i + 1) * t8, nb8 - 1), 0)),
                  pl.BlockSpec((1, 1, D), lambda b, i: (b, 0, 0)),
                  pl.BlockSpec((1, 1, D), lambda b, i: (b, 0, 0)),
                  pl.BlockSpec((D, 2 * FF), const, pipeline_mode=pl.Buffered(1)),
                  pl.BlockSpec((3, 2 * FF), const),
                  pl.BlockSpec((1, 2 * FF), const),
                  pl.BlockSpec((FF, D), const, pipeline_mode=pl.Buffered(1)),
                  pl.BlockSpec((1, 1, D), lambda b, i: (b, 0, 0)),
                  pl.BlockSpec((1, D), const)],
        out_specs=pl.BlockSpec((1, tm, D), row),
        out_shape=jax.ShapeDtypeStruct((B, L, D), F32),
        scratch_shapes=[pltpu.VMEM((tm, D), F32)],
        compiler_params=_cparams(("parallel", "arbitrary")),
        name="conv_ffn",
    )(x, x, x, gs, sh, w_up, cw, cb.reshape(1, 2 * FF), w_dn, gate, final_g.reshape(1, D))


def _hy_filter_kernel(feat_ref, w1_ref, b1_ref, w2_ref, b2_ref, w3_ref, fr_ref, dec_ref,
                      hs_ref, hd_ref, nyq_ref):
    L = feat_ref.shape[0]
    fr = fr_ref[...]
    z = jnp.dot(feat_ref[...].astype(BF16), w1_ref[...].astype(BF16), preferred_element_type=F32)
    hdn = jnp.sin(fr * (z + b1_ref[...]))
    z = jnp.dot(hdn.astype(BF16), w2_ref[...].astype(BF16), preferred_element_type=F32)
    hdn = jnp.sin(fr * (z + b2_ref[...])).astype(BF16)
    row = lax.broadcasted_iota(jnp.int32, (L, HY_DIM), 0)
    not_first = (row > 0).astype(F32)
    sign = jnp.where((row & 1) == 0, 1.0, -1.0).astype(F32)
    dec = dec_ref[...]
    n_order = w3_ref.shape[1] // (2 * HY_DIM)
    for o in range(n_order):
        c0 = o * 2 * HY_DIM
        hf = jnp.dot(hdn, w3_ref[:, c0:c0 + HY_DIM].astype(BF16), preferred_element_type=F32) * dec
        hb = (jnp.dot(hdn, w3_ref[:, c0 + HY_DIM:c0 + 2 * HY_DIM].astype(BF16),
                      preferred_element_type=F32) * dec * not_first)
        inv = 1.0 / (jnp.sum(jnp.abs(hf), axis=0, keepdims=True)
                     + jnp.sum(jnp.abs(hb), axis=0, keepdims=True))
        hs = (hf + hb) * inv
        hd = (hf - hb) * inv
        hs_ref[:, o * HY_DIM:(o + 1) * HY_DIM] = hs
        hd_ref[:, o * HY_DIM:(o + 1) * HY_DIM] = hd
        nyq_ref[:, o * HY_DIM:(o + 1) * HY_DIM] = jnp.sum(hs * sign, axis=0, keepdims=True)


def hyena_filter_taps(L, w1, b1, w2, b2, w3, freq):
    emb = w1.shape[0]
    bands = (emb - 1) // 2
    t = jnp.linspace(0.0, 1.0, L, dtype=F32)[:, None]
    fb = jnp.linspace(1e-4, bands - 1, bands, dtype=F32)
    wpos = (2.0 * math.pi / L) * jnp.arange(L, dtype=F32)[:, None]
    feats = jnp.concatenate([t, jnp.cos(fb * wpos), -jnp.sin(fb * wpos)], axis=-1)
    feats = jnp.pad(feats, ((0, 0), (0, LANES - emb)))
    w1p = jnp.pad(w1, ((0, LANES - emb), (0, 0)))
    deltas = jnp.abs(jnp.linspace(math.log(C_DECAY_TARGET) / C_SLOW_DECAY,
                                  math.log(C_DECAY_TARGET) / C_FAST_DECAY, HY_DIM, dtype=F32))
    dec = jnp.exp(-t * deltas)
    hid = w1.shape[1]
    NO = w3.shape[1] // 2
    full = lambda shape: pl.BlockSpec(shape, lambda: tuple(0 for _ in shape))
    return pl.pallas_call(
        _hy_filter_kernel,
        in_specs=[full((L, LANES)), full((LANES, hid)), full((1, hid)), full((hid, hid)),
                  full((1, hid)), full((hid, 2 * NO)), full((1, hid)), full((L, HY_DIM))],
        out_specs=[full((L, NO)), full((L, NO)), full((1, NO))],
        out_shape=[jax.ShapeDtypeStruct((L, NO), F32), jax.ShapeDtypeStruct((L, NO), F32),
                   jax.ShapeDtypeStruct((1, NO), F32)],
        compiler_params=pltpu.CompilerParams(vmem_limit_bytes=V7X_VMEM_LIMIT),
        name="hyena_filter",
    )(feats, w1p, b1.reshape(1, hid), w2, b2.reshape(1, hid), w3, freq.reshape(1, hid), dec)


_DFT_SPLIT = 64


def _dft_tables(L):
    N = 2 * L
    n = jnp.arange(L, dtype=jnp.int32)

    def cos_sin(mult):
        ang = ((mult[:, None] * n[None, :]) & (N - 1)).astype(F32) * (2.0 * math.pi / N)
        return jnp.cos(ang), jnp.sin(ang)

    ca, sa = cos_sin(jnp.arange(L // _DFT_SPLIT, dtype=jnp.int32) * _DFT_SPLIT)
    cb, sb = cos_sin(jnp.arange(_DFT_SPLIT, dtype=jnp.int32))
    nyq = jnp.where((n & 1) == 0, 1.0, -1.0).astype(F32)
    re = (ca[:, None, :] * cb[None] - sa[:, None, :] * sb[None]).reshape(L, L)
    im = -(sa[:, None, :] * cb[None] + ca[:, None, :] * sb[None]).reshape(L, L)
    im = jnp.where(jnp.arange(L)[:, None] == 0, nyq[None, :], im)
    f = jnp.concatenate([re, im], axis=0).astype(BF16)
    cat, sat, cbt, sbt = ca.T, sa.T, cb.T, sb.T
    re_t = (cat[:, :, None] * cbt[:, None, :] - sat[:, :, None] * sbt[:, None, :]).reshape(L, L)
    im_t = -(sat[:, :, None] * cbt[:, None, :] + cat[:, :, None] * sbt[:, None, :]).reshape(L, L)
    im_t = jnp.where(jnp.arange(L)[None, :] == 0, nyq[:, None], im_t)
    g = jnp.concatenate([re_t, im_t], axis=1).astype(BF16)
    return f, g


def _hy_spec_kernel(fre_ref, fim_ref, hs_ref, hd_ref, nyq_ref, a_ref, b_ref):
    kb = fre_ref.shape[0]
    L = fre_ref.shape[1]
    NO = hs_ref.shape[1]
    krow = lax.broadcasted_iota(jnp.int32, (kb, NO), 0) + pl.program_id(0) * kb
    dc = krow == 0
    scale = jnp.where(dc, 1.0, 2.0).astype(F32) * (1.0 / (2 * L))
    hr = jnp.dot(fre_ref[...], hs_ref[...].astype(BF16), preferred_element_type=F32)
    hi = jnp.dot(fim_ref[...], hd_ref[...].astype(BF16), preferred_element_type=F32)
    a_ref[...] = hr * scale
    b_ref[...] = jnp.where(dc, nyq_ref[...], hi) * scale


def hyena_spectrum(F, hs, hd, nyq):
    L, NO = hs.shape
    kb = _tile(L, (256, 128))
    nk = L // kb
    return pl.pallas_call(
        _hy_spec_kernel,
        grid=(nk,),
        in_specs=[pl.BlockSpec((kb, L), lambda i: (i, 0)),
                  pl.BlockSpec((kb, L), lambda i: (i + nk, 0)),
                  pl.BlockSpec((L, NO), lambda i: (0, 0)),
                  pl.BlockSpec((L, NO), lambda i: (0, 0)),
                  pl.BlockSpec((1, NO), lambda i: (0, 0))],
        out_specs=[pl.BlockSpec((kb, NO), lambda i: (i, 0)),
                   pl.BlockSpec((kb, NO), lambda i: (i, 0))],
        out_shape=[jax.ShapeDtypeStruct((L, NO), F32), jax.ShapeDtypeStruct((L, NO), F32)],
        compiler_params=_cparams(("arbitrary",)),
        name="hyena_spectrum",
    )(F, F, hs, hd, nyq)


def _hy_conv_kernel(z_ref, gate_ref, fre_ref, fim_ref, gre_ref, gim_ref, a_ref, b_ref, skip_ref,
                    o_ref, zb_ref, y_ref):
    kb = fre_ref.shape[0]
    C = z_ref.shape[2]
    kstep = pl.program_id(1)

    @pl.when(kstep == 0)
    def _():
        zb_ref[...] = z_ref[0].astype(BF16)
        y_ref[...] = jnp.zeros(y_ref.shape, F32)

    NH = 2
    hb = kb // NH
    CH = C // 2
    blocks = [(hf, ch) for hf in range(NH) for ch in range(2)]
    fwd = []
    for hf, ch in blocks:
        rows = slice(hf * hb, (hf + 1) * hb)
        zc = zb_ref[:, ch * CH:(ch + 1) * CH]
        fwd.append((jnp.dot(fre_ref[rows, :], zc, preferred_element_type=F32),
                    jnp.dot(fim_ref[rows, :], zc, preferred_element_type=F32)))
    for (hf, ch), (ur, ui) in zip(blocks, fwd):
        rows = slice(hf * hb, (hf + 1) * hb)
        cols = slice(ch * CH, (ch + 1) * CH)
        a = a_ref[rows, cols]
        b = b_ref[rows, cols]
        krow = lax.broadcasted_iota(jnp.int32, (hb, CH), 0) + (kstep * kb + hf * hb)
        dc = krow == 0
        vr = jnp.where(dc, ur * a, ur * a - ui * b)
        vi = jnp.where(dc, ui * b, ur * b + ui * a)
        y_ref[:, cols] += (jnp.dot(gre_ref[:, rows], vr.astype(BF16), preferred_element_type=F32)
                           + jnp.dot(gim_ref[:, rows], vi.astype(BF16), preferred_element_type=F32))

    @pl.when(kstep == pl.num_programs(1) - 1)
    def _():
        o_ref[0] = (gate_ref[0] * (y_ref[...] + skip_ref[...] * z_ref[0])).astype(o_ref.dtype)


def hyena_conv(z_src, z_col, gate_src, gate_col, F, G, A, Bt, spec_col, skip, out_dtype):
    B, L, _ = z_src.shape
    C = HY_DIM
    kb = _tile(L, (512, 256))
    nk = L // kb
    return pl.pallas_call(
        _hy_conv_kernel,
        grid=(B, nk),
        in_specs=[pl.BlockSpec((1, L, C), lambda b, k: (b, 0, z_col)),
                  pl.BlockSpec((1, L, C), lambda b, k: (b, 0, gate_col)),
                  pl.BlockSpec((kb, L), lambda b, k: (k, 0)),
                  pl.BlockSpec((kb, L), lambda b, k: (k + nk, 0)),
                  pl.BlockSpec((L, kb), lambda b, k: (0, k)),
                  pl.BlockSpec((L, kb), lambda b, k: (0, k + nk)),
                  pl.BlockSpec((kb, C), lambda b, k: (k, spec_col)),
                  pl.BlockSpec((kb, C), lambda b, k: (k, spec_col)),
                  pl.BlockSpec((1, C), lambda b, k: (0, 0))],
        out_specs=pl.BlockSpec((1, L, C), lambda b, k: (b, 0, 0)),
        out_shape=jax.ShapeDtypeStruct((B, L, C), out_dtype),
        scratch_shapes=[pltpu.VMEM((L, C), BF16), pltpu.VMEM((L, C), F32)],
        compiler_params=_cparams(("parallel", "arbitrary")),
        name="hyena_conv",
    )(z_src, gate_src, F, F, G, G, A, Bt, skip.reshape(1, C))


HG_GROUP = 128


def _hgrn2_kernel(ql_ref, ffl_ref, fbl_ref, il_ref, gl_ref, qc_ref, ffc_ref, fbc_ref, ic_ref,
                  lb_ref, gn_ref, o_ref, vb_ref, qdf_ref, qdb_ref, qgf_ref, qgb_ref,
                  kuf_ref, kub_ref, kgf_ref, kgb_ref, kdf_ref, kdb_ref, decf_ref, decb_ref,
                  oif_ref, oib_ref, stf_ref, stb_ref):
    C = HG_CHUNK
    GR = HG_GROUP
    G2 = 2 * C
    L = ql_ref.shape[1]
    Lc = qc_ref.shape[1]
    r = lax.broadcasted_iota(jnp.int32, (GR, GR), 0)
    c = lax.broadcasted_iota(jnp.int32, (GR, GR), 1)
    same = (r // C) == (c // C)
    same_group = (r // G2) == (c // G2)
    row_odd = ((lax.broadcasted_iota(jnp.int32, (GR, HG_DIM), 0) // C) & 1) == 1
    dirs = []
    for d, (incl, excl) in enumerate(((c <= r, c > r), (c >= r, c < r))):
        mask = same & incl
        cross = same_group & jnp.logical_not(same) & incl
        second = row_odd if d == 0 else jnp.logical_not(row_odd)
        dirs.append((lb_ref[d:d + 1, :], mask.astype(BF16), (same & excl).astype(BF16), mask, cross,
                     second))
    f_refs_lat = (ffl_ref, fbl_ref)
    f_refs_ctx = (ffc_ref, fbc_ref)
    qd_refs = (qdf_ref, qdb_ref)
    qg_refs = (qgf_ref, qgb_ref)
    ku_refs = (kuf_ref, kub_ref)
    kg_refs = (kgf_ref, kgb_ref)
    kd_refs = (kdf_ref, kdb_ref)
    dec_refs = (decf_ref, decb_ref)
    oi_refs = (oif_ref, oib_ref)
    st_refs = (stf_ref, stb_ref)
    nt = (((1,), (1,)), ((), ()))

    def gates(q_ref, f_refs, i_ref, n_rows, row_off, with_out):
        def body(gi, carry):
            r0 = pl.multiple_of(gi * GR, GR)
            ro = pl.multiple_of(row_off + gi * GR, GR)
            q = q_ref[0, pl.ds(r0, GR), :]
            vb_ref[pl.ds(ro, GR), :] = i_ref[0, pl.ds(r0, GR), :].astype(BF16)
            for d, (lbd, t_in, t_ex, _, _, second) in enumerate(dirs):
                f = lbd + (1.0 - lbd) * jax.nn.sigmoid(f_refs[d][0, pl.ds(r0, GR), :])
                k = 1.0 - f
                lg = jnp.log(f)
                hi = lg.astype(BF16)
                lo = (lg - hi.astype(F32)).astype(BF16)
                b = (jnp.dot(t_in, hi, preferred_element_type=F32)
                     + jnp.dot(t_in, lo, preferred_element_type=F32))
                rest = (jnp.dot(t_ex, hi, preferred_element_type=F32)
                        + jnp.dot(t_ex, lo, preferred_element_type=F32))
                tot = b + rest
                other = jnp.where(row_odd, pltpu.roll(tot, C, 0), pltpu.roll(tot, GR - C, 0))
                qd_refs[d][pl.ds(ro, GR), :] = (q * jnp.exp(b)).astype(BF16)
                qg_refs[d][pl.ds(ro, GR), :] = (
                    q * jnp.exp(b + jnp.where(second, other, 0.0))).astype(BF16)
                ku_refs[d][pl.ds(ro, GR), :] = (k * jnp.exp(rest)).astype(BF16)
                kg_refs[d][pl.ds(ro, GR), :] = (
                    k * jnp.exp(rest + jnp.where(second, 0.0, other))).astype(BF16)
                dec_refs[d][pl.ds(ro, GR), :] = tot + other
                if with_out:
                    kd_refs[d][pl.ds(r0, GR), :] = (k * jnp.exp(-b)).astype(BF16)
            return carry

        lax.fori_loop(0, n_rows // GR, body, 0, unroll=2)

    def intra():
        U = 4 if (L // GR) % 4 == 0 else 1

        def body(gi, carry):
            jobs = []
            for u in range(U):
                r0 = pl.multiple_of((gi * U + u) * GR, GR)
                ro = pl.multiple_of(Lc + (gi * U + u) * GR, GR)
                for d in range(2):
                    qd = qd_refs[d][pl.ds(ro, GR), :]
                    sc = lax.dot_general(qd, kd_refs[d][pl.ds(r0, GR), :], nt,
                                         preferred_element_type=F32)
                    sx = lax.dot_general(qd, ku_refs[d][pl.ds(ro, GR), :], nt,
                                         preferred_element_type=F32)
                    jobs.append((d, r0, ro, sc, sx))
            for d, r0, ro, sc, sx in jobs:
                sc = jnp.where(dirs[d][3], sc, jnp.where(dirs[d][4], sx, 0.0)).astype(BF16)
                oi_refs[d][pl.ds(r0, GR), :] = jnp.dot(sc, vb_ref[pl.ds(ro, GR), :],
                                                       preferred_element_type=F32)
            return carry

        lax.fori_loop(0, L // (GR * U), body, 0)

    def recurrence(n_rows, row_off, keep, carry):
        n = n_rows // G2

        def body(j, states):
            new = []
            for d, st in enumerate(states):
                jj = j if d == 0 else n - 1 - j
                ro = pl.multiple_of(row_off + jj * G2, G2)
                if keep:
                    st_refs[d][pl.ds(pl.multiple_of(jj * HG_DIM, HG_DIM), HG_DIM), :] = st.astype(BF16)
                upd = lax.dot_general(vb_ref[pl.ds(ro, G2), :], kg_refs[d][pl.ds(ro, G2), :],
                                      (((0,), (0,)), ((), ())), preferred_element_type=F32)
                new.append(st * jnp.exp(dec_refs[d][pl.ds(ro, 1), :]) + upd)
            return tuple(new)

        return lax.fori_loop(0, n, body, carry, unroll=8)

    def inter():
        def body(j, carry):
            r0 = pl.multiple_of(j * G2, G2)
            ro = pl.multiple_of(Lc + j * G2, G2)
            so = pl.multiple_of(j * HG_DIM, HG_DIM)
            for d in range(2):
                o = lax.dot_general(qg_refs[d][pl.ds(ro, G2), :], st_refs[d][pl.ds(so, HG_DIM), :], nt,
                                    preferred_element_type=F32)
                oi_refs[d][pl.ds(r0, G2), :] += o
            return carry

        lax.fori_loop(0, L // G2, body, 0, unroll=8)

    gates(qc_ref, f_refs_ctx, ic_ref, Lc, 0, False)
    gates(ql_ref, f_refs_lat, il_ref, L, Lc, True)
    intra()
    zero = jnp.zeros((HG_DIM, HG_DIM), F32)
    carry = recurrence(Lc, 0, False, (zero, zero))
    recurrence(L, Lc, True, carry)
    inter()
    o = oif_ref[...] + oib_ref[...]
    o = o * lax.rsqrt(jnp.mean(o * o, axis=-1, keepdims=True) + NORM_EPS) * gn_ref[...]
    g = gl_ref[0]
    o_ref[0] = (o * (g * jax.nn.sigmoid(g))).astype(o_ref.dtype)


def hgrn2_mixer(p_lat, lat_col0, p_ctx, ctx_col0, lb, gn_g):
    B, L, _ = p_lat.shape
    Lc = p_ctx.shape[1]
    lc0 = lat_col0 // HG_DIM
    cc0 = ctx_col0 // HG_DIM

    def lat(seg):
        return pl.BlockSpec((1, L, HG_DIM), lambda b, h: (b, 0, lc0 + HG_HEADS * seg + h))

    def ctx(seg):
        return pl.BlockSpec((1, Lc, HG_DIM), lambda b, h: (b, 0, cc0 + HG_HEADS * seg + h))

    return pl.pallas_call(
        _hgrn2_kernel,
        grid=(B, HG_HEADS),
        in_specs=[lat(0), lat(1), lat(2), lat(3), lat(4), ctx(0), ctx(1), ctx(2), ctx(3),
                  pl.BlockSpec((2, HG_DIM), lambda b, h: (0, h)),
                  pl.BlockSpec((1, HG_DIM), lambda b, h: (0, 0))],
        out_specs=pl.BlockSpec((1, L, HG_DIM), lambda b, h: (b, 0, h)),
        out_shape=jax.ShapeDtypeStruct((B, L, HG_W), BF16),
        scratch_shapes=([pltpu.VMEM((Lc + L, HG_DIM), BF16)] * 9
                        + [pltpu.VMEM((L, HG_DIM), BF16)] * 2
                        + [pltpu.VMEM((Lc + L, HG_DIM), F32)] * 2
                        + [pltpu.VMEM((L, HG_DIM), F32)] * 2
                        + [pltpu.VMEM((L // (2 * HG_CHUNK) * HG_DIM, HG_DIM), BF16)] * 2),
        compiler_params=_cparams(("parallel", "parallel")),
        name="hgrn2",
    )(p_lat, p_lat, p_lat, p_lat, p_lat, p_ctx, p_ctx, p_ctx, p_ctx, lb, gn_g.reshape(1, HG_DIM))


def _rope_tables(L, gain, scale, rotate):
    g2 = jnp.tile(gain.astype(F32), LANES // HEAD_DIM)[None, :] * scale
    gsw = g2.reshape(1, LANES // 2, 2)[:, :, ::-1].reshape(1, LANES)
    if not rotate:
        return jnp.broadcast_to(g2, (L, LANES)), jnp.zeros((L, LANES), F32)
    rows = L // GRID_W
    row = jnp.repeat(jnp.arange(rows, dtype=F32), GRID_W)
    col = jnp.tile(jnp.arange(GRID_W, dtype=F32), rows)
    axis_dim = HEAD_DIM // 2
    inv_freq = ROPE_THETA ** (-jnp.arange(0, axis_dim, 2, dtype=F32) / axis_dim)
    ang = jnp.concatenate([row[:, None] * inv_freq, col[:, None] * inv_freq], axis=-1)
    cos = jnp.tile(jnp.repeat(jnp.cos(ang), 2, axis=-1), (1, LANES // HEAD_DIM))
    sin = jnp.tile(jnp.repeat(jnp.sin(ang), 2, axis=-1), (1, LANES // HEAD_DIM))
    sign = jnp.tile(jnp.array([-1.0, 1.0], F32), LANES // 2)[None, :]
    return cos * g2, sin * sign * gsw


def _split6(m):
    return jnp.split(m, 6, axis=-1)


def kernel(x, c, ctx, c_ctx, ada_w, ada_b, norm1_g, norm2_g, final_g, ev_w_in, ev_q_gain, ev_k_gain,
           ev_conv_w, ev_conv_b, ev_ln_g, ev_ln_b, ev_w_out, od_w_in, od_short_w, od_short_b,
           od_filt_w1, od_filt_b1, od_filt_w2, od_filt_b2, od_filt_w3, od_filt_freq, od_hyena_skip,
           od_lower_bound, od_gnorm_g, od_w_out, ffn_w_up, ffn_conv_w, ffn_conv_b, ffn_w_down):
    B, L, D = x.shape
    Lc = ctx.shape[1]
    depth = ada_w.shape[0]
    assert depth == 2, "layer 0 (attention + conformer) followed by layer 1 (hyena + hgrn2)"

    lb_soft = jax.nn.softmax(od_lower_bound.astype(F32), axis=0)
    lower_bounds = jnp.cumsum(lb_soft, axis=0) - lb_soft[0:1]

    cond = jnp.concatenate([c, c_ctx[None, :], jnp.zeros((7, D), F32)], axis=0)

    def mods(l):
        m = ada_modulation(cond, ada_w[l], ada_b[l])
        ml = [t[:, None, :] for t in _split6(m[:B])]
        mc = [jnp.broadcast_to(t[None], (B, 1, D)) for t in _split6(m[B:B + 1])]
        return ml, mc

    def ffn(xx, mm, l, final=None):
        return conv_ffn_residual(xx, norm2_g[l] * (1.0 + mm[4]), mm[3], w_up[l],
                                 ffn_conv_w[l], ffn_conv_b[l], w_down[l], mm[5], final)

    w_up = [layer_weight_bf16(ffn_w_up, l) for l in range(depth)]
    w_down = [layer_weight_bf16(ffn_w_down, l) for l in range(depth)]

    ml, mc = mods(0)
    w_in = layer_weight_bf16(ev_w_in, 0)
    seg = jnp.kron(jnp.eye(256 // HEAD_DIM, dtype=F32), jnp.ones((HEAD_DIM, HEAD_DIM), F32)).astype(BF16)
    scale = HEAD_DIM ** -0.5
    cq_l, sq_l = _rope_tables(L, ev_q_gain[0], scale, True)
    ck_l, sk_l = _rope_tables(L, ev_k_gain[0], 1.0, True)
    cq_c, sq_c = _rope_tables(Lc, ev_q_gain[0], scale, False)
    ck_c, sk_c = _rope_tables(Lc, ev_k_gain[0], 1.0, False)
    q_l, *kv_l, glu_l = even_in_proj(x, norm1_g[0] * (1.0 + ml[1]), ml[0], w_in, seg,
                                     cq_l, sq_l, ck_l, sk_l)
    q_c, *kv_c, glu_c = even_in_proj(ctx, norm1_g[0] * (1.0 + mc[1]), mc[0], w_in, seg,
                                     cq_c, sq_c, ck_c, sk_c)
    attn_l = attention(q_l, [kv_c, kv_l])
    attn_c = attention(q_c, [kv_c])
    conv_l = dwconv(glu_l, ev_conv_w[0], ev_conv_b[0], ev_ln_g[0], ev_ln_b[0], out_dtype=BF16)
    conv_c = dwconv(glu_c, ev_conv_w[0], ev_conv_b[0], ev_ln_g[0], ev_ln_b[0], out_dtype=BF16)
    wo = layer_weight_bf16(ev_w_out, 0)
    wo1 = wo[:Q_W]
    wo2 = wo[Q_W:]
    x = out_proj_residual(x, attn_l, conv_l, wo1, wo2, ml[2])
    ctx = out_proj_residual(ctx, attn_c, conv_c, wo1, wo2, mc[2])
    x = ffn(x, ml, 0)
    ctx = ffn(ctx, mc, 0)

    ml, mc = mods(1)
    w_in = layer_weight_bf16(od_w_in, 0)
    p_lat = odd_in_proj(x, norm1_g[1] * (1.0 + ml[1]), ml[0], w_in, od_short_w[0], od_short_b[0])
    p_ctx = norm_mod_matmul(ctx, norm1_g[1] * (1.0 + mc[1]), mc[0],
                            layer_weight_bf16(od_w_in, 0, HY_IN))
    d_lat = hgrn2_mixer(p_lat, HY_IN, p_ctx, 0, lower_bounds[1], od_gnorm_g[0])

    hs, hd, nyq = hyena_filter_taps(L, od_filt_w1[0], od_filt_b1[0], od_filt_w2[0], od_filt_b2[0],
                                    od_filt_w3[0], od_filt_freq[0])
    F, G = _dft_tables(L)
    A, Bt = hyena_spectrum(F, hs, hd, nyq)
    z1 = hyena_conv(p_lat, 0, p_lat, 1, F, G, A, Bt, 0, od_hyena_skip[0][0], F32)
    c_lat = hyena_conv(z1, 0, p_lat, 2, F, G, A, Bt, 1, od_hyena_skip[0][1], BF16)

    wo = layer_weight_bf16(od_w_out, 0)
    x = out_proj_residual(x, c_lat, d_lat, wo[:HY_DIM], wo[HY_DIM:], ml[2])
    return ffn(x, ml, 1, final_g)
```

```python
import functools
import math

import jax
import jax.numpy as jnp
from jax import lax
from jax.experimental import pallas as pl
from jax.experimental.pallas import tpu as pltpu

F32 = jnp.float32
BF16 = jnp.bfloat16

NORM_EPS = 1e-6
GRID_W = 64
ROPE_THETA = 10000.0
HEAD_DIM = 64
N_Q_HEADS = 12
N_KV_HEADS = 4
Q_W = N_Q_HEADS * HEAD_DIM
KV_W = N_KV_HEADS * HEAD_DIM
CONF_DIM = 512
HY_DIM = 512
HY_IN = 3 * HY_DIM
HG_HEADS = 4
HG_DIM = 128
HG_W = HG_HEADS * HG_DIM
HG_CHUNK = 32
C_FAST_DECAY = 0.3
C_SLOW_DECAY = 1.5
C_DECAY_TARGET = 1e-2
V7X_VMEM_LIMIT = 56 * 1024 * 1024
LANES = 128


def _cparams(sem):
    return pltpu.CompilerParams(dimension_semantics=sem, vmem_limit_bytes=V7X_VMEM_LIMIT)


def _tile(n, prefs):
    for p in prefs:
        if n % p == 0:
            return p
    return n


def _norm_mod(x, gs, sh):
    r = lax.rsqrt(jnp.mean(x * x, axis=-1, keepdims=True) + NORM_EPS)
    return x * r * gs + sh


def _cast_kernel(w_ref, o_ref):
    o_ref[...] = w_ref[0].astype(o_ref.dtype)


def layer_weight_bf16(w, layer, col0=0):
    _, R, C = w.shape
    tr = _tile(R, (256, 128, 64, 32, 16))
    tc = math.gcd(C, col0) if col0 else C
    return pl.pallas_call(
        _cast_kernel,
        grid=(R // tr, (C - col0) // tc),
        in_specs=[pl.BlockSpec((1, tr, tc), lambda i, j: (layer, i, j + col0 // tc))],
        out_specs=pl.BlockSpec((tr, tc), lambda i, j: (i, j)),
        out_shape=jax.ShapeDtypeStruct((R, C - col0), BF16),
        compiler_params=_cparams(("parallel", "parallel")),
        name="layer_weight_bf16",
    )(w)


def _ada_kernel(c_ref, w_ref, b_ref, o_ref):
    c = c_ref[...]
    s = (c * jax.nn.sigmoid(c)).astype(BF16)
    o_ref[...] = jnp.dot(s, w_ref[...].astype(BF16), preferred_element_type=F32) + b_ref[...]


def ada_modulation(cs, w, b):
    R, D = cs.shape
    N = w.shape[1]
    tn = _tile(N, (1024, 512, 256, 128))
    return pl.pallas_call(
        _ada_kernel,
        grid=(N // tn,),
        in_specs=[pl.BlockSpec((R, D), lambda j: (0, 0)),
                  pl.BlockSpec((D, tn), lambda j: (0, j)),
                  pl.BlockSpec((1, tn), lambda j: (0, j))],
        out_specs=pl.BlockSpec((R, tn), lambda j: (0, j)),
        out_shape=jax.ShapeDtypeStruct((R, N), F32),
        compiler_params=_cparams(("arbitrary",)),
        name="ada_modulation",
    )(cs, w, b.reshape(1, N))


def _nm_mm_kernel(x_ref, gs_ref, sh_ref, w_ref, o_ref, *, cn):
    h = _norm_mod(x_ref[0], gs_ref[0], sh_ref[0]).astype(BF16)
    for c0 in range(0, w_ref.shape[1], cn):
        o_ref[0, :, c0:c0 + cn] = jnp.dot(h, w_ref[:, c0:c0 + cn],
                                          preferred_element_type=F32).astype(o_ref.dtype)


def norm_mod_matmul(x, gs, sh, w, out_dtype=F32):
    B, L, D = x.shape
    N = w.shape[1]
    tm = _tile(L, (512, 256, 128))
    cn = _tile(N, (512, 256, 128))
    return pl.pallas_call(
        functools.partial(_nm_mm_kernel, cn=cn),
        grid=(B, L // tm),
        in_specs=[pl.BlockSpec((1, tm, D), lambda b, i: (b, i, 0)),
                  pl.BlockSpec((1, 1, D), lambda b, i: (b, 0, 0)),
                  pl.BlockSpec((1, 1, D), lambda b, i: (b, 0, 0)),
                  pl.BlockSpec((D, N), lambda b, i: (0, 0), pipeline_mode=pl.Buffered(1))],
        out_specs=pl.BlockSpec((1, tm, N), lambda b, i: (b, i, 0)),
        out_shape=jax.ShapeDtypeStruct((B, L, N), out_dtype),
        compiler_params=_cparams(("parallel", "parallel")),
        name="norm_mod_matmul",
    )(x, gs, sh, w)


def _odd_in_kernel(x_ref, xp_ref, xn_ref, gs_ref, sh_ref, w_ref, cw_ref, cb_ref, o_ref, *, cn, conv_cols):
    tm = x_ref.shape[1]
    i = pl.program_id(1)
    has_prev = (i > 0).astype(F32)
    has_next = (i < pl.num_programs(1) - 1).astype(F32)
    gs = gs_ref[0]
    sh = sh_ref[0]
    h = _norm_mod(x_ref[0], gs, sh).astype(BF16)
    halo = jnp.concatenate([_norm_mod(xp_ref[0], gs, sh), _norm_mod(xn_ref[0], gs, sh)],
                           axis=0).astype(BF16)
    row = lax.broadcasted_iota(jnp.int32, (tm, cn), 0)
    first = row == 0
    last = row == tm - 1
    hcat = jnp.concatenate([h, halo], axis=0)

    def proj(c0):
        lhs = hcat if c0 < conv_cols else h
        return jnp.dot(lhs, w_ref[:, c0:c0 + cn], preferred_element_type=F32)

    starts = list(range(0, w_ref.shape[1], cn))
    r = proj(starts[0])
    for idx, c0 in enumerate(starts):
        r_next = proj(starts[idx + 1]) if idx + 1 < len(starts) else None
        if c0 < conv_cols:
            p = r[:tm]
            pm1 = jnp.where(first, r[tm + 7:tm + 8, :] * has_prev, pltpu.roll(p, 1, 0))
            pp1 = jnp.where(last, r[tm + 8:tm + 9, :] * has_next, pltpu.roll(p, tm - 1, 0))
            r = (cw_ref[0:1, c0:c0 + cn] * pm1 + cw_ref[1:2, c0:c0 + cn] * p
                 + cw_ref[2:3, c0:c0 + cn] * pp1 + cb_ref[:, c0:c0 + cn])
        o_ref[0, :, c0:c0 + cn] = r
        r = r_next


def odd_in_proj(x, gs, sh, w, cw, cb):
    B, L, D = x.shape
    N = w.shape[1]
    conv_cols = cw.shape[1]
    tm = _tile(L, (1024, 512, 256, 128))
    cn = _tile(math.gcd(N, conv_cols), (512, 256, 128))
    nb8 = L // 8
    t8 = tm // 8
    const = lambda b, i: (0, 0)
    return pl.pallas_call(
        functools.partial(_odd_in_kernel, cn=cn, conv_cols=conv_cols),
        grid=(B, L // tm),
        in_specs=[pl.BlockSpec((1, tm, D), lambda b, i: (b, i, 0)),
                  pl.BlockSpec((1, 8, D), lambda b, i: (b, jnp.maximum(i * t8 - 1, 0), 0)),
                  pl.BlockSpec((1, 8, D), lambda b, i: (b, jnp.minimum((i + 1) * t8, nb8 - 1), 0)),
                  pl.BlockSpec((1, 1, D), lambda b, i: (b, 0, 0)),
                  pl.BlockSpec((1, 1, D), lambda b, i: (b, 0, 0)),
                  pl.BlockSpec((D, N), const, pipeline_mode=pl.Buffered(1)),
                  pl.BlockSpec((3, conv_cols), const),
                  pl.BlockSpec((1, conv_cols), const)],
        out_specs=pl.BlockSpec((1, tm, N), lambda b, i: (b, i, 0)),
        out_shape=jax.ShapeDtypeStruct((B, L, N), F32),
        compiler_params=_cparams(("parallel", "arbitrary")),
        name="odd_in_proj",
    )(x, x, x, gs, sh, w, cw, cb.reshape(1, conv_cols))


def _even_in_kernel(x_ref, gs_ref, sh_ref, w_ref, seg_ref, cq_ref, sq_ref, ck_ref, sk_ref,
                    q_ref, klo_ref, khi_ref, vlo_ref, vhi_ref, glu_ref):
    tm = x_ref.shape[1]
    h = _norm_mod(x_ref[0], gs_ref[0], sh_ref[0]).astype(BF16)
    seg = seg_ref[...]
    lane = lax.broadcasted_iota(jnp.int32, (tm, LANES), 1)
    even = (lane & 1) == 0
    low = lane < HEAD_DIM

    def proj(col0, width):
        return jnp.dot(h, w_ref[:, col0:col0 + width], preferred_element_type=F32)

    def head_norm_rope(y, c_ref, s_ref):
        sq = y * y
        hi = sq.astype(BF16)
        lo = (sq - hi.astype(F32)).astype(BF16)
        ss = (jnp.dot(hi, seg, preferred_element_type=F32)
              + jnp.dot(lo, seg, preferred_element_type=F32))
        yn = y * lax.rsqrt(ss * (1.0 / HEAD_DIM) + NORM_EPS)
        out = []
        for j in range(2):
            c = yn[:, LANES * j:LANES * (j + 1)]
            sw = jnp.where(even, pltpu.roll(c, LANES - 1, 1), pltpu.roll(c, 1, 1))
            out.append(c * c_ref[...] + sw * s_ref[...])
        return out

    def put_padded(c, j, lo_ref, hi_ref):
        r = pltpu.roll(c, HEAD_DIM, 1)
        dt = lo_ref.dtype
        lo_ref[0, :, LANES * 2 * j:LANES * (2 * j + 1)] = jnp.where(low, c, 0.0).astype(dt)
        hi_ref[0, :, LANES * 2 * j:LANES * (2 * j + 1)] = jnp.where(low, 0.0, r).astype(dt)
        lo_ref[0, :, LANES * (2 * j + 1):LANES * (2 * j + 2)] = jnp.where(low, r, 0.0).astype(dt)
        hi_ref[0, :, LANES * (2 * j + 1):LANES * (2 * j + 2)] = jnp.where(low, 0.0, c).astype(dt)

    n_q = Q_W // 256
    n_k = KV_W // 256
    g0 = Q_W + 2 * KV_W
    y = proj(0, 256)
    for t in range(n_q + n_k):
        y_next = proj(256 * (t + 1), 256) if t + 1 < n_q + n_k else None
        if t < n_q:
            for j, c in enumerate(head_norm_rope(y, cq_ref, sq_ref)):
                q_ref[0, :, 256 * t + LANES * j:256 * t + LANES * (j + 1)] = c.astype(q_ref.dtype)
        else:
            for j, c in enumerate(head_norm_rope(y, ck_ref, sk_ref)):
                put_padded(c, 2 * (t - n_q) + j, klo_ref, khi_ref)
        y = y_next
    v = proj(Q_W + KV_W, KV_W)
    for j in range(KV_W // LANES):
        put_padded(v[:, LANES * j:LANES * (j + 1)], j, vlo_ref, vhi_ref)
    a = proj(g0, CONF_DIM)
    gate = proj(g0 + CONF_DIM, CONF_DIM)
    glu_ref[0] = a * jax.nn.sigmoid(gate)


def even_in_proj(x, gs, sh, w, seg, cq, sq, ck, sk):
    B, L, D = x.shape
    N = w.shape[1]
    tm = _tile(L, (1024, 512, 256, 128))
    row = lambda b, i: (b, i, 0)
    tab = pl.BlockSpec((tm, LANES), lambda b, i: (i, 0))
    KP = N_KV_HEADS * LANES
    kv_spec = pl.BlockSpec((1, tm, KP), row)
    kv_shape = jax.ShapeDtypeStruct((B, L, KP), BF16)
    return pl.pallas_call(
        _even_in_kernel,
        grid=(B, L // tm),
        in_specs=[pl.BlockSpec((1, tm, D), row),
                  pl.BlockSpec((1, 1, D), lambda b, i: (b, 0, 0)),
                  pl.BlockSpec((1, 1, D), lambda b, i: (b, 0, 0)),
                  pl.BlockSpec((D, N), lambda b, i: (0, 0), pipeline_mode=pl.Buffered(1)),
                  pl.BlockSpec((256, 256), lambda b, i: (0, 0)),
                  tab, tab, tab, tab],
        out_specs=[pl.BlockSpec((1, tm, Q_W), row), kv_spec, kv_spec, kv_spec, kv_spec,
                   pl.BlockSpec((1, tm, CONF_DIM), row)],
        out_shape=[jax.ShapeDtypeStruct((B, L, Q_W), BF16), kv_shape, kv_shape, kv_shape, kv_shape,
                   jax.ShapeDtypeStruct((B, L, CONF_DIM), F32)],
        compiler_params=_cparams(("parallel", "parallel")),
        name="even_in_proj",
    )(x, gs, sh, w, seg, cq, sq, ck, sk)


def _attn_kernel(q_ref, *refs, n_seg):
    o_ref = refs[4 * n_seg]
    group = N_Q_HEADS // N_KV_HEADS
    nt = (((1,), (1,)), ((), ()))

    def scores(hq):
        pr, pos = divmod(hq, 2)
        q2 = q_ref[0, :, LANES * pr:LANES * (pr + 1)]
        g = hq // group
        cols = slice(LANES * g, LANES * (g + 1))
        return [lax.dot_general(q2, refs[4 * t + pos][0, :, cols], nt, preferred_element_type=F32)
                for t in range(n_seg)]

    ss = scores(0)
    acc = None
    for hq in range(N_Q_HEADS):
        nxt = scores(hq + 1) if hq + 1 < N_Q_HEADS else None
        pr, pos = divmod(hq, 2)
        g = hq // group
        cols = slice(LANES * g, LANES * (g + 1))
        m = functools.reduce(jnp.maximum, [jnp.max(s, axis=-1, keepdims=True) for s in ss])
        ps = [jnp.exp(s - m) for s in ss]
        l = functools.reduce(jnp.add, [jnp.sum(p, axis=-1, keepdims=True) for p in ps])
        o = functools.reduce(jnp.add, [
            jnp.dot(p.astype(BF16), refs[4 * t + 2 + pos][0, :, cols], preferred_element_type=F32)
            for t, p in enumerate(ps)])
        o = o / l
        if pos == 0:
            acc = o
        else:
            o_ref[0, :, LANES * pr:LANES * (pr + 1)] = (acc + o).astype(o_ref.dtype)
        ss = nxt


def attention(q, segments):
    B, Lq, _ = q.shape
    tq = _tile(Lq, (256, 128))
    KP = N_KV_HEADS * LANES
    specs, args = [], []
    for seg in segments:
        for t in seg:
            specs.append(pl.BlockSpec((1, t.shape[1], KP), lambda b, i: (b, 0, 0)))
            args.append(t)
    return pl.pallas_call(
        functools.partial(_attn_kernel, n_seg=len(segments)),
        grid=(B, Lq // tq),
        in_specs=[pl.BlockSpec((1, tq, Q_W), lambda b, i: (b, i, 0))] + specs,
        out_specs=pl.BlockSpec((1, tq, Q_W), lambda b, i: (b, i, 0)),
        out_shape=jax.ShapeDtypeStruct((B, Lq, Q_W), BF16),
        compiler_params=_cparams(("parallel", "parallel")),
        name="attention",
    )(q, *args)


_CONV_PAD = 16


def _dwconv_kernel(x_ref, w_ref, b_ref, g_ref, beta_ref, o_ref, xp_ref, *, K, TR, ln_swish):
    L = x_ref.shape[1]
    C = x_ref.shape[2]
    half = (K - 1) // 2
    xp_ref[0:_CONV_PAD, :] = jnp.zeros((_CONV_PAD, C), F32)
    xp_ref[_CONV_PAD + L:, :] = jnp.zeros((_CONV_PAD, C), F32)
    xp_ref[_CONV_PAD:_CONV_PAD + L, :] = x_ref[0]

    groups = {}
    for k in range(K):
        q8, r8 = divmod(_CONV_PAD - half + k, 8)
        groups.setdefault(r8, []).append((k, q8))
    R = TR + 8
    CW = LANES if C % LANES == 0 else C

    def tile(t, carry):
        r0 = pl.multiple_of(t * TR, TR)
        parts = []
        for c0 in range(0, C, CW):
            acc = None
            for r8, taps in sorted(groups.items()):
                part = None
                for k, q8 in taps:
                    term = (xp_ref[pl.ds(pl.multiple_of(r0 + 8 * q8, 8), R), c0:c0 + CW]
                            * w_ref[k:k + 1, c0:c0 + CW])
                    part = term if part is None else part + term
                if r8:
                    part = pltpu.roll(part, R - r8, 0)
                part = part[:TR]
                acc = part if acc is None else acc + part
            parts.append(acc + b_ref[:, c0:c0 + CW])
        acc = parts[0] if len(parts) == 1 else jnp.concatenate(parts, axis=-1)
        if ln_swish:
            mu = jnp.mean(acc, axis=-1, keepdims=True)
            d = acc - mu
            var = jnp.mean(d * d, axis=-1, keepdims=True)
            y = d * lax.rsqrt(var + NORM_EPS) * g_ref[...] + beta_ref[...]
            acc = y * jax.nn.sigmoid(y)
        o_ref[0, pl.ds(r0, TR), :] = acc.astype(o_ref.dtype)
        return carry

    lax.fori_loop(0, L // TR, tile, 0)


def dwconv(x, w, b, g=None, beta=None, *, out_dtype=F32):
    B, L, _ = x.shape
    K, C = w.shape
    ln_swish = g is not None
    tc = C if ln_swish else _tile(C, (512, 256, 128))
    TR = _tile(L, (256, 128, 64, 32, 16, 8))
    if g is None:
        g = jnp.ones((C,), F32)
        beta = jnp.zeros((C,), F32)
    vec = pl.BlockSpec((1, tc), lambda bb, j: (0, j))
    return pl.pallas_call(
        functools.partial(_dwconv_kernel, K=K, TR=TR, ln_swish=ln_swish),
        grid=(B, C // tc),
        in_specs=[pl.BlockSpec((1, L, tc), lambda bb, j: (bb, 0, j)),
                  pl.BlockSpec((K, tc), lambda bb, j: (0, j)),
                  vec, vec, vec],
        out_specs=pl.BlockSpec((1, L, tc), lambda bb, j: (bb, 0, j)),
        out_shape=jax.ShapeDtypeStruct((B, L, C), out_dtype),
        scratch_shapes=[pltpu.VMEM((L + 2 * _CONV_PAD, tc), F32)],
        compiler_params=_cparams(("parallel", "parallel")),
        name="dwconv%d" % K,
    )(x, w, b.reshape(1, C), g.reshape(1, C), beta.reshape(1, C))


def _out_proj_kernel(x_ref, a1_ref, a2_ref, w1_ref, w2_ref, g_ref, o_ref):
    y = jnp.dot(a1_ref[0], w1_ref[...], preferred_element_type=F32)
    y = y + jnp.dot(a2_ref[0], w2_ref[...], preferred_element_type=F32)
    o_ref[0] = x_ref[0] + g_ref[0] * y


def out_proj_residual(x, a1, a2, w1, w2, gate):
    B, L, D = x.shape
    K1, K2 = a1.shape[2], a2.shape[2]
    tm = _tile(L, (2048, 1024, 512, 256, 128))
    row = lambda b, i: (b, i, 0)
    return pl.pallas_call(
        _out_proj_kernel,
        grid=(B, L // tm),
        in_specs=[pl.BlockSpec((1, tm, D), row),
                  pl.BlockSpec((1, tm, K1), row),
                  pl.BlockSpec((1, tm, K2), row),
                  pl.BlockSpec((K1, D), lambda b, i: (0, 0)),
                  pl.BlockSpec((K2, D), lambda b, i: (0, 0)),
                  pl.BlockSpec((1, 1, D), lambda b, i: (b, 0, 0))],
        out_specs=pl.BlockSpec((1, tm, D), row),
        out_shape=jax.ShapeDtypeStruct((B, L, D), F32),
        compiler_params=_cparams(("parallel", "parallel")),
        name="out_proj_residual",
    )(x, a1, a2, w1, w2, gate)


_FFN_CHUNK = 256


def _ffn_kernel(x_ref, xp_ref, xn_ref, gs_ref, sh_ref, wup_ref, cw_ref, cb_ref, wdn_ref, g5_ref,
                fg_ref, o_ref, acc_ref, *, final_norm):
    tm = x_ref.shape[1]
    FF = wdn_ref.shape[0]
    i = pl.program_id(1)
    has_prev = (i > 0).astype(F32)
    has_next = (i < pl.num_programs(1) - 1).astype(F32)
    gs = gs_ref[0]
    sh = sh_ref[0]
    x = x_ref[0]
    h = _norm_mod(x, gs, sh).astype(BF16)
    halo = jnp.concatenate([_norm_mod(xp_ref[0], gs, sh), _norm_mod(xn_ref[0], gs, sh)],
                           axis=0).astype(BF16)
    row = lax.broadcasted_iota(jnp.int32, (tm, _FFN_CHUNK), 0)
    first = row == 0
    last = row == tm - 1

    hcat = jnp.concatenate([h, halo], axis=0)

    def up(c0):
        r = jnp.dot(hcat, wup_ref[:, c0:c0 + _FFN_CHUNK], preferred_element_type=F32)
        return r[:tm], r[tm:]

    def conv3(c0, p, ph):
        prev_row = ph[7:8, :] * has_prev
        next_row = ph[8:9, :] * has_next
        pm1 = jnp.where(first, prev_row, pltpu.roll(p, 1, 0))
        pp1 = jnp.where(last, next_row, pltpu.roll(p, tm - 1, 0))
        return (cw_ref[0:1, c0:c0 + _FFN_CHUNK] * pm1 + cw_ref[1:2, c0:c0 + _FFN_CHUNK] * p
                + cw_ref[2:3, c0:c0 + _FFN_CHUNK] * pp1 + cb_ref[:, c0:c0 + _FFN_CHUNK])

    n_chunks = FF // _FFN_CHUNK
    ahead = 3
    ups = [(up(c * _FFN_CHUNK), up(FF + c * _FFN_CHUNK)) for c in range(min(ahead, n_chunks))]
    for c in range(n_chunks):
        if c + ahead < n_chunks:
            ups.append((up((c + ahead) * _FFN_CHUNK), up(FF + (c + ahead) * _FFN_CHUNK)))
        val = conv3(c * _FFN_CHUNK, *ups[c][0])
        gate = conv3(FF + c * _FFN_CHUNK, *ups[c][1])
        ups[c] = None
        act = (gate * jax.nn.sigmoid(gate) * val).astype(BF16)
        part = jnp.dot(act, wdn_ref[c * _FFN_CHUNK:(c + 1) * _FFN_CHUNK, :],
                       preferred_element_type=F32)
        if c == 0:
            acc_ref[...] = part
        else:
            acc_ref[...] += part
    y = x + g5_ref[0] * acc_ref[...]
    if final_norm:
        y = y * lax.rsqrt(jnp.mean(y * y, axis=-1, keepdims=True) + NORM_EPS) * fg_ref[...]
    o_ref[0] = y


def conv_ffn_residual(x, gs, sh, w_up, cw, cb, w_dn, gate, final_g=None):
    B, L, D = x.shape
    FF = w_dn.shape[0]
    assert FF % _FFN_CHUNK == 0
    tm = _tile(L, (256, 128))
    nb8 = L // 8
    t8 = tm // 8
    final_norm = final_g is not None
    if final_g is None:
        final_g = jnp.ones((D,), F32)
    row = lambda b, i: (b, i, 0)
    const = lambda b, i: (0, 0)
    return pl.pallas_call(
        functools.partial(_ffn_kernel, final_norm=final_norm),
        grid=(B, L // tm),
        in_specs=[pl.BlockSpec((1, tm, D), row),
                  pl.BlockSpec((1, 8, D), lambda b, i: (b, jnp.maximum(i * t8 - 1, 0), 0)),
                  pl.BlockSpec((1, 8, D), lambda b, i: (b, jnp.minimum((i + 1) * t8, nb8 - 1), 0)),
                  pl.BlockSpec((1, 1, D), lambda b, i: (b, 0, 0)),
                  pl.BlockSpec((1, 1, D), lambda b, i: (b, 0, 0)),
                  pl.BlockSpec((D, 2 * FF), const, pipeline_mode=pl.Buffered(1)),
                  pl.BlockSpec((3, 2 * FF), const),
                  pl.BlockSpec((1, 2 * FF), const),
                  pl.BlockSpec((FF, D), const, pipeline_mode=pl.Buffered(1)),
                  pl.BlockSpec((1, 1, D), lambda b, i: (b, 0, 0)),
                  pl.BlockSpec((1, D), const)],
        out_specs=pl.BlockSpec((1, tm, D), row),
        out_shape=jax.ShapeDtypeStruct((B, L, D), F32),
        scratch_shapes=[pltpu.VMEM((tm, D), F32)],
        compiler_params=_cparams(("parallel", "arbitrary")),
        name="conv_ffn",
    )(x, x, x, gs, sh, w_up, cw, cb.reshape(1, 2 * FF), w_dn, gate, final_g.reshape(1, D))


def _hy_filter_kernel(feat_ref, w1_ref, b1_ref, w2_ref, b2_ref, w3_ref, fr_ref, dec_ref,
                      hs_ref, hd_ref, nyq_ref):
    L = feat_ref.shape[0]
    fr = fr_ref[...]
    z = jnp.dot(feat_ref[...].astype(BF16), w1_ref[...].astype(BF16), preferred_element_type=F32)
    hdn = jnp.sin(fr * (z + b1_ref[...]))
    z = jnp.dot(hdn.astype(BF16), w2_ref[...].astype(BF16), preferred_element_type=F32)
    hdn = jnp.sin(fr * (z + b2_ref[...])).astype(BF16)
    row = lax.broadcasted_iota(jnp.int32, (L, HY_DIM), 0)
    not_first = (row > 0).astype(F32)
    sign = jnp.where((row & 1) == 0, 1.0, -1.0).astype(F32)
    dec = dec_ref[...]
    n_order = w3_ref.shape[1] // (2 * HY_DIM)
    for o in range(n_order):
        c0 = o * 2 * HY_DIM
        hf = jnp.dot(hdn, w3_ref[:, c0:c0 + HY_DIM].astype(BF16), preferred_element_type=F32) * dec
        hb = (jnp.dot(hdn, w3_ref[:, c0 + HY_DIM:c0 + 2 * HY_DIM].astype(BF16),
                      preferred_element_type=F32) * dec * not_first)
        inv = 1.0 / (jnp.sum(jnp.abs(hf), axis=0, keepdims=True)
                     + jnp.sum(jnp.abs(hb), axis=0, keepdims=True))
        hs = (hf + hb) * inv
        hd = (hf - hb) * inv
        hs_ref[:, o * HY_DIM:(o + 1) * HY_DIM] = hs
        hd_ref[:, o * HY_DIM:(o + 1) * HY_DIM] = hd
        nyq_ref[:, o * HY_DIM:(o + 1) * HY_DIM] = jnp.sum(hs * sign, axis=0, keepdims=True)


def hyena_filter_taps(L, w1, b1, w2, b2, w3, freq):
    emb = w1.shape[0]
    bands = (emb - 1) // 2
    t = jnp.linspace(0.0, 1.0, L, dtype=F32)[:, None]
    fb = jnp.linspace(1e-4, bands - 1, bands, dtype=F32)
    wpos = (2.0 * math.pi / L) * jnp.arange(L, dtype=F32)[:, None]
    feats = jnp.concatenate([t, jnp.cos(fb * wpos), -jnp.sin(fb * wpos)], axis=-1)
    feats = jnp.pad(feats, ((0, 0), (0, LANES - emb)))
    w1p = jnp.pad(w1, ((0, LANES - emb), (0, 0)))
    deltas = jnp.abs(jnp.linspace(math.log(C_DECAY_TARGET) / C_SLOW_DECAY,
                                  math.log(C_DECAY_TARGET) / C_FAST_DECAY, HY_DIM, dtype=F32))
    dec = jnp.exp(-t * deltas)
    hid = w1.shape[1]
    NO = w3.shape[1] // 2
    full = lambda shape: pl.BlockSpec(shape, lambda: tuple(0 for _ in shape))
    return pl.pallas_call(
        _hy_filter_kernel,
        in_specs=[full((L, LANES)), full((LANES, hid)), full((1, hid)), full((hid, hid)),
                  full((1, hid)), full((hid, 2 * NO)), full((1, hid)), full((L, HY_DIM))],
        out_specs=[full((L, NO)), full((L, NO)), full((1, NO))],
        out_shape=[jax.ShapeDtypeStruct((L, NO), F32), jax.ShapeDtypeStruct((L, NO), F32),
                   jax.ShapeDtypeStruct((1, NO), F32)],
        compiler_params=pltpu.CompilerParams(vmem_limit_bytes=V7X_VMEM_LIMIT),
        name="hyena_filter",
    )(feats, w1p, b1.reshape(1, hid), w2, b2.reshape(1, hid), w3, freq.reshape(1, hid), dec)


_DFT_SPLIT = 64


def _dft_tables(L):
    N = 2 * L
    n = jnp.arange(L, dtype=jnp.int32)

    def cos_sin(mult):
        ang = ((mult[:, None] * n[None, :]) & (N - 1)).astype(F32) * (2.0 * math.pi / N)
        return jnp.cos(ang), jnp.sin(ang)

    ca, sa = cos_sin(jnp.arange(L // _DFT_SPLIT, dtype=jnp.int32) * _DFT_SPLIT)
    cb, sb = cos_sin(jnp.arange(_DFT_SPLIT, dtype=jnp.int32))
    nyq = jnp.where((n & 1) == 0, 1.0, -1.0).astype(F32)
    re = (ca[:, None, :] * cb[None] - sa[:, None, :] * sb[None]).reshape(L, L)
    im = -(sa[:, None, :] * cb[None] + ca[:, None, :] * sb[None]).reshape(L, L)
    im = jnp.where(jnp.arange(L)[:, None] == 0, nyq[None, :], im)
    f = jnp.concatenate([re, im], axis=0).astype(BF16)
    cat, sat, cbt, sbt = ca.T, sa.T, cb.T, sb.T
    re_t = (cat[:, :, None] * cbt[:, None, :] - sat[:, :, None] * sbt[:, None, :]).reshape(L, L)
    im_t = -(sat[:, :, None] * cbt[:, None, :] + cat[:, :, None] * sbt[:, None, :]).reshape(L, L)
    im_t = jnp.where(jnp.arange(L)[None, :] == 0, nyq[:, None], im_t)
    g = jnp.concatenate([re_t, im_t], axis=1).astype(BF16)
    return f, g


def _hy_spec_kernel(fre_ref, fim_ref, hs_ref, hd_ref, nyq_ref, a_ref, b_ref):
    kb = fre_ref.shape[0]
    L = fre_ref.shape[1]
    NO = hs_ref.shape[1]
    krow = lax.broadcasted_iota(jnp.int32, (kb, NO), 0) + pl.program_id(0) * kb
    dc = krow == 0
    scale = jnp.where(dc, 1.0, 2.0).astype(F32) * (1.0 / (2 * L))
    hr = jnp.dot(fre_ref[...], hs_ref[...].astype(BF16), preferred_element_type=F32)
    hi = jnp.dot(fim_ref[...], hd_ref[...].astype(BF16), preferred_element_type=F32)
    a_ref[...] = hr * scale
    b_ref[...] = jnp.where(dc, nyq_ref[...], hi) * scale


def hyena_spectrum(F, hs, hd, nyq):
    L, NO = hs.shape
    kb = _tile(L, (256, 128))
    nk = L // kb
    return pl.pallas_call(
        _hy_spec_kernel,
        grid=(nk,),
        in_specs=[pl.BlockSpec((kb, L), lambda i: (i, 0)),
                  pl.BlockSpec((kb, L), lambda i: (i + nk, 0)),
                  pl.BlockSpec((L, NO), lambda i: (0, 0)),
                  pl.BlockSpec((L, NO), lambda i: (0, 0)),
                  pl.BlockSpec((1, NO), lambda i: (0, 0))],
        out_specs=[pl.BlockSpec((kb, NO), lambda i: (i, 0)),
                   pl.BlockSpec((kb, NO), lambda i: (i, 0))],
        out_shape=[jax.ShapeDtypeStruct((L, NO), F32), jax.ShapeDtypeStruct((L, NO), F32)],
        compiler_params=_cparams(("arbitrary",)),
        name="hyena_spectrum",
    )(F, F, hs, hd, nyq)


def _hy_conv_kernel(z_ref, gate_ref, fre_ref, fim_ref, gre_ref, gim_ref, a_ref, b_ref, skip_ref,
                    o_ref, zb_ref, y_ref):
    kb = fre_ref.shape[0]
    C = z_ref.shape[2]
    kstep = pl.program_id(1)

    @pl.when(kstep == 0)
    def _():
        zb_ref[...] = z_ref[0].astype(BF16)
        y_ref[...] = jnp.zeros(y_ref.shape, F32)

    NH = 2
    hb = kb // NH
    CH = C // 2
    blocks = [(hf, ch) for hf in range(NH) for ch in range(2)]
    fwd = []
    for hf, ch in blocks:
        rows = slice(hf * hb, (hf + 1) * hb)
        zc = zb_ref[:, ch * CH:(ch + 1) * CH]
        fwd.append((jnp.dot(fre_ref[rows, :], zc, preferred_element_type=F32),
                    jnp.dot(fim_ref[rows, :], zc, preferred_element_type=F32)))
    for (hf, ch), (ur, ui) in zip(blocks, fwd):
        rows = slice(hf * hb, (hf + 1) * hb)
        cols = slice(ch * CH, (ch + 1) * CH)
        a = a_ref[rows, cols]
        b = b_ref[rows, cols]
        krow = lax.broadcasted_iota(jnp.int32, (hb, CH), 0) + (kstep * kb + hf * hb)
        dc = krow == 0
        vr = jnp.where(dc, ur * a, ur * a - ui * b)
        vi = jnp.where(dc, ui * b, ur * b + ui * a)
        y_ref[:, cols] += (jnp.dot(gre_ref[:, rows], vr.astype(BF16), preferred_element_type=F32)
                           + jnp.dot(gim_ref[:, rows], vi.astype(BF16), preferred_element_type=F32))

    @pl.when(kstep == pl.num_programs(1) - 1)
    def _():
        o_ref[0] = (gate_ref[0] * (y_ref[...] + skip_ref[...] * z_ref[0])).astype(o_ref.dtype)


def hyena_conv(z_src, z_col, gate_src, gate_col, F, G, A, Bt, spec_col, skip, out_dtype):
    B, L, _ = z_src.shape
    C = HY_DIM
    kb = _tile(L, (512, 256))
    nk = L // kb
    return pl.pallas_call(
        _hy_conv_kernel,
        grid=(B, nk),
        in_specs=[pl.BlockSpec((1, L, C), lambda b, k: (b, 0, z_col)),
                  pl.BlockSpec((1, L, C), lambda b, k: (b, 0, gate_col)),
                  pl.BlockSpec((kb, L), lambda b, k: (k, 0)),
                  pl.BlockSpec((kb, L), lambda b, k: (k + nk, 0)),
                  pl.BlockSpec((L, kb), lambda b, k: (0, k)),
                  pl.BlockSpec((L, kb), lambda b, k: (0, k + nk)),
                  pl.BlockSpec((kb, C), lambda b, k: (k, spec_col)),
                  pl.BlockSpec((kb, C), lambda b, k: (k, spec_col)),
                  pl.BlockSpec((1, C), lambda b, k: (0, 0))],
        out_specs=pl.BlockSpec((1, L, C), lambda b, k: (b, 0, 0)),
        out_shape=jax.ShapeDtypeStruct((B, L, C), out_dtype),
        scratch_shapes=[pltpu.VMEM((L, C), BF16), pltpu.VMEM((L, C), F32)],
        compiler_params=_cparams(("parallel", "arbitrary")),
        name="hyena_conv",
    )(z_src, gate_src, F, F, G, G, A, Bt, skip.reshape(1, C))


HG_GROUP = 128


def _hgrn2_kernel(ql_ref, ffl_ref, fbl_ref, il_ref, gl_ref, qc_ref, ffc_ref, fbc_ref, ic_ref,
                  lb_ref, gn_ref, o_ref, vb_ref, qdf_ref, qdb_ref, qgf_ref, qgb_ref,
                  kuf_ref, kub_ref, kgf_ref, kgb_ref, kdf_ref, kdb_ref, decf_ref, decb_ref,
                  oif_ref, oib_ref, stf_ref, stb_ref):
    C = HG_CHUNK
    GR = HG_GROUP
    G2 = 2 * C
    L = ql_ref.shape[1]
    Lc = qc_ref.shape[1]
    r = lax.broadcasted_iota(jnp.int32, (GR, GR), 0)
    c = lax.broadcasted_iota(jnp.int32, (GR, GR), 1)
    same = (r // C) == (c // C)
    same_group = (r // G2) == (c // G2)
    row_odd = ((lax.broadcasted_iota(jnp.int32, (GR, HG_DIM), 0) // C) & 1) == 1
    dirs = []
    for d, (incl, excl) in enumerate(((c <= r, c > r), (c >= r, c < r))):
        mask = same & incl
        cross = same_group & jnp.logical_not(same) & incl
        second = row_odd if d == 0 else jnp.logical_not(row_odd)
        dirs.append((lb_ref[d:d + 1, :], mask.astype(BF16), (same & excl).astype(BF16), mask, cross,
                     second))
    f_refs_lat = (ffl_ref, fbl_ref)
    f_refs_ctx = (ffc_ref, fbc_ref)
    qd_refs = (qdf_ref, qdb_ref)
    qg_refs = (qgf_ref, qgb_ref)
    ku_refs = (kuf_ref, kub_ref)
    kg_refs = (kgf_ref, kgb_ref)
    kd_refs = (kdf_ref, kdb_ref)
    dec_refs = (decf_ref, decb_ref)
    oi_refs = (oif_ref, oib_ref)
    st_refs = (stf_ref, stb_ref)
    nt = (((1,), (1,)), ((), ()))

    def gates(q_ref, f_refs, i_ref, n_rows, row_off, with_out):
        def body(gi, carry):
            r0 = pl.multiple_of(gi * GR, GR)
            ro = pl.multiple_of(row_off + gi * GR, GR)
            q = q_ref[0, pl.ds(r0, GR), :]
            vb_ref[pl.ds(ro, GR), :] = i_ref[0, pl.ds(r0, GR), :].astype(BF16)
            for d, (lbd, t_in, t_ex, _, _, second) in enumerate(dirs):
                f = lbd + (1.0 - lbd) * jax.nn.sigmoid(f_refs[d][0, pl.ds(r0, GR), :])
                k = 1.0 - f
                lg = jnp.log(f)
                hi = lg.astype(BF16)
                lo = (lg - hi.astype(F32)).astype(BF16)
                b = (jnp.dot(t_in, hi, preferred_element_type=F32)
                     + jnp.dot(t_in, lo, preferred_element_type=F32))
                rest = (jnp.dot(t_ex, hi, preferred_element_type=F32)
                        + jnp.dot(t_ex, lo, preferred_element_type=F32))
                tot = b + rest
                other = jnp.where(row_odd, pltpu.roll(tot, C, 0), pltpu.roll(tot, GR - C, 0))
                qd_refs[d][pl.ds(ro, GR), :] = (q * jnp.exp(b)).astype(BF16)
                qg_refs[d][pl.ds(ro, GR), :] = (
                    q * jnp.exp(b + jnp.where(second, other, 0.0))).astype(BF16)
                ku_refs[d][pl.ds(ro, GR), :] = (k * jnp.exp(rest)).astype(BF16)
                kg_refs[d][pl.ds(ro, GR), :] = (
                    k * jnp.exp(rest + jnp.where(second, 0.0, other))).astype(BF16)
                dec_refs[d][pl.ds(ro, GR), :] = tot + other
                if with_out:
                    kd_refs[d][pl.ds(r0, GR), :] = (k * jnp.exp(-b)).astype(BF16)
            return carry

        lax.fori_loop(0, n_rows // GR, body, 0, unroll=2)

    def intra():
        U = 4 if (L // GR) % 4 == 0 else 1

        def body(gi, carry):
            jobs = []
            for u in range(U):
                r0 = pl.multiple_of((gi * U + u) * GR, GR)
                ro = pl.multiple_of(Lc + (gi * U + u) * GR, GR)
                for d in range(2):
                    qd = qd_refs[d][pl.ds(ro, GR), :]
                    sc = lax.dot_general(qd, kd_refs[d][pl.ds(r0, GR), :], nt,
                                         preferred_element_type=F32)
                    sx = lax.dot_general(qd, ku_refs[d][pl.ds(ro, GR), :], nt,
                                         preferred_element_type=F32)
                    jobs.append((d, r0, ro, sc, sx))
            for d, r0, ro, sc, sx in jobs:
                sc = jnp.where(dirs[d][3], sc, jnp.where(dirs[d][4], sx, 0.0)).astype(BF16)
                oi_refs[d][pl.ds(r0, GR), :] = jnp.dot(sc, vb_ref[pl.ds(ro, GR), :],
                                                       preferred_element_type=F32)
            return carry

        lax.fori_loop(0, L // (GR * U), body, 0)

    def recurrence(n_rows, row_off, keep, carry):
        n = n_rows // G2

        def body(j, states):
            new = []
            for d, st in enumerate(states):
                jj = j if d == 0 else n - 1 - j
                ro = pl.multiple_of(row_off + jj * G2, G2)
                if keep:
                    st_refs[d][pl.ds(pl.multiple_of(jj * HG_DIM, HG_DIM), HG_DIM), :] = st.astype(BF16)
                upd = lax.dot_general(vb_ref[pl.ds(ro, G2), :], kg_refs[d][pl.ds(ro, G2), :],
                                      (((0,), (0,)), ((), ())), preferred_element_type=F32)
                new.append(st * jnp.exp(dec_refs[d][pl.ds(ro, 1), :]) + upd)
            return tuple(new)

        return lax.fori_loop(0, n, body, carry, unroll=16)

    def inter():
        def body(j, carry):
            r0 = pl.multiple_of(j * G2, G2)
            ro = pl.multiple_of(Lc + j * G2, G2)
            so = pl.multiple_of(j * HG_DIM, HG_DIM)
            for d in range(2):
                o = lax.dot_general(qg_refs[d][pl.ds(ro, G2), :], st_refs[d][pl.ds(so, HG_DIM), :], nt,
                                    preferred_element_type=F32)
                oi_refs[d][pl.ds(r0, G2), :] += o
            return carry

        lax.fori_loop(0, L // G2, body, 0, unroll=32)

    gates(qc_ref, f_refs_ctx, ic_ref, Lc, 0, False)
    gates(ql_ref, f_refs_lat, il_ref, L, Lc, True)
    intra()
    zero = jnp.zeros((HG_DIM, HG_DIM), F32)
    carry = recurrence(Lc, 0, False, (zero, zero))
    recurrence(L, Lc, True, carry)
    inter()
    o = oif_ref[...] + oib_ref[...]
    o = o * lax.rsqrt(jnp.mean(o * o, axis=-1, keepdims=True) + NORM_EPS) * gn_ref[...]
    g = gl_ref[0]
    o_ref[0] = (o * (g * jax.nn.sigmoid(g))).astype(o_ref.dtype)


def hgrn2_mixer(p_lat, lat_col0, p_ctx, ctx_col0, lb, gn_g):
    B, L, _ = p_lat.shape
    Lc = p_ctx.shape[1]
    lc0 = lat_col0 // HG_DIM
    cc0 = ctx_col0 // HG_DIM

    def lat(seg):
        return pl.BlockSpec((1, L, HG_DIM), lambda b, h: (b, 0, lc0 + HG_HEADS * seg + h))

    def ctx(seg):
        return pl.BlockSpec((1, Lc, HG_DIM), lambda b, h: (b, 0, cc0 + HG_HEADS * seg + h))

    return pl.pallas_call(
        _hgrn2_kernel,
        grid=(B, HG_HEADS),
        in_specs=[lat(0), lat(1), lat(2), lat(3), lat(4), ctx(0), ctx(1), ctx(2), ctx(3),
                  pl.BlockSpec((2, HG_DIM), lambda b, h: (0, h)),
                  pl.BlockSpec((1, HG_DIM), lambda b, h: (0, 0))],
        out_specs=pl.BlockSpec((1, L, HG_DIM), lambda b, h: (b, 0, h)),
        out_shape=jax.ShapeDtypeStruct((B, L, HG_W), BF16),
        scratch_shapes=([pltpu.VMEM((Lc + L, HG_DIM), BF16)] * 9
                        + [pltpu.VMEM((L, HG_DIM), BF16)] * 2
                        + [pltpu.VMEM((Lc + L, HG_DIM), F32)] * 2
                        + [pltpu.VMEM((L, HG_DIM), F32)] * 2
                        + [pltpu.VMEM((L // (2 * HG_CHUNK) * HG_DIM, HG_DIM), BF16)] * 2),
        compiler_params=_cparams(("parallel", "parallel")),
        name="hgrn2",
    )(p_lat, p_lat, p_lat, p_lat, p_lat, p_ctx, p_ctx, p_ctx, p_ctx, lb, gn_g.reshape(1, HG_DIM))


def _rope_tables(L, gain, scale, rotate):
    g2 = jnp.tile(gain.astype(F32), LANES // HEAD_DIM)[None, :] * scale
    gsw = g2.reshape(1, LANES // 2, 2)[:, :, ::-1].reshape(1, LANES)
    if not rotate:
        return jnp.broadcast_to(g2, (L, LANES)), jnp.zeros((L, LANES), F32)
    rows = L // GRID_W
    row = jnp.repeat(jnp.arange(rows, dtype=F32), GRID_W)
    col = jnp.tile(jnp.arange(GRID_W, dtype=F32), rows)
    axis_dim = HEAD_DIM // 2
    inv_freq = ROPE_THETA ** (-jnp.arange(0, axis_dim, 2, dtype=F32) / axis_dim)
    ang = jnp.concatenate([row[:, None] * inv_freq, col[:, None] * inv_freq], axis=-1)
    cos = jnp.tile(jnp.repeat(jnp.cos(ang), 2, axis=-1), (1, LANES // HEAD_DIM))
    sin = jnp.tile(jnp.repeat(jnp.sin(ang), 2, axis=-1), (1, LANES // HEAD_DIM))
    sign = jnp.tile(jnp.array([-1.0, 1.0], F32), LANES // 2)[None, :]
    return cos * g2, sin * sign * gsw


def _split6(m):
    return jnp.split(m, 6, axis=-1)


def kernel(x, c, ctx, c_ctx, ada_w, ada_b, norm1_g, norm2_g, final_g, ev_w_in, ev_q_gain, ev_k_gain,
           ev_conv_w, ev_conv_b, ev_ln_g, ev_ln_b, ev_w_out, od_w_in, od_short_w, od_short_b,
           od_filt_w1, od_filt_b1, od_filt_w2, od_filt_b2, od_filt_w3, od_filt_freq, od_hyena_skip,
           od_lower_bound, od_gnorm_g, od_w_out, ffn_w_up, ffn_conv_w, ffn_conv_b, ffn_w_down):
    B, L, D = x.shape
    Lc = ctx.shape[1]
    depth = ada_w.shape[0]
    assert depth == 2, "layer 0 (attention + conformer) followed by layer 1 (hyena + hgrn2)"

    lb_soft = jax.nn.softmax(od_lower_bound.astype(F32), axis=0)
    lower_bounds = jnp.cumsum(lb_soft, axis=0) - lb_soft[0:1]

    cond = jnp.concatenate([c, c_ctx[None, :], jnp.zeros((7, D), F32)], axis=0)

    def mods(l):
        m = ada_modulation(cond, ada_w[l], ada_b[l])
        ml = [t[:, None, :] for t in _split6(m[:B])]
        mc = [jnp.broadcast_to(t[None], (B, 1, D)) for t in _split6(m[B:B + 1])]
        return ml, mc

    def ffn(xx, mm, l, final=None):
        return conv_ffn_residual(xx, norm2_g[l] * (1.0 + mm[4]), mm[3], w_up[l],
                                 ffn_conv_w[l], ffn_conv_b[l], w_down[l], mm[5], final)

    w_up = [layer_weight_bf16(ffn_w_up, l) for l in range(depth)]
    w_down = [layer_weight_bf16(ffn_w_down, l) for l in range(depth)]

    ml, mc = mods(0)
    w_in = layer_weight_bf16(ev_w_in, 0)
    seg = jnp.kron(jnp.eye(256 // HEAD_DIM, dtype=F32), jnp.ones((HEAD_DIM, HEAD_DIM), F32)).astype(BF16)
    scale = HEAD_DIM ** -0.5
    cq_l, sq_l = _rope_tables(L, ev_q_gain[0], scale, True)
    ck_l, sk_l = _rope_tables(L, ev_k_gain[0], 1.0, True)
    cq_c, sq_c = _rope_tables(Lc, ev_q_gain[0], scale, False)
    ck_c, sk_c = _rope_tables(Lc, ev_k_gain[0], 1.0, False)
    q_l, *kv_l, glu_l = even_in_proj(x, norm1_g[0] * (1.0 + ml[1]), ml[0], w_in, seg,
                                     cq_l, sq_l, ck_l, sk_l)
    q_c, *kv_c, glu_c = even_in_proj(ctx, norm1_g[0] * (1.0 + mc[1]), mc[0], w_in, seg,
                                     cq_c, sq_c, ck_c, sk_c)
    attn_l = attention(q_l, [kv_c, kv_l])
    attn_c = attention(q_c, [kv_c])
    conv_l = dwconv(glu_l, ev_conv_w[0], ev_conv_b[0], ev_ln_g[0], ev_ln_b[0], out_dtype=BF16)
    conv_c = dwconv(glu_c, ev_conv_w[0], ev_conv_b[0], ev_ln_g[0], ev_ln_b[0], out_dtype=BF16)
    wo = layer_weight_bf16(ev_w_out, 0)
    wo1 = wo[:Q_W]
    wo2 = wo[Q_W:]
    x = out_proj_residual(x, attn_l, conv_l, wo1, wo2, ml[2])
    ctx = out_proj_residual(ctx, attn_c, conv_c, wo1, wo2, mc[2])
    x = ffn(x, ml, 0)
    ctx = ffn(ctx, mc, 0)

    ml, mc = mods(1)
    w_in = layer_weight_bf16(od_w_in, 0)
    p_lat = odd_in_proj(x, norm1_g[1] * (1.0 + ml[1]), ml[0], w_in, od_short_w[0], od_short_b[0])
    p_ctx = norm_mod_matmul(ctx, norm1_g[1] * (1.0 + mc[1]), mc[0],
                            layer_weight_bf16(od_w_in, 0, HY_IN))
    d_lat = hgrn2_mixer(p_lat, HY_IN, p_ctx, 0, lower_bounds[1], od_gnorm_g[0])

    hs, hd, nyq = hyena_filter_taps(L, od_filt_w1[0], od_filt_b1[0], od_filt_w2[0], od_filt_b2[0],
                                    od_filt_w3[0], od_filt_freq[0])
    F, G = _dft_tables(L)
    A, Bt = hyena_spectrum(F, hs, hd, nyq)
    z1 = hyena_conv(p_lat, 0, p_lat, 1, F, G, A, Bt, 0, od_hyena_skip[0][0], F32)
    c_lat = hyena_conv(z1, 0, p_lat, 2, F, G, A, Bt, 1, od_hyena_skip[0][1], BF16)

    wo = layer_weight_bf16(od_w_out, 0)
    x = out_proj_residual(x, c_lat, d_lat, wo[:HY_DIM], wo[HY_DIM:], ml[2])
    return ffn(x, ml, 1, final_g)
```

```python
import functools
import math

import jax
import jax.numpy as jnp
from jax import lax
from jax.experimental import pallas as pl
from jax.experimental.pallas import tpu as pltpu

F32 = jnp.float32
BF16 = jnp.bfloat16

NORM_EPS = 1e-6
GRID_W = 64
ROPE_THETA = 10000.0
HEAD_DIM = 64
N_Q_HEADS = 12
N_KV_HEADS = 4
Q_W = N_Q_HEADS * HEAD_DIM
KV_W = N_KV_HEADS * HEAD_DIM
CONF_DIM = 512
HY_DIM = 512
HY_IN = 3 * HY_DIM
HG_HEADS = 4
HG_DIM = 128
HG_W = HG_HEADS * HG_DIM
HG_CHUNK = 32
C_FAST_DECAY = 0.3
C_SLOW_DECAY = 1.5
C_DECAY_TARGET = 1e-2
V7X_VMEM_LIMIT = 56 * 1024 * 1024
LANES = 128


def _cparams(sem):
    return pltpu.CompilerParams(dimension_semantics=sem, vmem_limit_bytes=V7X_VMEM_LIMIT)


def _tile(n, prefs):
    for p in prefs:
        if n % p == 0:
            return p
    return n


def _norm_mod(x, gs, sh):
    r = lax.rsqrt(jnp.mean(x * x, axis=-1, keepdims=True) + NORM_EPS)
    return x * r * gs + sh


def _cast_kernel(w_ref, o_ref):
    o_ref[...] = w_ref[0].astype(o_ref.dtype)


def layer_weight_bf16(w, layer, col0=0):
    _, R, C = w.shape
    tr = _tile(R, (256, 128, 64, 32, 16))
    tc = math.gcd(C, col0) if col0 else C
    return pl.pallas_call(
        _cast_kernel,
        grid=(R // tr, (C - col0) // tc),
        in_specs=[pl.BlockSpec((1, tr, tc), lambda i, j: (layer, i, j + col0 // tc))],
        out_specs=pl.BlockSpec((tr, tc), lambda i, j: (i, j)),
        out_shape=jax.ShapeDtypeStruct((R, C - col0), BF16),
        compiler_params=_cparams(("parallel", "parallel")),
        name="layer_weight_bf16",
    )(w)


def _ada_kernel(c_ref, w_ref, b_ref, o_ref):
    c = c_ref[...]
    s = (c * jax.nn.sigmoid(c)).astype(BF16)
    o_ref[...] = jnp.dot(s, w_ref[...].astype(BF16), preferred_element_type=F32) + b_ref[...]


def ada_modulation(cs, w, b):
    R, D = cs.shape
    N = w.shape[1]
    tn = _tile(N, (1024, 512, 256, 128))
    return pl.pallas_call(
        _ada_kernel,
        grid=(N // tn,),
        in_specs=[pl.BlockSpec((R, D), lambda j: (0, 0)),
                  pl.BlockSpec((D, tn), lambda j: (0, j)),
                  pl.BlockSpec((1, tn), lambda j: (0, j))],
        out_specs=pl.BlockSpec((R, tn), lambda j: (0, j)),
        out_shape=jax.ShapeDtypeStruct((R, N), F32),
        compiler_params=_cparams(("arbitrary",)),
        name="ada_modulation",
    )(cs, w, b.reshape(1, N))


def _nm_mm_kernel(x_ref, gs_ref, sh_ref, w_ref, o_ref, *, cn):
    h = _norm_mod(x_ref[0], gs_ref[0], sh_ref[0]).astype(BF16)
    for c0 in range(0, w_ref.shape[1], cn):
        o_ref[0, :, c0:c0 + cn] = jnp.dot(h, w_ref[:, c0:c0 + cn],
                                          preferred_element_type=F32).astype(o_ref.dtype)


def norm_mod_matmul(x, gs, sh, w, out_dtype=F32):
    B, L, D = x.shape
    N = w.shape[1]
    tm = _tile(L, (512, 256, 128))
    cn = _tile(N, (512, 256, 128))
    return pl.pallas_call(
        functools.partial(_nm_mm_kernel, cn=cn),
        grid=(B, L // tm),
        in_specs=[pl.BlockSpec((1, tm, D), lambda b, i: (b, i, 0)),
                  pl.BlockSpec((1, 1, D), lambda b, i: (b, 0, 0)),
                  pl.BlockSpec((1, 1, D), lambda b, i: (b, 0, 0)),
                  pl.BlockSpec((D, N), lambda b, i: (0, 0), pipeline_mode=pl.Buffered(1))],
        out_specs=pl.BlockSpec((1, tm, N), lambda b, i: (b, i, 0)),
        out_shape=jax.ShapeDtypeStruct((B, L, N), out_dtype),
        compiler_params=_cparams(("parallel", "parallel")),
        name="norm_mod_matmul",
    )(x, gs, sh, w)


def _odd_in_kernel(x_ref, xp_ref, xn_ref, gs_ref, sh_ref, w_ref, cw_ref, cb_ref, o_ref, *, cn, conv_cols):
    tm = x_ref.shape[1]
    i = pl.program_id(1)
    has_prev = (i > 0).astype(F32)
    has_next = (i < pl.num_programs(1) - 1).astype(F32)
    gs = gs_ref[0]
    sh = sh_ref[0]
    h = _norm_mod(x_ref[0], gs, sh).astype(BF16)
    halo = jnp.concatenate([_norm_mod(xp_ref[0], gs, sh), _norm_mod(xn_ref[0], gs, sh)],
                           axis=0).astype(BF16)
    row = lax.broadcasted_iota(jnp.int32, (tm, cn), 0)
    first = row == 0
    last = row == tm - 1
    hcat = jnp.concatenate([h, halo], axis=0)

    def proj(c0):
        lhs = hcat if c0 < conv_cols else h
        return jnp.dot(lhs, w_ref[:, c0:c0 + cn], preferred_element_type=F32)

    starts = list(range(0, w_ref.shape[1], cn))
    r = proj(starts[0])
    for idx, c0 in enumerate(starts):
        r_next = proj(starts[idx + 1]) if idx + 1 < len(starts) else None
        if c0 < conv_cols:
            p = r[:tm]
            pm1 = jnp.where(first, r[tm + 7:tm + 8, :] * has_prev, pltpu.roll(p, 1, 0))
            pp1 = jnp.where(last, r[tm + 8:tm + 9, :] * has_next, pltpu.roll(p, tm - 1, 0))
            r = (cw_ref[0:1, c0:c0 + cn] * pm1 + cw_ref[1:2, c0:c0 + cn] * p
                 + cw_ref[2:3, c0:c0 + cn] * pp1 + cb_ref[:, c0:c0 + cn])
        o_ref[0, :, c0:c0 + cn] = r
        r = r_next


def odd_in_proj(x, gs, sh, w, cw, cb):
    B, L, D = x.shape
    N = w.shape[1]
    conv_cols = cw.shape[1]
    tm = _tile(L, (1024, 512, 256, 128))
    cn = _tile(math.gcd(N, conv_cols), (512, 256, 128))
    nb8 = L // 8
    t8 = tm // 8
    const = lambda b, i: (0, 0)
    return pl.pallas_call(
        functools.partial(_odd_in_kernel, cn=cn, conv_cols=conv_cols),
        grid=(B, L // tm),
        in_specs=[pl.BlockSpec((1, tm, D), lambda b, i: (b, i, 0)),
                  pl.BlockSpec((1, 8, D), lambda b, i: (b, jnp.maximum(i * t8 - 1, 0), 0)),
                  pl.BlockSpec((1, 8, D), lambda b, i: (b, jnp.minimum((i + 1) * t8, nb8 - 1), 0)),
                  pl.BlockSpec((1, 1, D), lambda b, i: (b, 0, 0)),
                  pl.BlockSpec((1, 1, D), lambda b, i: (b, 0, 0)),
                  pl.BlockSpec((D, N), const, pipeline_mode=pl.Buffered(1)),
                  pl.BlockSpec((3, conv_cols), const),
                  pl.BlockSpec((1, conv_cols), const)],
        out_specs=pl.BlockSpec((1, tm, N), lambda b, i: (b, i, 0)),
        out_shape=jax.ShapeDtypeStruct((B, L, N), F32),
        compiler_params=_cparams(("parallel", "arbitrary")),
        name="odd_in_proj",
    )(x, x, x, gs, sh, w, cw, cb.reshape(1, conv_cols))


def _even_in_kernel(x_ref, gs_ref, sh_ref, w_ref, seg_ref, cq_ref, sq_ref, ck_ref, sk_ref,
                    q_ref, klo_ref, khi_ref, vlo_ref, vhi_ref, glu_ref):
    tm = x_ref.shape[1]
    h = _norm_mod(x_ref[0], gs_ref[0], sh_ref[0]).astype(BF16)
    seg = seg_ref[...]
    lane = lax.broadcasted_iota(jnp.int32, (tm, LANES), 1)
    even = (lane & 1) == 0
    low = lane < HEAD_DIM

    def proj(col0, width):
        return jnp.dot(h, w_ref[:, col0:col0 + width], preferred_element_type=F32)

    def head_norm_rope(y, c_ref, s_ref):
        sq = y * y
        hi = sq.astype(BF16)
        lo = (sq - hi.astype(F32)).astype(BF16)
        ss = (jnp.dot(hi, seg, preferred_element_type=F32)
              + jnp.dot(lo, seg, preferred_element_type=F32))
        yn = y * lax.rsqrt(ss * (1.0 / HEAD_DIM) + NORM_EPS)
        out = []
        for j in range(2):
            c = yn[:, LANES * j:LANES * (j + 1)]
            sw = jnp.where(even, pltpu.roll(c, LANES - 1, 1), pltpu.roll(c, 1, 1))
            out.append(c * c_ref[...] + sw * s_ref[...])
        return out

    def put_padded(c, j, lo_ref, hi_ref):
        r = pltpu.roll(c, HEAD_DIM, 1)
        dt = lo_ref.dtype
        lo_ref[0, :, LANES * 2 * j:LANES * (2 * j + 1)] = jnp.where(low, c, 0.0).astype(dt)
        hi_ref[0, :, LANES * 2 * j:LANES * (2 * j + 1)] = jnp.where(low, 0.0, r).astype(dt)
        lo_ref[0, :, LANES * (2 * j + 1):LANES * (2 * j + 2)] = jnp.where(low, r, 0.0).astype(dt)
        hi_ref[0, :, LANES * (2 * j + 1):LANES * (2 * j + 2)] = jnp.where(low, 0.0, c).astype(dt)

    n_q = Q_W // 256
    n_k = KV_W // 256
    g0 = Q_W + 2 * KV_W
    y = proj(0, 256)
    for t in range(n_q + n_k):
        y_next = proj(256 * (t + 1), 256) if t + 1 < n_q + n_k else None
        if t < n_q:
            for j, c in enumerate(head_norm_rope(y, cq_ref, sq_ref)):
                q_ref[0, :, 256 * t + LANES * j:256 * t + LANES * (j + 1)] = c.astype(q_ref.dtype)
        else:
            for j, c in enumerate(head_norm_rope(y, ck_ref, sk_ref)):
                put_padded(c, 2 * (t - n_q) + j, klo_ref, khi_ref)
        y = y_next
    v = proj(Q_W + KV_W, KV_W)
    for j in range(KV_W // LANES):
        put_padded(v[:, LANES * j:LANES * (j + 1)], j, vlo_ref, vhi_ref)
    a = proj(g0, CONF_DIM)
    gate = proj(g0 + CONF_DIM, CONF_DIM)
    glu_ref[0] = a * jax.nn.sigmoid(gate)


def even_in_proj(x, gs, sh, w, seg, cq, sq, ck, sk):
    B, L, D = x.shape
    N = w.shape[1]
    tm = _tile(L, (1024, 512, 256, 128))
    row = lambda b, i: (b, i, 0)
    tab = pl.BlockSpec((tm, LANES), lambda b, i: (i, 0))
    KP = N_KV_HEADS * LANES
    kv_spec = pl.BlockSpec((1, tm, KP), row)
    kv_shape = jax.ShapeDtypeStruct((B, L, KP), BF16)
    return pl.pallas_call(
        _even_in_kernel,
        grid=(B, L // tm),
        in_specs=[pl.BlockSpec((1, tm, D), row),
                  pl.BlockSpec((1, 1, D), lambda b, i: (b, 0, 0)),
                  pl.BlockSpec((1, 1, D), lambda b, i: (b, 0, 0)),
                  pl.BlockSpec((D, N), lambda b, i: (0, 0), pipeline_mode=pl.Buffered(1)),
                  pl.BlockSpec((256, 256), lambda b, i: (0, 0)),
                  tab, tab, tab, tab],
        out_specs=[pl.BlockSpec((1, tm, Q_W), row), kv_spec, kv_spec, kv_spec, kv_spec,
                   pl.BlockSpec((1, tm, CONF_DIM), row)],
        out_shape=[jax.ShapeDtypeStruct((B, L, Q_W), BF16), kv_shape, kv_shape, kv_shape, kv_shape,
                   jax.ShapeDtypeStruct((B, L, CONF_DIM), F32)],
        compiler_params=_cparams(("parallel", "parallel")),
        name="even_in_proj",
    )(x, gs, sh, w, seg, cq, sq, ck, sk)


def _attn_kernel(q_ref, *refs, n_seg):
    o_ref = refs[4 * n_seg]
    group = N_Q_HEADS // N_KV_HEADS
    nt = (((1,), (1,)), ((), ()))

    def scores(hq):
        pr, pos = divmod(hq, 2)
        q2 = q_ref[0, :, LANES * pr:LANES * (pr + 1)]
        g = hq // group
        cols = slice(LANES * g, LANES * (g + 1))
        return [lax.dot_general(q2, refs[4 * t + pos][0, :, cols], nt, preferred_element_type=F32)
                for t in range(n_seg)]

    ss = scores(0)
    acc = None
    for hq in range(N_Q_HEADS):
        nxt = scores(hq + 1) if hq + 1 < N_Q_HEADS else None
        pr, pos = divmod(hq, 2)
        g = hq // group
        cols = slice(LANES * g, LANES * (g + 1))
        m = functools.reduce(jnp.maximum, [jnp.max(s, axis=-1, keepdims=True) for s in ss])
        ps = [jnp.exp(s - m) for s in ss]
        l = functools.reduce(jnp.add, [jnp.sum(p, axis=-1, keepdims=True) for p in ps])
        o = functools.reduce(jnp.add, [
            jnp.dot(p.astype(BF16), refs[4 * t + 2 + pos][0, :, cols], preferred_element_type=F32)
            for t, p in enumerate(ps)])
        o = o / l
        if pos == 0:
            acc = o
        else:
            o_ref[0, :, LANES * pr:LANES * (pr + 1)] = (acc + o).astype(o_ref.dtype)
        ss = nxt


def attention(q, segments):
    B, Lq, _ = q.shape
    tq = _tile(Lq, (256, 128))
    KP = N_KV_HEADS * LANES
    specs, args = [], []
    for seg in segments:
        for t in seg:
            specs.append(pl.BlockSpec((1, t.shape[1], KP), lambda b, i: (b, 0, 0)))
            args.append(t)
    return pl.pallas_call(
        functools.partial(_attn_kernel, n_seg=len(segments)),
        grid=(B, Lq // tq),
        in_specs=[pl.BlockSpec((1, tq, Q_W), lambda b, i: (b, i, 0))] + specs,
        out_specs=pl.BlockSpec((1, tq, Q_W), lambda b, i: (b, i, 0)),
        out_shape=jax.ShapeDtypeStruct((B, Lq, Q_W), BF16),
        compiler_params=_cparams(("parallel", "parallel")),
        name="attention",
    )(q, *args)


_CONV_PAD = 16


def _dwconv_kernel(x_ref, w_ref, b_ref, g_ref, beta_ref, o_ref, xp_ref, *, K, TR, ln_swish):
    L = x_ref.shape[1]
    C = x_ref.shape[2]
    half = (K - 1) // 2
    xp_ref[0:_CONV_PAD, :] = jnp.zeros((_CONV_PAD, C), F32)
    xp_ref[_CONV_PAD + L:, :] = jnp.zeros((_CONV_PAD, C), F32)
    xp_ref[_CONV_PAD:_CONV_PAD + L, :] = x_ref[0]

    groups = {}
    for k in range(K):
        q8, r8 = divmod(_CONV_PAD - half + k, 8)
        groups.setdefault(r8, []).append((k, q8))
    R = TR + 8
    CW = LANES if C % LANES == 0 else C

    def tile(t, carry):
        r0 = pl.multiple_of(t * TR, TR)
        parts = []
        for c0 in range(0, C, CW):
            acc = None
            for r8, taps in sorted(groups.items()):
                part = None
                for k, q8 in taps:
                    term = (xp_ref[pl.ds(pl.multiple_of(r0 + 8 * q8, 8), R), c0:c0 + CW]
                            * w_ref[k:k + 1, c0:c0 + CW])
                    part = term if part is None else part + term
                if r8:
                    part = pltpu.roll(part, R - r8, 0)
                part = part[:TR]
                acc = part if acc is None else acc + part
            parts.append(acc + b_ref[:, c0:c0 + CW])
        acc = parts[0] if len(parts) == 1 else jnp.concatenate(parts, axis=-1)
        if ln_swish:
            mu = jnp.mean(acc, axis=-1, keepdims=True)
            d = acc - mu
            var = jnp.mean(d * d, axis=-1, keepdims=True)
            y = d * lax.rsqrt(var + NORM_EPS) * g_ref[...] + beta_ref[...]
            acc = y * jax.nn.sigmoid(y)
        o_ref[0, pl.ds(r0, TR), :] = acc.astype(o_ref.dtype)
        return carry

    lax.fori_loop(0, L // TR, tile, 0)


def dwconv(x, w, b, g=None, beta=None, *, out_dtype=F32):
    B, L, _ = x.shape
    K, C = w.shape
    ln_swish = g is not None
    tc = C if ln_swish else _tile(C, (512, 256, 128))
    TR = _tile(L, (256, 128, 64, 32, 16, 8))
    if g is None:
        g = jnp.ones((C,), F32)
        beta = jnp.zeros((C,), F32)
    vec = pl.BlockSpec((1, tc), lambda bb, j: (0, j))
    return pl.pallas_call(
        functools.partial(_dwconv_kernel, K=K, TR=TR, ln_swish=ln_swish),
        grid=(B, C // tc),
        in_specs=[pl.BlockSpec((1, L, tc), lambda bb, j: (bb, 0, j)),
                  pl.BlockSpec((K, tc), lambda bb, j: (0, j)),
                  vec, vec, vec],
        out_specs=pl.BlockSpec((1, L, tc), lambda bb, j: (bb, 0, j)),
        out_shape=jax.ShapeDtypeStruct((B, L, C), out_dtype),
        scratch_shapes=[pltpu.VMEM((L + 2 * _CONV_PAD, tc), F32)],
        compiler_params=_cparams(("parallel", "parallel")),
        name="dwconv%d" % K,
    )(x, w, b.reshape(1, C), g.reshape(1, C), beta.reshape(1, C))


def _out_proj_kernel(x_ref, a1_ref, a2_ref, w1_ref, w2_ref, g_ref, o_ref):
    y = jnp.dot(a1_ref[0], w1_ref[...], preferred_element_type=F32)
    y = y + jnp.dot(a2_ref[0], w2_ref[...], preferred_element_type=F32)
    o_ref[0] = x_ref[0] + g_ref[0] * y


def out_proj_residual(x, a1, a2, w1, w2, gate):
    B, L, D = x.shape
    K1, K2 = a1.shape[2], a2.shape[2]
    tm = _tile(L, (2048, 1024, 512, 256, 128))
    row = lambda b, i: (b, i, 0)
    return pl.pallas_call(
        _out_proj_kernel,
        grid=(B, L // tm),
        in_specs=[pl.BlockSpec((1, tm, D), row),
                  pl.BlockSpec((1, tm, K1), row),
                  pl.BlockSpec((1, tm, K2), row),
                  pl.BlockSpec((K1, D), lambda b, i: (0, 0)),
                  pl.BlockSpec((K2, D), lambda b, i: (0, 0)),
                  pl.BlockSpec((1, 1, D), lambda b, i: (b, 0, 0))],
        out_specs=pl.BlockSpec((1, tm, D), row),
        out_shape=jax.ShapeDtypeStruct((B, L, D), F32),
        compiler_params=_cparams(("parallel", "parallel")),
        name="out_proj_residual",
    )(x, a1, a2, w1, w2, gate)


_FFN_CHUNK = 256


def _ffn_kernel(x_ref, xp_ref, xn_ref, gs_ref, sh_ref, wup_ref, cw_ref, cb_ref, wdn_ref, g5_ref,
                fg_ref, o_ref, acc_ref, *, final_norm):
    tm = x_ref.shape[1]
    FF = wdn_ref.shape[0]
    i = pl.program_id(1)
    has_prev = (i > 0).astype(F32)
    has_next = (i < pl.num_programs(1) - 1).astype(F32)
    gs = gs_ref[0]
    sh = sh_ref[0]
    x = x_ref[0]
    h = _norm_mod(x, gs, sh).astype(BF16)
    halo = jnp.concatenate([_norm_mod(xp_ref[0], gs, sh), _norm_mod(xn_ref[0], gs, sh)],
                           axis=0).astype(BF16)
    row = lax.broadcasted_iota(jnp.int32, (tm, _FFN_CHUNK), 0)
    first = row == 0
    last = row == tm - 1

    hcat = jnp.concatenate([h, halo], axis=0)

    def up(c0):
        r = jnp.dot(hcat, wup_ref[:, c0:c0 + _FFN_CHUNK], preferred_element_type=F32)
        return r[:tm], r[tm:]

    def conv3(c0, p, ph):
        prev_row = ph[7:8, :] * has_prev
        next_row = ph[8:9, :] * has_next
        pm1 = jnp.where(first, prev_row, pltpu.roll(p, 1, 0))
        pp1 = jnp.where(last, next_row, pltpu.roll(p, tm - 1, 0))
        return (cw_ref[0:1, c0:c0 + _FFN_CHUNK] * pm1 + cw_ref[1:2, c0:c0 + _FFN_CHUNK] * p
                + cw_ref[2:3, c0:c0 + _FFN_CHUNK] * pp1 + cb_ref[:, c0:c0 + _FFN_CHUNK])

    n_chunks = FF // _FFN_CHUNK
    ahead = 3
    ups = [(up(c * _FFN_CHUNK), up(FF + c * _FFN_CHUNK)) for c in range(min(ahead, n_chunks))]
    for c in range(n_chunks):
        if c + ahead < n_chunks:
            ups.append((up((c + ahead) * _FFN_CHUNK), up(FF + (c + ahead) * _FFN_CHUNK)))
        val = conv3(c * _FFN_CHUNK, *ups[c][0])
        gate = conv3(FF + c * _FFN_CHUNK, *ups[c][1])
        ups[c] = None
        act = (gate * jax.nn.sigmoid(gate) * val).astype(BF16)
        part = jnp.dot(act, wdn_ref[c * _FFN_CHUNK:(c + 1) * _FFN_CHUNK, :],
                       preferred_element_type=F32)
        if c == 0:
            acc_ref[...] = part
        else:
            acc_ref[...] += part
    y = x + g5_ref[0] * acc_ref[...]
    if final_norm:
        y = y * lax.rsqrt(jnp.mean(y * y, axis=-1, keepdims=True) + NORM_EPS) * fg_ref[...]
    o_ref[0] = y


def conv_ffn_residual(x, gs, sh, w_up, cw, cb, w_dn, gate, final_g=None):
    B, L, D = x.shape
    FF = w_dn.shape[0]
    assert FF % _FFN_CHUNK == 0
    tm = _tile(L, (256, 128))
    nb8 = L // 8
    t8 = tm // 8
    final_norm = final_g is not None
    if final_g is None:
        final_g = jnp.ones((D,), F32)
    row = lambda b, i: (b, i, 0)
    const = lambda b, i: (0, 0)
    return pl.pallas_call(
        functools.partial(_ffn_kernel, final_norm=final_norm),
        grid=(B, L // tm),
        in_specs=[pl.BlockSpec((1, tm, D), row),
                  pl.BlockSpec((1, 8, D), lambda b, i: (b, jnp.maximum(i * t8 - 1, 0), 0)),
                  pl.BlockSpec((1, 8, D), lambda b, i: (b, jnp.minimum((i + 1) * t8, nb8 - 1), 0)),
                  pl.BlockSpec((1, 1, D), lambda b, i: (b, 0, 0)),
                  pl.BlockSpec((1, 1, D), lambda b, i: (b, 0, 0)),
                  pl.BlockSpec((D, 2 * FF), const, pipeline_mode=pl.Buffered(1)),
                  pl.BlockSpec((3, 2 * FF), const),
                  pl.BlockSpec((1, 2 * FF), const),
                  pl.BlockSpec((FF, D), const, pipeline_mode=pl.Buffered(1)),
                  pl.BlockSpec((1, 1, D), lambda b, i: (b, 0, 0)),
                  pl.BlockSpec((1, D), const)],
        out_specs=pl.BlockSpec((1, tm, D), row),
        out_shape=jax.ShapeDtypeStruct((B, L, D), F32),
        scratch_shapes=[pltpu.VMEM((tm, D), F32)],
        compiler_params=_cparams(("parallel", "arbitrary")),
        name="conv_ffn",
    )(x, x, x, gs, sh, w_up, cw, cb.reshape(1, 2 * FF), w_dn, gate, final_g.reshape(1, D))


def _hy_filter_kernel(feat_ref, w1_ref, b1_ref, w2_ref, b2_ref, w3_ref, fr_ref, dec_ref,
                      hs_ref, hd_ref, nyq_ref):
    L = feat_ref.shape[0]
    fr = fr_ref[...]
    z = jnp.dot(feat_ref[...].astype(BF16), w1_ref[...].astype(BF16), preferred_element_type=F32)
    hdn = jnp.sin(fr * (z + b1_ref[...]))
    z = jnp.dot(hdn.astype(BF16), w2_ref[...].astype(BF16), preferred_element_type=F32)
    hdn = jnp.sin(fr * (z + b2_ref[...])).astype(BF16)
    row = lax.broadcasted_iota(jnp.int32, (L, HY_DIM), 0)
    not_first = (row > 0).astype(F32)
    sign = jnp.where((row & 1) == 0, 1.0, -1.0).astype(F32)
    dec = dec_ref[...]
    n_order = w3_ref.shape[1] // (2 * HY_DIM)
    for o in range(n_order):
        c0 = o * 2 * HY_DIM
        hf = jnp.dot(hdn, w3_ref[:, c0:c0 + HY_DIM].astype(BF16), preferred_element_type=F32) * dec
        hb = (jnp.dot(hdn, w3_ref[:, c0 + HY_DIM:c0 + 2 * HY_DIM].astype(BF16),
                      preferred_element_type=F32) * dec * not_first)
        inv = 1.0 / (jnp.sum(jnp.abs(hf), axis=0, keepdims=True)
                     + jnp.sum(jnp.abs(hb), axis=0, keepdims=True))
        hs = (hf + hb) * inv
        hd = (hf - hb) * inv
        hs_ref[:, o * HY_DIM:(o + 1) * HY_DIM] = hs
        hd_ref[:, o * HY_DIM:(o + 1) * HY_DIM] = hd
        nyq_ref[:, o * HY_DIM:(o + 1) * HY_DIM] = jnp.sum(hs * sign, axis=0, keepdims=True)


def hyena_filter_taps(L, w1, b1, w2, b2, w3, freq):
    emb = w1.shape[0]
    bands = (emb - 1) // 2
    t = jnp.linspace(0.0, 1.0, L, dtype=F32)[:, None]
    fb = jnp.linspace(1e-4, bands - 1, bands, dtype=F32)
    wpos = (2.0 * math.pi / L) * jnp.arange(L, dtype=F32)[:, None]
    feats = jnp.concatenate([t, jnp.cos(fb * wpos), -jnp.sin(fb * wpos)], axis=-1)
    feats = jnp.pad(feats, ((0, 0), (0, LANES - emb)))
    w1p = jnp.pad(w1, ((0, LANES - emb), (0, 0)))
    deltas = jnp.abs(jnp.linspace(math.log(C_DECAY_TARGET) / C_SLOW_DECAY,
                                  math.log(C_DECAY_TARGET) / C_FAST_DECAY, HY_DIM, dtype=F32))
    dec = jnp.exp(-t * deltas)
    hid = w1.shape[1]
    NO = w3.shape[1] // 2
    full = lambda shape: pl.BlockSpec(shape, lambda: tuple(0 for _ in shape))
    return pl.pallas_call(
        _hy_filter_kernel,
        in_specs=[full((L, LANES)), full((LANES, hid)), full((1, hid)), full((hid, hid)),
                  full((1, hid)), full((hid, 2 * NO)), full((1, hid)), full((L, HY_DIM))],
        out_specs=[full((L, NO)), full((L, NO)), full((1, NO))],
        out_shape=[jax.ShapeDtypeStruct((L, NO), F32), jax.ShapeDtypeStruct((L, NO), F32),
                   jax.ShapeDtypeStruct((1, NO), F32)],
        compiler_params=pltpu.CompilerParams(vmem_limit_bytes=V7X_VMEM_LIMIT),
        name="hyena_filter",
    )(feats, w1p, b1.reshape(1, hid), w2, b2.reshape(1, hid), w3, freq.reshape(1, hid), dec)


_DFT_SPLIT = 64


def _dft_tables(L):
    N = 2 * L
    n = jnp.arange(L, dtype=jnp.int32)

    def cos_sin(mult):
        ang = ((mult[:, None] * n[None, :]) & (N - 1)).astype(F32) * (2.0 * math.pi / N)
        return jnp.cos(ang), jnp.sin(ang)

    ca, sa = cos_sin(jnp.arange(L // _DFT_SPLIT, dtype=jnp.int32) * _DFT_SPLIT)
    cb, sb = cos_sin(jnp.arange(_DFT_SPLIT, dtype=jnp.int32))
    nyq = jnp.where((n & 1) == 0, 1.0, -1.0).astype(F32)
    re = (ca[:, None, :] * cb[None] - sa[:, None, :] * sb[None]).reshape(L, L)
    im = -(sa[:, None, :] * cb[None] + ca[:, None, :] * sb[None]).reshape(L, L)
    im = jnp.where(jnp.arange(L)[:, None] == 0, nyq[None, :], im)
    f = jnp.concatenate([re, im], axis=0).astype(BF16)
    cat, sat, cbt, sbt = ca.T, sa.T, cb.T, sb.T
    re_t = (cat[:, :, None] * cbt[:, None, :] - sat[:, :, None] * sbt[:, None, :]).reshape(L, L)
    im_t = -(sat[:, :, None] * cbt[:, None, :] + cat[:, :, None] * sbt[:, None, :]).reshape(L, L)
    im_t = jnp.where(jnp.arange(L)[None, :] == 0, nyq[:, None], im_t)
    g = jnp.concatenate([re_t, im_t], axis=1).astype(BF16)
    return f, g


def _hy_spec_kernel(fre_ref, fim_ref, hs_ref, hd_ref, nyq_ref, a_ref, b_ref):
    kb = fre_ref.shape[0]
    L = fre_ref.shape[1]
    NO = hs_ref.shape[1]
    krow = lax.broadcasted_iota(jnp.int32, (kb, NO), 0) + pl.program_id(0) * kb
    dc = krow == 0
    scale = jnp.where(dc, 1.0, 2.0).astype(F32) * (1.0 / (2 * L))
    hr = jnp.dot(fre_ref[...], hs_ref[...].astype(BF16), preferred_element_type=F32)
    hi = jnp.dot(fim_ref[...], hd_ref[...].astype(BF16), preferred_element_type=F32)
    a_ref[...] = hr * scale
    b_ref[...] = jnp.where(dc, nyq_ref[...], hi) * scale


def hyena_spectrum(F, hs, hd, nyq):
    L, NO = hs.shape
    kb = _tile(L, (256, 128))
    nk = L // kb
    return pl.pallas_call(
        _hy_spec_kernel,
        grid=(nk,),
        in_specs=[pl.BlockSpec((kb, L), lambda i: (i, 0)),
                  pl.BlockSpec((kb, L), lambda i: (i + nk, 0)),
                  pl.BlockSpec((L, NO), lambda i: (0, 0)),
                  pl.BlockSpec((L, NO), lambda i: (0, 0)),
                  pl.BlockSpec((1, NO), lambda i: (0, 0))],
        out_specs=[pl.BlockSpec((kb, NO), lambda i: (i, 0)),
                   pl.BlockSpec((kb, NO), lambda i: (i, 0))],
        out_shape=[jax.ShapeDtypeStruct((L, NO), F32), jax.ShapeDtypeStruct((L, NO), F32)],
        compiler_params=_cparams(("arbitrary",)),
        name="hyena_spectrum",
    )(F, F, hs, hd, nyq)


def _hy_conv_kernel(z_ref, gate_ref, fre_ref, fim_ref, gre_ref, gim_ref, a_ref, b_ref, skip_ref,
                    o_ref, zb_ref, y_ref):
    kb = fre_ref.shape[0]
    C = z_ref.shape[2]
    kstep = pl.program_id(1)

    @pl.when(kstep == 0)
    def _():
        zb_ref[...] = z_ref[0].astype(BF16)
        y_ref[...] = jnp.zeros(y_ref.shape, F32)

    NH = 2
    hb = kb // NH
    CH = C // 2
    blocks = [(hf, ch) for hf in range(NH) for ch in range(2)]
    fwd = []
    for hf, ch in blocks:
        rows = slice(hf * hb, (hf + 1) * hb)
        zc = zb_ref[:, ch * CH:(ch + 1) * CH]
        fwd.append((jnp.dot(fre_ref[rows, :], zc, preferred_element_type=F32),
                    jnp.dot(fim_ref[rows, :], zc, preferred_element_type=F32)))
    for (hf, ch), (ur, ui) in zip(blocks, fwd):
        rows = slice(hf * hb, (hf + 1) * hb)
        cols = slice(ch * CH, (ch + 1) * CH)
        a = a_ref[rows, cols]
        b = b_ref[rows, cols]
        krow = lax.broadcasted_iota(jnp.int32, (hb, CH), 0) + (kstep * kb + hf * hb)
        dc = krow == 0
        vr = jnp.where(dc, ur * a, ur * a - ui * b)
        vi = jnp.where(dc, ui * b, ur * b + ui * a)
        y_ref[:, cols] += (jnp.dot(gre_ref[:, rows], vr.astype(BF16), preferred_element_type=F32)
                           + jnp.dot(gim_ref[:, rows], vi.astype(BF16), preferred_element_type=F32))

    @pl.when(kstep == pl.num_programs(1) - 1)
    def _():
        o_ref[0] = (gate_ref[0] * (y_ref[...] + skip_ref[...] * z_ref[0])).astype(o_ref.dtype)


def hyena_conv(z_src, z_col, gate_src, gate_col, F, G, A, Bt, spec_col, skip, out_dtype):
    B, L, _ = z_src.shape
    C = HY_DIM
    kb = _tile(L, (512, 256))
    nk = L // kb
    return pl.pallas_call(
        _hy_conv_kernel,
        grid=(B, nk),
        in_specs=[pl.BlockSpec((1, L, C), lambda b, k: (b, 0, z_col)),
                  pl.BlockSpec((1, L, C), lambda b, k: (b, 0, gate_col)),
                  pl.BlockSpec((kb, L), lambda b, k: (k, 0)),
                  pl.BlockSpec((kb, L), lambda b, k: (k + nk, 0)),
                  pl.BlockSpec((L, kb), lambda b, k: (0, k)),
                  pl.BlockSpec((L, kb), lambda b, k: (0, k + nk)),
                  pl.BlockSpec((kb, C), lambda b, k: (k, spec_col)),
                  pl.BlockSpec((kb, C), lambda b, k: (k, spec_col)),
                  pl.BlockSpec((1, C), lambda b, k: (0, 0))],
        out_specs=pl.BlockSpec((1, L, C), lambda b, k: (b, 0, 0)),
        out_shape=jax.ShapeDtypeStruct((B, L, C), out_dtype),
        scratch_shapes=[pltpu.VMEM((L, C), BF16), pltpu.VMEM((L, C), F32)],
        compiler_params=_cparams(("parallel", "arbitrary")),
        name="hyena_conv",
    )(z_src, gate_src, F, F, G, G, A, Bt, skip.reshape(1, C))


HG_GROUP = 128


def _hgrn2_kernel(ql_ref, ffl_ref, fbl_ref, il_ref, gl_ref, qc_ref, ffc_ref, fbc_ref, ic_ref,
                  lb_ref, gn_ref, o_ref, vb_ref, qdf_ref, qdb_ref, qgf_ref, qgb_ref,
                  kuf_ref, kub_ref, kgf_ref, kgb_ref, kdf_ref, kdb_ref, decf_ref, decb_ref,
                  oif_ref, oib_ref, stf_ref, stb_ref):
    C = HG_CHUNK
    GR = HG_GROUP
    G2 = 2 * C
    L = ql_ref.shape[1]
    Lc = qc_ref.shape[1]
    r = lax.broadcasted_iota(jnp.int32, (GR, GR), 0)
    c = lax.broadcasted_iota(jnp.int32, (GR, GR), 1)
    same = (r // C) == (c // C)
    same_group = (r // G2) == (c // G2)
    row_odd = ((lax.broadcasted_iota(jnp.int32, (GR, HG_DIM), 0) // C) & 1) == 1
    dirs = []
    for d, (incl, excl) in enumerate(((c <= r, c > r), (c >= r, c < r))):
        mask = same & incl
        cross = same_group & jnp.logical_not(same) & incl
        second = row_odd if d == 0 else jnp.logical_not(row_odd)
        dirs.append((lb_ref[d:d + 1, :], mask.astype(BF16), (same & excl).astype(BF16), mask, cross,
                     second))
    f_refs_lat = (ffl_ref, fbl_ref)
    f_refs_ctx = (ffc_ref, fbc_ref)
    qd_refs = (qdf_ref, qdb_ref)
    qg_refs = (qgf_ref, qgb_ref)
    ku_refs = (kuf_ref, kub_ref)
    kg_refs = (kgf_ref, kgb_ref)
    kd_refs = (kdf_ref, kdb_ref)
    dec_refs = (decf_ref, decb_ref)
    oi_refs = (oif_ref, oib_ref)
    st_refs = (stf_ref, stb_ref)
    nt = (((1,), (1,)), ((), ()))

    def gates(q_ref, f_refs, i_ref, n_rows, row_off, with_out):
        def body(gi, carry):
            r0 = pl.multiple_of(gi * GR, GR)
            ro = pl.multiple_of(row_off + gi * GR, GR)
            q = q_ref[0, pl.ds(r0, GR), :]
            vb_ref[pl.ds(ro, GR), :] = i_ref[0, pl.ds(r0, GR), :].astype(BF16)
            for d, (lbd, t_in, t_ex, _, _, second) in enumerate(dirs):
                f = lbd + (1.0 - lbd) * jax.nn.sigmoid(f_refs[d][0, pl.ds(r0, GR), :])
                k = 1.0 - f
                lg = jnp.log(f)
                hi = lg.astype(BF16)
                lo = (lg - hi.astype(F32)).astype(BF16)
                b = (jnp.dot(t_in, hi, preferred_element_type=F32)
                     + jnp.dot(t_in, lo, preferred_element_type=F32))
                rest = (jnp.dot(t_ex, hi, preferred_element_type=F32)
                        + jnp.dot(t_ex, lo, preferred_element_type=F32))
                tot = b + rest
                other = jnp.where(row_odd, pltpu.roll(tot, C, 0), pltpu.roll(tot, GR - C, 0))
                qd_refs[d][pl.ds(ro, GR), :] = (q * jnp.exp(b)).astype(BF16)
                qg_refs[d][pl.ds(ro, GR), :] = (
                    q * jnp.exp(b + jnp.where(second, other, 0.0))).astype(BF16)
                ku_refs[d][pl.ds(ro, GR), :] = (k * jnp.exp(rest)).astype(BF16)
                kg_refs[d][pl.ds(ro, GR), :] = (
                    k * jnp.exp(rest + jnp.where(second, 0.0, other))).astype(BF16)
                dec_refs[d][pl.ds(ro, GR), :] = tot + other
                if with_out:
                    kd_refs[d][pl.ds(r0, GR), :] = (k * jnp.exp(-b)).astype(BF16)
            return carry

        lax.fori_loop(0, n_rows // GR, body, 0, unroll=4)

    def intra():
        U = 4 if (L // GR) % 4 == 0 else 1

        def body(gi, carry):
            jobs = []
            for u in range(U):
                r0 = pl.multiple_of((gi * U + u) * GR, GR)
                ro = pl.multiple_of(Lc + (gi * U + u) * GR, GR)
                for d in range(2):
                    qd = qd_refs[d][pl.ds(ro, GR), :]
                    sc = lax.dot_general(qd, kd_refs[d][pl.ds(r0, GR), :], nt,
                                         preferred_element_type=F32)
                    sx = lax.dot_general(qd, ku_refs[d][pl.ds(ro, GR), :], nt,
                                         preferred_element_type=F32)
                    jobs.append((d, r0, ro, sc, sx))
            for d, r0, ro, sc, sx in jobs:
                sc = jnp.where(dirs[d][3], sc, jnp.where(dirs[d][4], sx, 0.0)).astype(BF16)
                oi_refs[d][pl.ds(r0, GR), :] = jnp.dot(sc, vb_ref[pl.ds(ro, GR), :],
                                                       preferred_element_type=F32)
            return carry

        lax.fori_loop(0, L // (GR * U), body, 0)

    def recurrence(n_rows, row_off, keep, carry):
        n = n_rows // G2

        def body(j, states):
            new = []
            for d, st in enumerate(states):
                jj = j if d == 0 else n - 1 - j
                ro = pl.multiple_of(row_off + jj * G2, G2)
                if keep:
                    st_refs[d][pl.ds(pl.multiple_of(jj * HG_DIM, HG_DIM), HG_DIM), :] = st.astype(BF16)
                upd = lax.dot_general(vb_ref[pl.ds(ro, G2), :], kg_refs[d][pl.ds(ro, G2), :],
                                      (((0,), (0,)), ((), ())), preferred_element_type=F32)
                new.append(st * jnp.exp(dec_refs[d][pl.ds(ro, 1), :]) + upd)
            return tuple(new)

        return lax.fori_loop(0, n, body, carry, unroll=32)

    def inter():
        def body(j, carry):
            r0 = pl.multiple_of(j * G2, G2)
            ro = pl.multiple_of(Lc + j * G2, G2)
            so = pl.multiple_of(j * HG_DIM, HG_DIM)
            for d in range(2):
                o = lax.dot_general(qg_refs[d][pl.ds(ro, G2), :], st_refs[d][pl.ds(so, HG_DIM), :], nt,
                                    preferred_element_type=F32)
                oi_refs[d][pl.ds(r0, G2), :] += o
            return carry

        lax.fori_loop(0, L // G2, body, 0, unroll=32)

    gates(qc_ref, f_refs_ctx, ic_ref, Lc, 0, False)
    gates(ql_ref, f_refs_lat, il_ref, L, Lc, True)
    intra()
    zero = jnp.zeros((HG_DIM, HG_DIM), F32)
    carry = recurrence(Lc, 0, False, (zero, zero))
    recurrence(L, Lc, True, carry)
    inter()
    o = oif_ref[...] + oib_ref[...]
    o = o * lax.rsqrt(jnp.mean(o * o, axis=-1, keepdims=True) + NORM_EPS) * gn_ref[...]
    g = gl_ref[0]
    o_ref[0] = (o * (g * jax.nn.sigmoid(g))).astype(o_ref.dtype)


def hgrn2_mixer(p_lat, lat_col0, p_ctx, ctx_col0, lb, gn_g):
    B, L, _ = p_lat.shape
    Lc = p_ctx.shape[1]
    lc0 = lat_col0 // HG_DIM
    cc0 = ctx_col0 // HG_DIM

    def lat(seg):
        return pl.BlockSpec((1, L, HG_DIM), lambda b, h: (b, 0, lc0 + HG_HEADS * seg + h))

    def ctx(seg):
        return pl.BlockSpec((1, Lc, HG_DIM), lambda b, h: (b, 0, cc0 + HG_HEADS * seg + h))

    return pl.pallas_call(
        _hgrn2_kernel,
        grid=(B, HG_HEADS),
        in_specs=[lat(0), lat(1), lat(2), lat(3), lat(4), ctx(0), ctx(1), ctx(2), ctx(3),
                  pl.BlockSpec((2, HG_DIM), lambda b, h: (0, h)),
                  pl.BlockSpec((1, HG_DIM), lambda b, h: (0, 0))],
        out_specs=pl.BlockSpec((1, L, HG_DIM), lambda b, h: (b, 0, h)),
        out_shape=jax.ShapeDtypeStruct((B, L, HG_W), BF16),
        scratch_shapes=([pltpu.VMEM((Lc + L, HG_DIM), BF16)] * 9
                        + [pltpu.VMEM((L, HG_DIM), BF16)] * 2
                        + [pltpu.VMEM((Lc + L, HG_DIM), F32)] * 2
                        + [pltpu.VMEM((L, HG_DIM), F32)] * 2
                        + [pltpu.VMEM((L // (2 * HG_CHUNK) * HG_DIM, HG_DIM), BF16)] * 2),
        compiler_params=_cparams(("parallel", "parallel")),
        name="hgrn2",
    )(p_lat, p_lat, p_lat, p_lat, p_lat, p_ctx, p_ctx, p_ctx, p_ctx, lb, gn_g.reshape(1, HG_DIM))


def _rope_tables(L, gain, scale, rotate):
    g2 = jnp.tile(gain.astype(F32), LANES // HEAD_DIM)[None, :] * scale
    gsw = g2.reshape(1, LANES // 2, 2)[:, :, ::-1].reshape(1, LANES)
    if not rotate:
        return jnp.broadcast_to(g2, (L, LANES)), jnp.zeros((L, LANES), F32)
    rows = L // GRID_W
    row = jnp.repeat(jnp.arange(rows, dtype=F32), GRID_W)
    col = jnp.tile(jnp.arange(GRID_W, dtype=F32), rows)
    axis_dim = HEAD_DIM // 2
    inv_freq = ROPE_THETA ** (-jnp.arange(0, axis_dim, 2, dtype=F32) / axis_dim)
    ang = jnp.concatenate([row[:, None] * inv_freq, col[:, None] * inv_freq], axis=-1)
    cos = jnp.tile(jnp.repeat(jnp.cos(ang), 2, axis=-1), (1, LANES // HEAD_DIM))
    sin = jnp.tile(jnp.repeat(jnp.sin(ang), 2, axis=-1), (1, LANES // HEAD_DIM))
    sign = jnp.tile(jnp.array([-1.0, 1.0], F32), LANES // 2)[None, :]
    return cos * g2, sin * sign * gsw


def _split6(m):
    return jnp.split(m, 6, axis=-1)


def kernel(x, c, ctx, c_ctx, ada_w, ada_b, norm1_g, norm2_g, final_g, ev_w_in, ev_q_gain, ev_k_gain,
           ev_conv_w, ev_conv_b, ev_ln_g, ev_ln_b, ev_w_out, od_w_in, od_short_w, od_short_b,
           od_filt_w1, od_filt_b1, od_filt_w2, od_filt_b2, od_filt_w3, od_filt_freq, od_hyena_skip,
           od_lower_bound, od_gnorm_g, od_w_out, ffn_w_up, ffn_conv_w, ffn_conv_b, ffn_w_down):
    B, L, D = x.shape
    Lc = ctx.shape[1]
    depth = ada_w.shape[0]
    assert depth == 2, "layer 0 (attention + conformer) followed by layer 1 (hyena + hgrn2)"

    lb_soft = jax.nn.softmax(od_lower_bound.astype(F32), axis=0)
    lower_bounds = jnp.cumsum(lb_soft, axis=0) - lb_soft[0:1]

    cond = jnp.concatenate([c, c_ctx[None, :], jnp.zeros((7, D), F32)], axis=0)

    def mods(l):
        m = ada_modulation(cond, ada_w[l], ada_b[l])
        ml = [t[:, None, :] for t in _split6(m[:B])]
        mc = [jnp.broadcast_to(t[None], (B, 1, D)) for t in _split6(m[B:B + 1])]
        return ml, mc

    def ffn(xx, mm, l, final=None):
        return conv_ffn_residual(xx, norm2_g[l] * (1.0 + mm[4]), mm[3], w_up[l],
                                 ffn_conv_w[l], ffn_conv_b[l], w_down[l], mm[5], final)

    w_up = [layer_weight_bf16(ffn_w_up, l) for l in range(depth)]
    w_down = [layer_weight_bf16(ffn_w_down, l) for l in range(depth)]

    ml, mc = mods(0)
    w_in = layer_weight_bf16(ev_w_in, 0)
    seg = jnp.kron(jnp.eye(256 // HEAD_DIM, dtype=F32), jnp.ones((HEAD_DIM, HEAD_DIM), F32)).astype(BF16)
    scale = HEAD_DIM ** -0.5
    cq_l, sq_l = _rope_tables(L, ev_q_gain[0], scale, True)
    ck_l, sk_l = _rope_tables(L, ev_k_gain[0], 1.0, True)
    cq_c, sq_c = _rope_tables(Lc, ev_q_gain[0], scale, False)
    ck_c, sk_c = _rope_tables(Lc, ev_k_gain[0], 1.0, False)
    q_l, *kv_l, glu_l = even_in_proj(x, norm1_g[0] * (1.0 + ml[1]), ml[0], w_in, seg,
                                     cq_l, sq_l, ck_l, sk_l)
    q_c, *kv_c, glu_c = even_in_proj(ctx, norm1_g[0] * (1.0 + mc[1]), mc[0], w_in, seg,
                                     cq_c, sq_c, ck_c, sk_c)
    attn_l = attention(q_l, [kv_c, kv_l])
    attn_c = attention(q_c, [kv_c])
    conv_l = dwconv(glu_l, ev_conv_w[0], ev_conv_b[0], ev_ln_g[0], ev_ln_b[0], out_dtype=BF16)
    conv_c = dwconv(glu_c, ev_conv_w[0], ev_conv_b[0], ev_ln_g[0], ev_ln_b[0], out_dtype=BF16)
    wo = layer_weight_bf16(ev_w_out, 0)
    wo1 = wo[:Q_W]
    wo2 = wo[Q_W:]
    x = out_proj_residual(x, attn_l, conv_l, wo1, wo2, ml[2])
    ctx = out_proj_residual(ctx, attn_c, conv_c, wo1, wo2, mc[2])
    x = ffn(x, ml, 0)
    ctx = ffn(ctx, mc, 0)

    ml, mc = mods(1)
    w_in = layer_weight_bf16(od_w_in, 0)
    p_lat = odd_in_proj(x, norm1_g[1] * (1.0 + ml[1]), ml[0], w_in, od_short_w[0], od_short_b[0])
    p_ctx = norm_mod_matmul(ctx, norm1_g[1] * (1.0 + mc[1]), mc[0],
                            layer_weight_bf16(od_w_in, 0, HY_IN))
    d_lat = hgrn2_mixer(p_lat, HY_IN, p_ctx, 0, lower_bounds[1], od_gnorm_g[0])

    hs, hd, nyq = hyena_filter_taps(L, od_filt_w1[0], od_filt_b1[0], od_filt_w2[0], od_filt_b2[0],
                                    od_filt_w3[0], od_filt_freq[0])
    F, G = _dft_tables(L)
    A, Bt = hyena_spectrum(F, hs, hd, nyq)
    z1 = hyena_conv(p_lat, 0, p_lat, 1, F, G, A, Bt, 0, od_hyena_skip[0][0], F32)
    c_lat = hyena_conv(z1, 0, p_lat, 2, F, G, A, Bt, 1, od_hyena_skip[0][1], BF16)

    wo = layer_weight_bf16(od_w_out, 0)
    x = out_proj_residual(x, c_lat, d_lat, wo[:HY_DIM], wo[HY_DIM:], ml[2])
    return ffn(x, ml, 1, final_g)
```
